```python
import math
import jax, jax.numpy as jnp
from jax import lax
import numpy as np

D_MODEL = 4096
BATCH = 4
SEQ = 4096
DEPTH = 4

N_META = 16
BLOCK_Q = 128
EPS = 1e-6
FOX_HEADS = 16
FOX_DIM = 128
MLA_HEADS = 16
MLA_Q_RANK = 896
MLA_KV_RANK = 320
MLA_NOPE_DIM = 128
MLA_ROPE_DIM = 64
MLA_V_DIM = 128
ROPE_THETA = 10000.0
DIFF_HEAD_DIM = 128
DIFF_HEADS = D_MODEL // (2 * DIFF_HEAD_DIM)
REL_BUCKETS = 32
REL_MAX_DIST = 128
D_FF = ((8 * D_MODEL + 3 * 256 - 1) // (3 * 256)) * 256

N_EVEN = (DEPTH + 1) // 2
N_ODD = DEPTH // 2
EVEN_SPLITS = (FOX_HEADS * FOX_DIM, FOX_HEADS * FOX_DIM, FOX_HEADS * FOX_DIM, FOX_HEADS,
               MLA_Q_RANK, MLA_KV_RANK, MLA_ROPE_DIM)
EVEN_IN = sum(EVEN_SPLITS)
EVEN_OUT = FOX_HEADS * FOX_DIM + MLA_HEADS * MLA_V_DIM
DIFF_IN = 3 * DIFF_HEADS * 2 * DIFF_HEAD_DIM
DIFF_OUT = DIFF_HEADS * 2 * DIFF_HEAD_DIM

kernel_name = "fox_mla_diff_hybrid_trunk"


def rms_norm(x, g):
    xf = x.astype(jnp.float32)
    y = xf * lax.rsqrt(jnp.mean(xf * xf, axis=-1, keepdims=True) + EPS)
    return (y * g.astype(jnp.float32)).astype(x.dtype)


def to_heads(t, n, d):
    b, l, _ = t.shape
    return t.reshape(b, l, n, d).transpose(0, 2, 1, 3)


def from_heads(t):
    b, h, l, d = t.shape
    return t.transpose(0, 2, 1, 3).reshape(b, l, h * d)


def query_blocks(L):
    return [(0, N_META)] + [(lo, min(lo + BLOCK_Q, L)) for lo in range(N_META, L, BLOCK_Q)]


def causal_probs(q, k, lo, hi, scale, bias=None):
    logits = jnp.einsum("bhqd,bhkd->bhqk", q[:, :, lo:hi], k[:, :, :hi]).astype(jnp.float32) * scale
    if bias is not None:
        logits = logits + bias
    t = jnp.arange(lo, hi)[:, None]
    s = jnp.arange(hi)[None, :]
    logits = jnp.where(s <= t, logits, -jnp.inf)
    return jax.nn.softmax(logits, axis=-1)


def rope_tables(L):
    inv = ROPE_THETA ** (-jnp.arange(0, MLA_ROPE_DIM, 2, dtype=jnp.float32) / MLA_ROPE_DIM)
    ang = jnp.arange(L, dtype=jnp.float32)[:, None] * inv[None, :]
    return jnp.cos(ang), jnp.sin(ang)


def apply_rope(x, cos, sin):
    x1, x2 = jnp.split(x.astype(jnp.float32), 2, axis=-1)
    return jnp.concatenate([x1 * cos - x2 * sin, x1 * sin + x2 * cos], axis=-1).astype(x.dtype)


def t5_bucket(dist):
    max_exact = REL_BUCKETS // 2
    d = jnp.maximum(dist, 1).astype(jnp.float32)
    large = max_exact + (jnp.log(d / max_exact) / math.log(REL_MAX_DIST / max_exact)
                         * (REL_BUCKETS - max_exact)).astype(jnp.int32)
    large = jnp.minimum(large, REL_BUCKETS - 1)
    return jnp.where(dist < max_exact, dist, large)


def rel_bias_block(rel_bias, lo, hi):
    dist = jnp.maximum(jnp.arange(lo, hi)[:, None] - jnp.arange(hi)[None, :], 0)
    return rel_bias.astype(jnp.float32)[t5_bucket(dist)].transpose(2, 0, 1)[None]


def fox_mla_mixer(h, w_in, b_f, q_norm_g, kv_norm_g, w_uq, w_ukv, w_out):
    b, L, _ = h.shape
    offsets = [int(o) for o in np.cumsum(EVEN_SPLITS)[:-1]]
    proj = jnp.einsum("bld,de->ble", h, w_in)
    q_f, k_f, v_f, f_logit, c_q, c_kv, k_r = jnp.split(proj, offsets, axis=-1)
    qf = to_heads(q_f, FOX_HEADS, FOX_DIM)
    kf = to_heads(k_f, FOX_HEADS, FOX_DIM)
    vf = to_heads(v_f, FOX_HEADS, FOX_DIM)
    log_f = jax.nn.log_sigmoid(f_logit.astype(jnp.float32) + b_f.astype(jnp.float32))
    cum = jnp.cumsum(log_f, axis=1).transpose(0, 2, 1)
    cos, sin = rope_tables(L)
    q = to_heads(rms_norm(c_q, q_norm_g) @ w_uq, MLA_HEADS, MLA_NOPE_DIM + MLA_ROPE_DIM)
    q_m = jnp.concatenate([q[..., :MLA_NOPE_DIM], apply_rope(q[..., MLA_NOPE_DIM:], cos, sin)], axis=-1)
    kv = to_heads(rms_norm(c_kv, kv_norm_g) @ w_ukv, MLA_HEADS, MLA_NOPE_DIM + MLA_V_DIM)
    k_rope = apply_rope(k_r[:, None], cos, sin)
    k_m = jnp.concatenate([kv[..., :MLA_NOPE_DIM],
                           jnp.broadcast_to(k_rope, (b, MLA_HEADS, L, MLA_ROPE_DIM))], axis=-1)
    v_m = kv[..., MLA_NOPE_DIM:]
    fox_scale = FOX_DIM ** -0.5
    mla_scale = (MLA_NOPE_DIM + MLA_ROPE_DIM) ** -0.5
    outs = []
    for lo, hi in query_blocks(L):
        decay = cum[:, :, lo:hi, None] - cum[:, :, None, :hi]
        p_f = causal_probs(qf, kf, lo, hi, fox_scale, decay)
        o_f = jnp.einsum("bhqk,bhkd->bhqd", p_f.astype(vf.dtype), vf[:, :, :hi])
        p_m = causal_probs(q_m, k_m, lo, hi, mla_scale)
        o_m = jnp.einsum("bhqk,bhkd->bhqd", p_m.astype(v_m.dtype), v_m[:, :, :hi])
        outs.append(jnp.concatenate([o_f, o_m], axis=1))
    o = jnp.concatenate(outs, axis=2)
    return from_heads(o) @ w_out


def diff_mixer(h, w_in, lam, subln_g, w_out, rel_bias, lam_init):
    b, L, _ = h.shape
    q, k, v = jnp.split(jnp.einsum("bld,de->ble", h, w_in), 3, axis=-1)
    q = q.reshape(b, L, DIFF_HEADS, 2, DIFF_HEAD_DIM).transpose(0, 2, 3, 1, 4)
    k = k.reshape(b, L, DIFF_HEADS, 2, DIFF_HEAD_DIM).transpose(0, 2, 3, 1, 4)
    v = to_heads(v, DIFF_HEADS, 2 * DIFF_HEAD_DIM)
    lf = lam.astype(jnp.float32)
    lam_full = jnp.exp(jnp.sum(lf[0] * lf[1])) - jnp.exp(jnp.sum(lf[2] * lf[3])) + lam_init
    scale = DIFF_HEAD_DIM ** -0.5
    q1, q2, k1, k2 = q[:, :, 0], q[:, :, 1], k[:, :, 0], k[:, :, 1]
    outs = []
    for lo, hi in query_blocks(L):
        bias = rel_bias_block(rel_bias, lo, hi)
        p1 = causal_probs(q1, k1, lo, hi, scale, bias)
        p2 = causal_probs(q2, k2, lo, hi, scale, bias)
        a = p1 - lam_full * p2
        outs.append(jnp.einsum("bhqk,bhkd->bhqd", a.astype(v.dtype), v[:, :, :hi]))
    o = jnp.concatenate(outs, axis=2)
    o = rms_norm(o, subln_g) * (1.0 - lam_init)
    return from_heads(o) @ w_out


def swiglu(h, w_gate, w_up, w_down):
    return (jax.nn.silu(h @ w_gate) * (h @ w_up)) @ w_down


def setup_inputs(seed: int = 0) -> dict:
    key = jax.random.key(seed)
    ks = jax.random.split(key, 20)

    def normal(k, shape, scale):
        return jax.random.normal(k, shape, jnp.float32) * scale

    return {
        "x": normal(ks[0], (BATCH, SEQ, D_MODEL), 1.0),
        "meta_tokens": normal(ks[1], (N_META, D_MODEL), 1.0),
        "rel_bias": normal(ks[2], (REL_BUCKETS, DIFF_HEADS), 0.5),
        "ev_w_in": normal(ks[3], (N_EVEN, D_MODEL, EVEN_IN), D_MODEL ** -0.5),
        "ev_b_f": 3.0 + normal(ks[4], (N_EVEN, FOX_HEADS), 0.5),
        "ev_q_norm": 1.0 + normal(ks[5], (N_EVEN, MLA_Q_RANK), 0.02),
        "ev_kv_norm": 1.0 + normal(ks[6], (N_EVEN, MLA_KV_RANK), 0.02),
        "ev_w_uq": normal(ks[7], (N_EVEN, MLA_Q_RANK, MLA_HEADS * (MLA_NOPE_DIM + MLA_ROPE_DIM)), MLA_Q_RANK ** -0.5),
        "ev_w_ukv": normal(ks[8], (N_EVEN, MLA_KV_RANK, MLA_HEADS * (MLA_NOPE_DIM + MLA_V_DIM)), MLA_KV_RANK ** -0.5),
        "ev_w_out": normal(ks[9], (N_EVEN, EVEN_OUT, D_MODEL), EVEN_OUT ** -0.5),
        "od_w_in": normal(ks[10], (N_ODD, D_MODEL, DIFF_IN), D_MODEL ** -0.5),
        "od_lambda": normal(ks[11], (N_ODD, 4, DIFF_HEAD_DIM), 0.1),
        "od_subln": 1.0 + normal(ks[12], (N_ODD, 2 * DIFF_HEAD_DIM), 0.02),
        "od_w_out": normal(ks[13], (N_ODD, DIFF_OUT, D_MODEL), DIFF_OUT ** -0.5),
        "norm_g": 1.0 + normal(ks[14], (DEPTH, 4, D_MODEL), 0.02),
        "ffn_w_gate": normal(ks[15], (DEPTH, D_MODEL, D_FF), D_MODEL ** -0.5),
        "ffn_w_up": normal(ks[16], (DEPTH, D_MODEL, D_FF), D_MODEL ** -0.5),
        "ffn_w_down": normal(ks[17], (DEPTH, D_FF, D_MODEL), D_FF ** -0.5),
    }


def reference(x, meta_tokens, rel_bias, ev_w_in, ev_b_f, ev_q_norm, ev_kv_norm, ev_w_uq, ev_w_ukv,
              ev_w_out, od_w_in, od_lambda, od_subln, od_w_out, norm_g, ffn_w_gate, ffn_w_up, ffn_w_down):
    b = x.shape[0]
    meta = jnp.broadcast_to(meta_tokens[None].astype(x.dtype), (b, N_META, D_MODEL))
    hs = jnp.concatenate([meta, x], axis=1)
    for layer in range(DEPTH):
        g = norm_g[layer]
        i = layer // 2
        hn = rms_norm(hs, g[0])
        if layer % 2 == 0:
            m = fox_mla_mixer(hn, ev_w_in[i], ev_b_f[i], ev_q_norm[i], ev_kv_norm[i],
                              ev_w_uq[i], ev_w_ukv[i], ev_w_out[i])
        else:
            lam_init = 0.8 - 0.6 * math.exp(-0.3 * layer)
            m = diff_mixer(hn, od_w_in[i], od_lambda[i], od_subln[i], od_w_out[i], rel_bias, lam_init)
        hs = hs + rms_norm(m, g[1])
        f = swiglu(rms_norm(hs, g[2]), ffn_w_gate[layer], ffn_w_up[layer], ffn_w_down[layer])
        hs = hs + rms_norm(f, g[3])
    return hs[:, N_META:]
```

```python
import functools
import math

import jax
import jax.numpy as jnp
from jax import lax
from jax.experimental import pallas as pl
from jax.experimental.pallas import tpu as pltpu

D_MODEL = 4096
DEPTH = 4
N_META = 16
EPS = 1e-6
FOX_HEADS = 16
FOX_DIM = 128
MLA_HEADS = 16
MLA_Q_RANK = 896
MLA_KV_RANK = 320
MLA_NOPE_DIM = 128
MLA_ROPE_DIM = 64
MLA_V_DIM = 128
ROPE_THETA = 10000.0
DIFF_HEAD_DIM = 128
DIFF_HEADS = D_MODEL // (2 * DIFF_HEAD_DIM)
REL_BUCKETS = 32
REL_MAX_DIST = 128
D_FF = ((8 * D_MODEL + 3 * 256 - 1) // (3 * 256)) * 256

LANES = 128
META_BLOCK = LANES
MASK_VALUE = -1e30
VMEM_LIMIT_BYTES = 56 * 1024 * 1024

FF_TILE = 512
D_FF_PAD = ((D_FF + FF_TILE - 1) // FF_TILE) * FF_TILE
KV_RANK_PAD = 384
EVEN_SMALL = MLA_Q_RANK + KV_RANK_PAD + 2 * LANES
ATTN_TILE = 512


def _params(*sem):
    return pltpu.CompilerParams(dimension_semantics=sem, vmem_limit_bytes=VMEM_LIMIT_BYTES)


def _tile(n, t, unit=LANES):
    if n <= t:
        return n
    return max(c for c in range(unit, t + 1, unit) if n % c == 0)


def _rms(x, g, n):
    ms = jnp.sum(x * x, axis=-1, keepdims=True) / n
    return x * lax.rsqrt(ms + EPS) * g


def _norm_kernel(x_ref, g_ref, o_ref):
    o_ref[...] = _rms(x_ref[...], g_ref[...], x_ref.shape[-1]).astype(o_ref.dtype)


def rmsnorm_bf16(x, g, tm=256):
    m, d = x.shape
    tm = _tile(m, tm)
    return pl.pallas_call(
        _norm_kernel,
        grid=(m // tm,),
        in_specs=[pl.BlockSpec((tm, d), lambda i: (i, 0)), pl.BlockSpec((1, d), lambda i: (0, 0))],
        out_specs=pl.BlockSpec((tm, d), lambda i: (i, 0)),
        out_shape=jax.ShapeDtypeStruct((m, d), jnp.bfloat16),
        compiler_params=_params("arbitrary"),
        name="rmsnorm",
    )(x, g.reshape(1, d))


def _resid_norm_kernel(hs_ref, m_ref, g1_ref, g2_ref, hs_out_ref, hn_out_ref):
    d = hs_ref.shape[-1]
    hs = hs_ref[...] + _rms(m_ref[...], g1_ref[...], d)
    hs_out_ref[...] = hs
    hn_out_ref[...] = _rms(hs, g2_ref[...], d).astype(hn_out_ref.dtype)


def _resid_kernel(hs_ref, m_ref, g1_ref, hs_out_ref):
    hs_out_ref[...] = hs_ref[...] + _rms(m_ref[...], g1_ref[...], hs_ref.shape[-1])


def resid_norm(hs, m, g1, g2, tm=256):
    rows, d = hs.shape
    tm = _tile(rows, tm)
    row = pl.BlockSpec((tm, d), lambda i: (i, 0))
    vec = pl.BlockSpec((1, d), lambda i: (0, 0))
    if g2 is None:
        return pl.pallas_call(
            _resid_kernel, grid=(rows // tm,), in_specs=[row, row, vec], out_specs=row,
            out_shape=jax.ShapeDtypeStruct((rows, d), jnp.float32),
            input_output_aliases={0: 0}, compiler_params=_params("arbitrary"), name="resid",
        )(hs, m, g1.reshape(1, d)), None
    return pl.pallas_call(
        _resid_norm_kernel, grid=(rows // tm,), in_specs=[row, row, vec, vec], out_specs=[row, row],
        out_shape=[jax.ShapeDtypeStruct((rows, d), jnp.float32),
                   jax.ShapeDtypeStruct((rows, d), jnp.bfloat16)],
        input_output_aliases={0: 0}, compiler_params=_params("arbitrary"), name="resid_norm",
    )(hs, m, g1.reshape(1, d), g2.reshape(1, d))


def _mm_kernel(x_ref, w_ref, o_ref):
    o_ref[...] = jnp.dot(x_ref[...], w_ref[...], preferred_element_type=jnp.float32).astype(o_ref.dtype)


def matmul(x, w, out_dtype, tm=1024, tn=1024):
    m, k = x.shape
    n = w.shape[1]
    tm, tn = _tile(m, tm), _tile(n, tn)
    return pl.pallas_call(
        _mm_kernel,
        grid=(m // tm, n // tn),
        in_specs=[pl.BlockSpec((tm, k), lambda i, j: (i, 0)), pl.BlockSpec((k, tn), lambda i, j: (0, j))],
        out_specs=pl.BlockSpec((tm, tn), lambda i, j: (i, j)),
        out_shape=jax.ShapeDtypeStruct((m, n), out_dtype),
        compiler_params=_params("arbitrary", "arbitrary"),
        name="matmul",
    )(x, w)


def _mm_acc_kernel(x_ref, w_ref, o_ref):
    d = jnp.dot(x_ref[...], w_ref[...], preferred_element_type=jnp.float32)
    k = pl.program_id(2)

    @pl.when(k == 0)
    def _():
        o_ref[...] = d

    @pl.when(k > 0)
    def _():
        o_ref[...] += d


def matmul_ktiled(x, w, tm=1024, tn=1024, tk=1024):
    m, k = x.shape
    n = w.shape[1]
    tm, tn, tk = _tile(m, tm), _tile(n, tn), _tile(k, tk)
    return pl.pallas_call(
        _mm_acc_kernel,
        grid=(m // tm, n // tn, k // tk),
        in_specs=[pl.BlockSpec((tm, tk), lambda i, j, kk: (i, kk)),
                  pl.BlockSpec((tk, tn), lambda i, j, kk: (kk, j))],
        out_specs=pl.BlockSpec((tm, tn), lambda i, j, kk: (i, j)),
        out_shape=jax.ShapeDtypeStruct((m, n), jnp.float32),
        compiler_params=_params("arbitrary", "arbitrary", "arbitrary"),
        name="matmul_ktiled",
    )(x, w)


def _ffn_up_kernel(x_ref, wg_ref, wu_ref, o_ref, *, d_ff):
    x = x_ref[...]
    g = jnp.dot(x, wg_ref[...], preferred_element_type=jnp.float32)
    u = jnp.dot(x, wu_ref[...], preferred_element_type=jnp.float32)
    h = (g / (1.0 + jnp.exp(-g))) * u
    tf = o_ref.shape[-1]
    col = pl.program_id(1) * tf + lax.broadcasted_iota(jnp.int32, h.shape, 1)
    o_ref[...] = jnp.where(col < d_ff, h, 0.0).astype(o_ref.dtype)


def ffn_up(x, wg, wu, tm=1024):
    m, k = x.shape
    d_ff = wg.shape[1]
    tm = _tile(m, tm)
    return pl.pallas_call(
        functools.partial(_ffn_up_kernel, d_ff=d_ff),
        grid=(m // tm, D_FF_PAD // FF_TILE),
        in_specs=[pl.BlockSpec((tm, k), lambda i, j: (i, 0)),
                  pl.BlockSpec((k, FF_TILE), lambda i, j: (0, j)),
                  pl.BlockSpec((k, FF_TILE), lambda i, j: (0, j))],
        out_specs=pl.BlockSpec((tm, FF_TILE), lambda i, j: (i, j)),
        out_shape=jax.ShapeDtypeStruct((m, D_FF_PAD), jnp.bfloat16),
        compiler_params=_params("arbitrary", "arbitrary"),
        name="ffn_up",
    )(x, wg, wu)


def _rope128(r, c, sa, sb):
    return r * c + pltpu.roll(r, LANES - MLA_ROPE_DIM // 2, 1) * sa + pltpu.roll(r, MLA_ROPE_DIM // 2, 1) * sb


def _mla_latent_kernel(s_ref, gq_ref, gkv_ref, c_ref, sa_ref, sb_ref, cq_ref, ckv_ref, kr_ref):
    q0, kv0, r0 = 0, MLA_Q_RANK, MLA_Q_RANK + KV_RANK_PAD
    cq_ref[...] = _rms(s_ref[:, q0:kv0], gq_ref[...], MLA_Q_RANK).astype(cq_ref.dtype)
    ckv_ref[...] = _rms(s_ref[:, kv0:r0], gkv_ref[...], MLA_KV_RANK).astype(ckv_ref.dtype)
    kr_ref[...] = _rope128(s_ref[:, r0:r0 + LANES], c_ref[...], sa_ref[...], sb_ref[...]).astype(kr_ref.dtype)


def mla_latent(small, gq, gkv_pad, tabs, seq, tm=512):
    m = small.shape[0]
    tm = _tile(m, tm)
    nseq = seq // tm
    tab = pl.BlockSpec((tm, LANES), lambda i: (i % nseq, 0))
    return pl.pallas_call(
        _mla_latent_kernel,
        grid=(m // tm,),
        in_specs=[pl.BlockSpec((tm, EVEN_SMALL), lambda i: (i, 0)),
                  pl.BlockSpec((1, MLA_Q_RANK), lambda i: (0, 0)),
                  pl.BlockSpec((1, KV_RANK_PAD), lambda i: (0, 0)), tab, tab, tab],
        out_specs=[pl.BlockSpec((tm, MLA_Q_RANK), lambda i: (i, 0)),
                   pl.BlockSpec((tm, KV_RANK_PAD), lambda i: (i, 0)),
                   pl.BlockSpec((tm, LANES), lambda i: (i, 0))],
        out_shape=[jax.ShapeDtypeStruct((m, MLA_Q_RANK), jnp.bfloat16),
                   jax.ShapeDtypeStruct((m, KV_RANK_PAD), jnp.bfloat16),
                   jax.ShapeDtypeStruct((m, LANES), jnp.bfloat16)],
        compiler_params=_params("arbitrary"),
        name="mla_latent",
    )(small, gq.reshape(1, -1), gkv_pad.reshape(1, -1), *tabs)


def _q_rope_kernel(q_ref, c_ref, sa_ref, sb_ref, o_ref):
    c, sa, sb = c_ref[...], sa_ref[...], sb_ref[...]
    for h in range(MLA_HEADS):
        n0, r0 = 2 * h * LANES, (2 * h + 1) * LANES
        o_ref[:, n0:r0] = q_ref[:, n0:r0].astype(o_ref.dtype)
        o_ref[:, r0:r0 + LANES] = _rope128(q_ref[:, r0:r0 + LANES], c, sa, sb).astype(o_ref.dtype)


def q_rope(q, tabs, seq, tm=512):
    m, d = q.shape
    tm = _tile(m, tm)
    nseq = seq // tm
    tab = pl.BlockSpec((tm, LANES), lambda i: (i % nseq, 0))
    row = pl.BlockSpec((tm, d), lambda i: (i, 0))
    return pl.pallas_call(
        _q_rope_kernel, grid=(m // tm,), in_specs=[row, tab, tab, tab], out_specs=row,
        out_shape=jax.ShapeDtypeStruct((m, d), jnp.bfloat16),
        compiler_params=_params("arbitrary"), name="q_rope",
    )(q, *tabs)


def _forget_cumsum_kernel(x_ref, b_ref, o_ref):
    z = x_ref[...] + b_ref[...]
    log_f = jnp.minimum(z, 0.0) - jnp.log(1.0 + jnp.exp(-jnp.abs(z)))
    n = log_f.shape[-1]
    row = lax.broadcasted_iota(jnp.int32, (LANES, LANES), 0)
    col = lax.broadcasted_iota(jnp.int32, (LANES, LANES), 1)
    upper = (row <= col).astype(jnp.float32)
    carry = jnp.zeros((log_f.shape[0], 1), jnp.float32)
    for c in range(n // LANES):
        chunk = jnp.dot(log_f[:, c * LANES:(c + 1) * LANES], upper,
                        precision=lax.Precision.HIGHEST, preferred_element_type=jnp.float32) + carry
        o_ref[:, c * LANES:(c + 1) * LANES] = chunk
        carry = chunk[:, LANES - 1:LANES]


def forget_cumsum(f_logit, b_f):
    b, h, n = f_logit.shape
    return pl.pallas_call(
        _forget_cumsum_kernel,
        grid=(b,),
        in_specs=[pl.BlockSpec((None, h, n), lambda i: (i, 0, 0)), pl.BlockSpec((h, 1), lambda i: (0, 0))],
        out_specs=pl.BlockSpec((None, h, n), lambda i: (i, 0, 0)),
        out_shape=jax.ShapeDtypeStruct((b, h, n), jnp.float32),
        compiler_params=_params("arbitrary"),
        name="forget_cumsum",
    )(f_logit, b_f.reshape(h, 1))


def _t5_tile_kernel(rb_ref, o_ref, *, offset, causal):
    h = pl.program_id(0)
    shape = o_ref.shape
    i = lax.broadcasted_iota(jnp.int32, shape, 0)
    j = lax.broadcasted_iota(jnp.int32, shape, 1)
    dist = jnp.maximum(i - j + offset, 0)
    max_exact = REL_BUCKETS // 2
    d = jnp.maximum(dist, 1).astype(jnp.float32)
    large = max_exact + (jnp.log(d / max_exact) / math.log(REL_MAX_DIST / max_exact)
                         * (REL_BUCKETS - max_exact)).astype(jnp.int32)
    large = jnp.minimum(large, REL_BUCKETS - 1)
    bucket = jnp.where(dist < max_exact, dist, large)
    tile = jnp.zeros(shape, jnp.float32)
    for b in range(REL_BUCKETS):
        tile = jnp.where(bucket == b, rb_ref[h, b], tile)
    if causal:
        tile = jnp.where(j > i, MASK_VALUE, tile)
    o_ref[...] = tile


def t5_tiles(rel_bias_t, rows, cols, offset, causal):
    h = rel_bias_t.shape[0]
    return pl.pallas_call(
        functools.partial(_t5_tile_kernel, offset=offset, causal=causal),
        grid=(h,),
        in_specs=[pl.BlockSpec(memory_space=pltpu.SMEM)],
        out_specs=pl.BlockSpec((None, rows, cols), lambda i: (i, 0, 0)),
        out_shape=jax.ShapeDtypeStruct((h, rows, cols), jnp.float32),
        compiler_params=_params("arbitrary"),
        name="t5_tiles",
    )(rel_bias_t)


def _dot_nt(a, b):
    return lax.dot_general(a, b, (((1,), (1,)), ((), ())), preferred_element_type=jnp.float32)


def _online(s, v, state):
    m, l, acc = state
    m_new = jnp.maximum(m, jnp.max(s, axis=-1, keepdims=True))
    alpha = jnp.exp(m - m_new)
    p = jnp.exp(s - m_new)
    l = alpha * l + jnp.sum(p, axis=-1, keepdims=True)
    acc = alpha * acc + jnp.dot(p.astype(v.dtype), v, preferred_element_type=jnp.float32)
    return m_new, l, acc


def _init_state(tq, dv):
    return (jnp.full((tq, 1), MASK_VALUE, jnp.float32), jnp.zeros((tq, 1), jnp.float32),
            jnp.zeros((tq, dv), jnp.float32))


def _causal_mask(s):
    i = lax.broadcasted_iota(jnp.int32, s.shape, 0)
    j = lax.broadcasted_iota(jnp.int32, s.shape, 1)
    return jnp.where(j > i, MASK_VALUE, s)


def _fox_kernel(*refs, scale, meta_only):
    if meta_only:
        q_ref, km_ref, vm_ref, nm_ref, o_ref = refs
    else:
        q_ref, kr_ref, vr_ref, km_ref, vm_ref, nr_ref, nm_ref, o_ref = refs
    qb = pl.program_id(2)
    q = q_ref[...]
    tq = q.shape[0]
    state = _init_state(tq, vm_ref.shape[-1])
    ref = 0.0 if meta_only else jnp.min(nr_ref[qb], axis=-1, keepdims=True)
    s = _dot_nt(q, km_ref[...]) * scale + (nm_ref[...] - ref)
    if meta_only:
        s = _causal_mask(s)
    state = _online(s, vm_ref[...], state)
    if not meta_only:
        def body(kb, st):
            sc = _dot_nt(q, kr_ref[kb]) * scale + (nr_ref[kb] - ref)
            return _online(sc, vr_ref[kb], st)
        state = lax.fori_loop(0, qb, body, state)
        s = _causal_mask(_dot_nt(q, kr_ref[qb]) * scale + (nr_ref[qb] - ref))
        state = _online(s, vr_ref[qb], state)
    _, l, acc = state
    o_ref[...] = (acc * (1.0 / l)).astype(o_ref.dtype)


def _mla_kernel(*refs, scale, meta_only):
    if meta_only:
        q_ref, knm_ref, krm_ref, vm_ref, pad_ref, o_ref = refs
    else:
        q_ref, knr_ref, krr_ref, vr_ref, knm_ref, krm_ref, vm_ref, pad_ref, o_ref = refs
    qb = pl.program_id(2)
    qn, qr = q_ref[:, :LANES], q_ref[:, LANES:]
    state = _init_state(qn.shape[0], vm_ref.shape[-1])
    s = (_dot_nt(qn, knm_ref[...]) + _dot_nt(qr, krm_ref[...])) * scale + pad_ref[...]
    if meta_only:
        s = _causal_mask(s)
    state = _online(s, vm_ref[...], state)
    if not meta_only:
        def body(kb, st):
            sc = (_dot_nt(qn, knr_ref[kb]) + _dot_nt(qr, krr_ref[kb])) * scale
            return _online(sc, vr_ref[kb], st)
        state = lax.fori_loop(0, qb, body, state)
        s = _causal_mask((_dot_nt(qn, knr_ref[qb]) + _dot_nt(qr, krr_ref[qb])) * scale)
        state = _online(s, vr_ref[qb], state)
    _, l, acc = state
    o_ref[...] = (acc * (1.0 / l)).astype(o_ref.dtype)


def _diff_kernel(*refs, scale, lam_init, meta_only):
    if meta_only:
        q_ref, km_ref, vm_ref, pad_ref, tm_ref, lam_ref, g_ref, o_ref = refs
    else:
        (q_ref, kr_ref, vr_ref, km_ref, vm_ref, pad_ref, tm_ref, td_ref, ts_ref, c_ref,
         lam_ref, g_ref, o_ref) = refs
    qb = pl.program_id(2)
    d = DIFF_HEAD_DIM
    q1, q2 = q_ref[:, :d], q_ref[:, d:]
    tq = q1.shape[0]
    dv = vm_ref.shape[-1]

    def step(k, v, bias, st):
        st1, st2 = st
        s1 = _dot_nt(q1, k[:, :d]) * scale + bias
        s2 = _dot_nt(q2, k[:, d:]) * scale + bias
        return _online(s1, v, st1), _online(s2, v, st2)

    state = (_init_state(tq, dv), _init_state(tq, dv))
    state = step(km_ref[...], vm_ref[...], tm_ref[...] + pad_ref[...], state)
    if not meta_only:
        far = c_ref[:, :1]
        state = lax.fori_loop(0, jnp.maximum(qb - 1, 0),
                              lambda kb, st: step(kr_ref[kb], vr_ref[kb], far, st), state)
        state = lax.fori_loop(jnp.maximum(qb - 1, 0), qb,
                              lambda kb, st: step(kr_ref[kb], vr_ref[kb], ts_ref[...], st), state)
        state = step(kr_ref[qb], vr_ref[qb], td_ref[...], state)
    (_, l1, a1), (_, l2, a2) = state
    lam = lam_ref[...]
    lam_full = (jnp.exp(jnp.sum(lam[0:1] * lam[1:2], axis=-1, keepdims=True))
                - jnp.exp(jnp.sum(lam[2:3] * lam[3:4], axis=-1, keepdims=True)) + lam_init)
    o = a1 * (1.0 / l1) - lam_full * (a2 * (1.0 / l2))
    o_ref[...] = (_rms(o, g_ref[...], dv) * (1.0 - lam_init)).astype(o_ref.dtype)


def _attn_call(kernel, name, b, heads, nq, tq, dv, in_specs, args):
    return pl.pallas_call(
        kernel,
        grid=(b, heads, nq),
        in_specs=in_specs,
        out_specs=pl.BlockSpec((tq, dv), lambda bi, h, qi: (bi * nq + qi, h)),
        out_shape=jax.ShapeDtypeStruct((b * nq * tq, heads * dv), jnp.bfloat16),
        compiler_params=_params("arbitrary", "arbitrary", "arbitrary"),
        name=name,
    )(*args)


def _qspec(tq, dq, nq, col=lambda h: h):
    return pl.BlockSpec((tq, dq), lambda bi, h, qi: (bi * nq + qi, col(h)))


def _kspec(nkb, tk, d, col):
    return pl.BlockSpec((None, nkb, tk, d), lambda bi, h, qi: (bi, 0, 0, col(h)))


def _mspec(d, col):
    return pl.BlockSpec((META_BLOCK, d), lambda bi, h, qi: (0, col(h)))


def fox_attention(qkv, qkv_meta, negc, negc_meta, b, seq, t=ATTN_TILE):
    hd = FOX_HEADS
    scale = FOX_DIM ** -0.5
    nm_spec = pl.BlockSpec((None, 1, META_BLOCK), lambda bi, h, qi: (h, 0, 0))
    out_meta = _attn_call(
        functools.partial(_fox_kernel, scale=scale, meta_only=True), "fox_attention_meta",
        1, hd, 1, META_BLOCK, FOX_DIM,
        [_qspec(META_BLOCK, FOX_DIM, 1), _mspec(FOX_DIM, lambda h: hd + h),
         _mspec(FOX_DIM, lambda h: 2 * hd + h), nm_spec],
        (qkv_meta, qkv_meta, qkv_meta, negc_meta))
    if qkv is None:
        return out_meta
    t = _tile(seq, t)
    nq = seq // t
    kv4 = qkv.reshape(b, nq, t, qkv.shape[-1])
    out = _attn_call(
        functools.partial(_fox_kernel, scale=scale, meta_only=False), "fox_attention",
        b, hd, nq, t, FOX_DIM,
        [_qspec(t, FOX_DIM, nq), _kspec(nq, t, FOX_DIM, lambda h: hd + h),
         _kspec(nq, t, FOX_DIM, lambda h: 2 * hd + h), _mspec(FOX_DIM, lambda h: hd + h),
         _mspec(FOX_DIM, lambda h: 2 * hd + h),
         pl.BlockSpec((None, None, nq, 1, t), lambda bi, h, qi: (bi, h, 0, 0, 0)), nm_spec],
        (qkv, kv4, kv4, qkv_meta, qkv_meta, negc, negc_meta))
    return out, out_meta


def mla_attention(q, kv, kr, q_meta, kv_meta, kr_meta, pad_row, b, seq, t=ATTN_TILE):
    hd = MLA_HEADS
    scale = (MLA_NOPE_DIM + MLA_ROPE_DIM) ** -0.5
    pad_spec = pl.BlockSpec((1, META_BLOCK), lambda bi, h, qi: (0, 0))
    meta_specs = [_mspec(LANES, lambda h: 2 * h), _mspec(LANES, lambda h: 0),
                  _mspec(LANES, lambda h: 2 * h + 1), pad_spec]
    meta_args = (kv_meta, kr_meta, kv_meta, pad_row)
    out_meta = _attn_call(
        functools.partial(_mla_kernel, scale=scale, meta_only=True), "mla_attention_meta",
        1, hd, 1, META_BLOCK, MLA_V_DIM, [_qspec(META_BLOCK, 2 * LANES, 1)] + meta_specs,
        (q_meta,) + meta_args)
    if q is None:
        return out_meta
    t = _tile(seq, t)
    nq = seq // t
    kv4 = kv.reshape(b, nq, t, kv.shape[-1])
    kr4 = kr.reshape(b, nq, t, kr.shape[-1])
    out = _attn_call(
        functools.partial(_mla_kernel, scale=scale, meta_only=False), "mla_attention",
        b, hd, nq, t, MLA_V_DIM,
        [_qspec(t, 2 * LANES, nq), _kspec(nq, t, LANES, lambda h: 2 * h),
         _kspec(nq, t, LANES, lambda h: 0), _kspec(nq, t, LANES, lambda h: 2 * h + 1)] + meta_specs,
        (q, kv4, kr4, kv4) + meta_args)
    return out, out_meta


def diff_attention(qkv, qkv_meta, pad_row, tiles, lam, subln, lam_init, b, seq, t=ATTN_TILE):
    hd = DIFF_HEADS
    dv = 2 * DIFF_HEAD_DIM
    scale = DIFF_HEAD_DIM ** -0.5
    tile_meta_only, tile_meta, tile_diag, tile_sub, far = tiles
    pad_spec = pl.BlockSpec((1, META_BLOCK), lambda bi, h, qi: (0, 0))
    lam_spec = pl.BlockSpec((4, DIFF_HEAD_DIM), lambda bi, h, qi: (0, 0))
    g_spec = pl.BlockSpec((1, dv), lambda bi, h, qi: (0, 0))
    kcol, vcol = (lambda h: hd + h), (lambda h: 2 * hd + h)
    out_meta = _attn_call(
        functools.partial(_diff_kernel, scale=scale, lam_init=lam_init, meta_only=True),
        "diff_attention_meta", 1, hd, 1, META_BLOCK, dv,
        [_qspec(META_BLOCK, dv, 1), _mspec(dv, kcol), _mspec(dv, vcol), pad_spec,
         pl.BlockSpec((None, META_BLOCK, META_BLOCK), lambda bi, h, qi: (h, 0, 0)), lam_spec, g_spec],
        (qkv_meta, qkv_meta, qkv_meta, pad_row, tile_meta_only, lam, subln.reshape(1, dv)))
    if qkv is None:
        return out_meta
    t = _tile(seq, t)
    nq = seq // t
    kv4 = qkv.reshape(b, nq, t, qkv.shape[-1])
    sq = pl.BlockSpec((None, t, t), lambda bi, h, qi: (h, 0, 0))
    out = _attn_call(
        functools.partial(_diff_kernel, scale=scale, lam_init=lam_init, meta_only=False),
        "diff_attention", b, hd, nq, t, dv,
        [_qspec(t, dv, nq), _kspec(nq, t, dv, kcol), _kspec(nq, t, dv, vcol), _mspec(dv, kcol),
         _mspec(dv, vcol), pad_spec,
         pl.BlockSpec((None, None, t, META_BLOCK), lambda bi, h, qi: (h, jnp.minimum(qi, 1), 0, 0)),
         sq, sq, pl.BlockSpec((None, 1, LANES), lambda bi, h, qi: (h, 0, 0)), lam_spec, g_spec],
        (qkv, kv4, kv4, qkv_meta, qkv_meta, pad_row, tile_meta, tile_diag, tile_sub, far, lam,
         subln.reshape(1, dv)))
    return out, out_meta


def _front_pad(x):
    return jnp.pad(x, ((META_BLOCK - x.shape[0], 0), (0, 0)))


def _rope_tables(n_pos):
    inv = ROPE_THETA ** (-jnp.arange(0, MLA_ROPE_DIM, 2, dtype=jnp.float32) / MLA_ROPE_DIM)
    ang = jnp.arange(n_pos, dtype=jnp.float32)[:, None] * inv[None, :]
    cos, sin = jnp.cos(ang), jnp.sin(ang)
    z32 = jnp.zeros_like(cos)
    z64 = jnp.zeros((n_pos, LANES - MLA_ROPE_DIM), jnp.float32)
    c = jnp.concatenate([cos, cos, z64], axis=1)
    sa = jnp.concatenate([-sin, z32, z64], axis=1)
    sb = jnp.concatenate([z32, sin, z64], axis=1)
    return c, sa, sb


def _even_weights(w_in, w_uq, w_ukv, gkv):
    o = [0]
    for s in (FOX_HEADS * FOX_DIM,) * 3 + (FOX_HEADS, MLA_Q_RANK, MLA_KV_RANK, MLA_ROPE_DIM):
        o.append(o[-1] + s)
    bf = jnp.bfloat16
    w_qkv = w_in[:, :o[3]].astype(bf)
    pad = lambda w, n: jnp.pad(w, ((0, 0), (0, n - w.shape[1])))
    w_small = jnp.concatenate([w_in[:, o[4]:o[5]], pad(w_in[:, o[5]:o[6]], KV_RANK_PAD),
                               pad(w_in[:, o[6]:o[7]], LANES), pad(w_in[:, o[3]:o[4]], LANES)],
                              axis=1).astype(bf)
    hq = w_uq.reshape(MLA_Q_RANK, MLA_HEADS, MLA_NOPE_DIM + MLA_ROPE_DIM)
    hq = jnp.pad(hq, ((0, 0), (0, 0), (0, 2 * LANES - hq.shape[-1])))
    w_uq_p = hq.reshape(MLA_Q_RANK, MLA_HEADS * 2 * LANES).astype(bf)
    w_ukv_p = jnp.pad(w_ukv, ((0, KV_RANK_PAD - MLA_KV_RANK), (0, 0))).astype(bf)
    gkv_p = jnp.pad(gkv, (0, KV_RANK_PAD - MLA_KV_RANK))
    return w_qkv, w_small, w_uq_p, w_ukv_p, gkv_p


def _blocks(x, b, seq, t):
    return x.reshape(b, x.shape[1], seq // t, 1, t)


def _even_mixer(hn, hn_m, b, seq, w_in, b_f, gq, gkv, w_uq, w_ukv, w_out, tabs, tabs_m, pad_row):
    w_qkv, w_small, w_uq_p, w_ukv_p, gkv_p = _even_weights(w_in, w_uq, w_ukv, gkv)
    w_out = w_out.astype(jnp.bfloat16)
    f0 = MLA_Q_RANK + KV_RANK_PAD + LANES
    t = _tile(seq, ATTN_TILE)

    def stream(x, tab, n_pos):
        qkv = matmul(x, w_qkv, jnp.bfloat16)
        small = matmul(x, w_small, jnp.float32, tn=EVEN_SMALL // 2)
        cq, ckv, kr = mla_latent(small, gq, gkv_p, tab, n_pos)
        q = q_rope(matmul(cq, w_uq_p, jnp.float32), tab, n_pos)
        kv = matmul(ckv, w_ukv_p, jnp.bfloat16)
        return qkv, small[:, f0:f0 + FOX_HEADS], q, kv, kr

    qkv, fl, q, kv, kr = stream(hn, tabs, seq)
    qkv_m, fl_m, q_m, kv_m, kr_m = stream(hn_m, tabs_m, N_META)

    cum = forget_cumsum(fl.reshape(b, seq, FOX_HEADS).transpose(0, 2, 1), b_f)
    fl_m = jnp.pad(fl_m.T, ((0, 0), (0, LANES - N_META)))[None]
    cum_m = forget_cumsum(fl_m, b_f)[0, :, :N_META]
    negc = _blocks(-cum, b, seq, t)
    negc_meta = jnp.pad(cum_m[:, -1:] - cum_m, ((0, 0), (META_BLOCK - N_META, 0)),
                        constant_values=MASK_VALUE)[:, None, :]
    negc_meta_only = jnp.pad(-cum_m, ((0, 0), (META_BLOCK - N_META, 0)),
                             constant_values=MASK_VALUE)[:, None, :]

    qkv_mp, q_mp, kv_mp, kr_mp = map(_front_pad, (qkv_m, q_m, kv_m, kr_m))
    o_f, _ = fox_attention(qkv, qkv_mp, negc, negc_meta, b, seq)
    o_f_m = fox_attention(None, qkv_mp, None, negc_meta_only, b, seq)
    o_m, o_m_m = mla_attention(q, kv, kr, q_mp, kv_mp, kr_mp, pad_row, b, seq)
    o = jnp.concatenate([o_f, o_m], axis=1)
    o_meta = jnp.concatenate([o_f_m, o_m_m], axis=1)[-N_META:]
    return matmul(o, w_out, jnp.float32), matmul(o_meta, w_out, jnp.float32)


def _diff_mixer(hn, hn_m, b, seq, w_in, lam, subln, w_out, tiles, lam_init, pad_row):
    w_in = w_in.astype(jnp.bfloat16)
    w_out = w_out.astype(jnp.bfloat16)
    qkv = matmul(hn, w_in, jnp.bfloat16)
    qkv_mp = _front_pad(matmul(hn_m, w_in, jnp.bfloat16))
    o, o_meta = diff_attention(qkv, qkv_mp, pad_row, tiles, lam, subln, lam_init, b, seq)
    return matmul(o, w_out, jnp.float32), matmul(o_meta[-N_META:], w_out, jnp.float32)


def _ffn(hn, wg, wu, wd):
    return matmul_ktiled(ffn_up(hn, wg, wu), wd)


def kernel(x, meta_tokens, rel_bias, ev_w_in, ev_b_f, ev_q_norm, ev_kv_norm, ev_w_uq, ev_w_ukv, ev_w_out,
           od_w_in, od_lambda, od_subln, od_w_out, norm_g, ffn_w_gate, ffn_w_up, ffn_w_down):
    b, seq, d = x.shape
    hs = x.reshape(b * seq, d)
    hs_m = meta_tokens.astype(x.dtype)

    c, sa, sb = _rope_tables(N_META + seq)
    tabs_m = tuple(tb[:N_META] for tb in (c, sa, sb))
    tabs = tuple(tb[N_META:] for tb in (c, sa, sb))
    pad_row = jnp.where(jnp.arange(META_BLOCK) < META_BLOCK - N_META, MASK_VALUE, 0.0
                        ).astype(jnp.float32)[None, :]

    t = _tile(seq, ATTN_TILE)
    rb_t = rel_bias.astype(jnp.float32).T
    tile_meta0 = t5_tiles(rb_t, t, META_BLOCK, META_BLOCK, False)
    far = jnp.broadcast_to(rb_t[:, REL_BUCKETS - 1][:, None, None], (DIFF_HEADS, 1, LANES))
    tiles = (t5_tiles(rb_t, META_BLOCK, META_BLOCK, 0, True),
             jnp.stack([tile_meta0, jnp.broadcast_to(far[:, :, :1], tile_meta0.shape)], axis=1),
             t5_tiles(rb_t, t, t, 0, True), t5_tiles(rb_t, t, t, t, False), far)

    hn = rmsnorm_bf16(hs, norm_g[0, 0])
    hn_m = rmsnorm_bf16(hs_m, norm_g[0, 0])
    for layer in range(DEPTH):
        g = norm_g[layer]
        i = layer // 2
        if layer % 2 == 0:
            m, m_m = _even_mixer(hn, hn_m, b, seq, ev_w_in[i], ev_b_f[i], ev_q_norm[i], ev_kv_norm[i],
                                 ev_w_uq[i], ev_w_ukv[i], ev_w_out[i], tabs, tabs_m, pad_row)
        else:
            lam_init = 0.8 - 0.6 * math.exp(-0.3 * layer)
            m, m_m = _diff_mixer(hn, hn_m, b, seq, od_w_in[i], od_lambda[i], od_subln[i], od_w_out[i],
                                 tiles, lam_init, pad_row)
        hs, hn = resid_norm(hs, m, g[1], g[2])
        hs_m, hn_m = resid_norm(hs_m, m_m, g[1], g[2])
        wg = ffn_w_gate[layer].astype(jnp.bfloat16)
        wu = ffn_w_up[layer].astype(jnp.bfloat16)
        wd = jnp.pad(ffn_w_down[layer], ((0, D_FF_PAD - D_FF), (0, 0))).astype(jnp.bfloat16)
        f, f_m = _ffn(hn, wg, wu, wd), _ffn(hn_m, wg, wu, wd)
        g_next = norm_g[layer + 1, 0] if layer + 1 < DEPTH else None
        hs, hn = resid_norm(hs, f, g[3], g_next)
        if g_next is not None:
            hs_m, hn_m = resid_norm(hs_m, f_m, g[3], g_next)
    return hs.reshape(b, seq, d)
```

```python
import functools
import math

import jax
import jax.numpy as jnp
from jax import lax
from jax.experimental import pallas as pl
from jax.experimental.pallas import tpu as pltpu

D_MODEL = 4096
DEPTH = 4
N_META = 16
EPS = 1e-6
FOX_HEADS = 16
FOX_DIM = 128
MLA_HEADS = 16
MLA_Q_RANK = 896
MLA_KV_RANK = 320
MLA_NOPE_DIM = 128
MLA_ROPE_DIM = 64
MLA_V_DIM = 128
ROPE_THETA = 10000.0
DIFF_HEAD_DIM = 128
DIFF_HEADS = D_MODEL // (2 * DIFF_HEAD_DIM)
REL_BUCKETS = 32
REL_MAX_DIST = 128
D_FF = ((8 * D_MODEL + 3 * 256 - 1) // (3 * 256)) * 256

LANES = 128
BF16_ROWS = 16
META_BLOCK = LANES
MASK_VALUE = -1e30
VMEM_LIMIT_BYTES = 56 * 1024 * 1024
LOG2E = math.log2(math.e)

FF_TILE = 512
D_FF_PAD = ((D_FF + FF_TILE - 1) // FF_TILE) * FF_TILE
KV_RANK_PAD = 384
EVEN_SMALL = MLA_Q_RANK + KV_RANK_PAD + 2 * LANES
ATTN_TILE = 512
BIAS_SLOTS = 4
MLA_MASK_LANE = MLA_ROPE_DIM


def _params(*sem):
    return pltpu.CompilerParams(dimension_semantics=sem, vmem_limit_bytes=VMEM_LIMIT_BYTES)


def _tile(n, t, unit=LANES):
    if n <= t:
        return n
    return max(c for c in range(unit, t + 1, unit) if n % c == 0)


def _rms(x, g, n):
    ms = jnp.sum(x * x, axis=-1, keepdims=True) / n
    return x * lax.rsqrt(ms + EPS) * g


def _norm_kernel(x_ref, g_ref, o_ref):
    o_ref[...] = _rms(x_ref[...], g_ref[...], x_ref.shape[-1]).astype(o_ref.dtype)


def rmsnorm_bf16(x, g, tm=256):
    m, d = x.shape
    tm = _tile(m, tm)
    return pl.pallas_call(
        _norm_kernel,
        grid=(m // tm,),
        in_specs=[pl.BlockSpec((tm, d), lambda i: (i, 0)), pl.BlockSpec((1, d), lambda i: (0, 0))],
        out_specs=pl.BlockSpec((tm, d), lambda i: (i, 0)),
        out_shape=jax.ShapeDtypeStruct((m, d), jnp.bfloat16),
        compiler_params=_params("arbitrary"),
        name="rmsnorm",
    )(x, g.reshape(1, d))


def _resid_norm_kernel(hs_ref, m_ref, g1_ref, g2_ref, hs_out_ref, hn_out_ref):
    d = hs_ref.shape[-1]
    hs = hs_ref[...] + _rms(m_ref[...], g1_ref[...], d)
    hs_out_ref[...] = hs
    hn_out_ref[...] = _rms(hs, g2_ref[...], d).astype(hn_out_ref.dtype)


def _resid_kernel(hs_ref, m_ref, g1_ref, hs_out_ref):
    hs_out_ref[...] = hs_ref[...] + _rms(m_ref[...], g1_ref[...], hs_ref.shape[-1])


def resid_norm(hs, m, g1, g2, tm=256):
    rows, d = hs.shape
    tm = _tile(rows, tm)
    row = pl.BlockSpec((tm, d), lambda i: (i, 0))
    vec = pl.BlockSpec((1, d), lambda i: (0, 0))
    if g2 is None:
        return pl.pallas_call(
            _resid_kernel, grid=(rows // tm,), in_specs=[row, row, vec], out_specs=row,
            out_shape=jax.ShapeDtypeStruct((rows, d), jnp.float32),
            input_output_aliases={0: 0}, compiler_params=_params("arbitrary"), name="resid",
        )(hs, m, g1.reshape(1, d)), None
    return pl.pallas_call(
        _resid_norm_kernel, grid=(rows // tm,), in_specs=[row, row, vec, vec], out_specs=[row, row],
        out_shape=[jax.ShapeDtypeStruct((rows, d), jnp.float32),
                   jax.ShapeDtypeStruct((rows, d), jnp.bfloat16)],
        input_output_aliases={0: 0}, compiler_params=_params("arbitrary"), name="resid_norm",
    )(hs, m, g1.reshape(1, d), g2.reshape(1, d))


def _mm_kernel(x_ref, w_ref, o_ref):
    o_ref[...] = jnp.dot(x_ref[...], w_ref[...], preferred_element_type=jnp.float32).astype(o_ref.dtype)


def matmul(x, w, out_dtype, tm=1024, tn=1024):
    m, k = x.shape
    n = w.shape[1]
    tm, tn = _tile(m, tm), _tile(n, tn)
    return pl.pallas_call(
        _mm_kernel,
        grid=(m // tm, n // tn),
        in_specs=[pl.BlockSpec((tm, k), lambda i, j: (i, 0)), pl.BlockSpec((k, tn), lambda i, j: (0, j))],
        out_specs=pl.BlockSpec((tm, tn), lambda i, j: (i, j)),
        out_shape=jax.ShapeDtypeStruct((m, n), out_dtype),
        compiler_params=_params("arbitrary", "arbitrary"),
        name="matmul",
    )(x, w)


def _mm_acc_kernel(x_ref, w_ref, o_ref):
    d = jnp.dot(x_ref[...], w_ref[...], preferred_element_type=jnp.float32)
    k = pl.program_id(2)

    @pl.when(k == 0)
    def _():
        o_ref[...] = d

    @pl.when(k > 0)
    def _():
        o_ref[...] += d


def matmul_ktiled(x, w, tm=1024, tn=1024, tk=2816):
    m, k = x.shape
    n = w.shape[1]
    tm, tn, tk = _tile(m, tm), _tile(n, tn), _tile(k, tk)
    return pl.pallas_call(
        _mm_acc_kernel,
        grid=(m // tm, n // tn, k // tk),
        in_specs=[pl.BlockSpec((tm, tk), lambda i, j, kk: (i, kk)),
                  pl.BlockSpec((tk, tn), lambda i, j, kk: (kk, j))],
        out_specs=pl.BlockSpec((tm, tn), lambda i, j, kk: (i, j)),
        out_shape=jax.ShapeDtypeStruct((m, n), jnp.float32),
        compiler_params=_params("arbitrary", "arbitrary", "arbitrary"),
        name="matmul_ktiled",
    )(x, w)


def _ffn_up_kernel(x_ref, wg_ref, wu_ref, o_ref, *, d_ff):
    x = x_ref[...]
    g = jnp.dot(x, wg_ref[...], preferred_element_type=jnp.float32)
    u = jnp.dot(x, wu_ref[...], preferred_element_type=jnp.float32)
    h = (g / (1.0 + jnp.exp(-g))) * u
    tf = o_ref.shape[-1]
    col = pl.program_id(1) * tf + lax.broadcasted_iota(jnp.int32, h.shape, 1)
    o_ref[...] = jnp.where(col < d_ff, h, 0.0).astype(o_ref.dtype)


def ffn_up(x, wg, wu, tm=1024):
    m, k = x.shape
    d_ff = wg.shape[1]
    tm = _tile(m, tm)
    return pl.pallas_call(
        functools.partial(_ffn_up_kernel, d_ff=d_ff),
        grid=(m // tm, D_FF_PAD // FF_TILE),
        in_specs=[pl.BlockSpec((tm, k), lambda i, j: (i, 0)),
                  pl.BlockSpec((k, FF_TILE), lambda i, j: (0, j)),
                  pl.BlockSpec((k, FF_TILE), lambda i, j: (0, j))],
        out_specs=pl.BlockSpec((tm, FF_TILE), lambda i, j: (i, j)),
        out_shape=jax.ShapeDtypeStruct((m, D_FF_PAD), jnp.bfloat16),
        compiler_params=_params("arbitrary", "arbitrary"),
        name="ffn_up",
    )(x, wg, wu)


def _rope128(r, c, sa, sb):
    return r * c + pltpu.roll(r, LANES - MLA_ROPE_DIM // 2, 1) * sa + pltpu.roll(r, MLA_ROPE_DIM // 2, 1) * sb


def _mla_latent_kernel(s_ref, gq_ref, gkv_ref, c_ref, sa_ref, sb_ref, cq_ref, ckv_ref, kr_ref):
    q0, kv0, r0 = 0, MLA_Q_RANK, MLA_Q_RANK + KV_RANK_PAD
    cq_ref[...] = _rms(s_ref[:, q0:kv0], gq_ref[...], MLA_Q_RANK).astype(cq_ref.dtype)
    ckv_ref[...] = _rms(s_ref[:, kv0:r0], gkv_ref[...], MLA_KV_RANK).astype(ckv_ref.dtype)
    kr_ref[...] = _rope128(s_ref[:, r0:r0 + LANES], c_ref[...], sa_ref[...], sb_ref[...]).astype(kr_ref.dtype)


def mla_latent(small, gq, gkv_pad, tabs, seq, tm=512):
    m = small.shape[0]
    tm = _tile(m, tm)
    nseq = seq // tm
    tab = pl.BlockSpec((tm, LANES), lambda i: (i % nseq, 0))
    return pl.pallas_call(
        _mla_latent_kernel,
        grid=(m // tm,),
        in_specs=[pl.BlockSpec((tm, EVEN_SMALL), lambda i: (i, 0)),
                  pl.BlockSpec((1, MLA_Q_RANK), lambda i: (0, 0)),
                  pl.BlockSpec((1, KV_RANK_PAD), lambda i: (0, 0)), tab, tab, tab],
        out_specs=[pl.BlockSpec((tm, MLA_Q_RANK), lambda i: (i, 0)),
                   pl.BlockSpec((tm, KV_RANK_PAD), lambda i: (i, 0)),
                   pl.BlockSpec((tm, LANES), lambda i: (i, 0))],
        out_shape=[jax.ShapeDtypeStruct((m, MLA_Q_RANK), jnp.bfloat16),
                   jax.ShapeDtypeStruct((m, KV_RANK_PAD), jnp.bfloat16),
                   jax.ShapeDtypeStruct((m, LANES), jnp.bfloat16)],
        compiler_params=_params("arbitrary"),
        name="mla_latent",
    )(small, gq.reshape(1, -1), gkv_pad.reshape(1, -1), *tabs)


def _q_rope_kernel(q_ref, c_ref, sa_ref, sb_ref, o_ref):
    c, sa, sb = c_ref[...], sa_ref[...], sb_ref[...]
    one = (lax.broadcasted_iota(jnp.int32, (1, LANES), 1) == MLA_MASK_LANE).astype(jnp.float32)
    for h in range(MLA_HEADS):
        n0, r0 = 2 * h * LANES, (2 * h + 1) * LANES
        o_ref[:, n0:r0] = q_ref[:, n0:r0].astype(o_ref.dtype)
        o_ref[:, r0:r0 + LANES] = (_rope128(q_ref[:, r0:r0 + LANES], c, sa, sb) + one).astype(o_ref.dtype)


def q_rope(q, tabs, seq, tm=512):
    m, d = q.shape
    tm = _tile(m, tm)
    nseq = seq // tm
    tab = pl.BlockSpec((tm, LANES), lambda i: (i % nseq, 0))
    row = pl.BlockSpec((tm, d), lambda i: (i, 0))
    return pl.pallas_call(
        _q_rope_kernel, grid=(m // tm,), in_specs=[row, tab, tab, tab], out_specs=row,
        out_shape=jax.ShapeDtypeStruct((m, d), jnp.bfloat16),
        compiler_params=_params("arbitrary"), name="q_rope",
    )(q, *tabs)


def _forget_cumsum_kernel(x_ref, b_ref, o_ref):
    z = x_ref[...] + b_ref[...]
    log_f = jnp.minimum(z, 0.0) - jnp.log(1.0 + jnp.exp(-jnp.abs(z)))
    n = log_f.shape[-1]
    row = lax.broadcasted_iota(jnp.int32, (LANES, LANES), 0)
    col = lax.broadcasted_iota(jnp.int32, (LANES, LANES), 1)
    upper = (row <= col).astype(jnp.float32)
    carry = jnp.zeros((log_f.shape[0], 1), jnp.float32)
    for c in range(n // LANES):
        chunk = jnp.dot(log_f[:, c * LANES:(c + 1) * LANES], upper,
                        precision=lax.Precision.HIGHEST, preferred_element_type=jnp.float32) + carry
        o_ref[:, c * LANES:(c + 1) * LANES] = chunk
        carry = chunk[:, LANES - 1:LANES]


def forget_cumsum(f_logit, b_f):
    b, h, n = f_logit.shape
    return pl.pallas_call(
        _forget_cumsum_kernel,
        grid=(b,),
        in_specs=[pl.BlockSpec((None, h, n), lambda i: (i, 0, 0)), pl.BlockSpec((h, 1), lambda i: (0, 0))],
        out_specs=pl.BlockSpec((None, h, n), lambda i: (i, 0, 0)),
        out_shape=jax.ShapeDtypeStruct((b, h, n), jnp.float32),
        compiler_params=_params("arbitrary"),
        name="forget_cumsum",
    )(f_logit, b_f.reshape(h, 1))


def _t5_tile_kernel(rb_ref, o_ref, *, offset, causal):
    h = pl.program_id(0)
    shape = o_ref.shape
    i = lax.broadcasted_iota(jnp.int32, shape, 0)
    j = lax.broadcasted_iota(jnp.int32, shape, 1)
    dist = jnp.maximum(j - i + offset, 0)
    max_exact = REL_BUCKETS // 2
    d = jnp.maximum(dist, 1).astype(jnp.float32)
    large = max_exact + (jnp.log(d / max_exact) / math.log(REL_MAX_DIST / max_exact)
                         * (REL_BUCKETS - max_exact)).astype(jnp.int32)
    large = jnp.minimum(large, REL_BUCKETS - 1)
    bucket = jnp.where(dist < max_exact, dist, large)
    far = rb_ref[h, REL_BUCKETS - 1]
    tile = jnp.zeros(shape, jnp.float32)
    for b in range(REL_BUCKETS):
        tile = jnp.where(bucket == b, (rb_ref[h, b] - far) * LOG2E, tile)
    if causal:
        tile = jnp.where(i > j, MASK_VALUE, tile)
    o_ref[...] = tile


def t5_tiles(rel_bias_t, rows, cols, offset, causal):
    h = rel_bias_t.shape[0]
    return pl.pallas_call(
        functools.partial(_t5_tile_kernel, offset=offset, causal=causal),
        grid=(h,),
        in_specs=[pl.BlockSpec(memory_space=pltpu.SMEM)],
        out_specs=pl.BlockSpec((None, rows, cols), lambda i: (i, 0, 0)),
        out_shape=jax.ShapeDtypeStruct((h, rows, cols), jnp.float32),
        compiler_params=_params("arbitrary"),
        name="t5_tiles",
    )(rel_bias_t)


def _q_aug(q):
    ones = (lax.broadcasted_iota(jnp.int32, (LANES, q.shape[0]), 0) < BIAS_SLOTS).astype(q.dtype)
    return jnp.concatenate([q.T, ones], axis=0)


def _v_aug(v):
    return jnp.concatenate([v.T, jnp.ones((BF16_ROWS, v.shape[0]), v.dtype)], axis=0)


def _k_aug(k, extra):
    return jnp.concatenate([k, extra], axis=1)


def _init_state(tq, dv):
    return (jnp.full((1, tq), MASK_VALUE, jnp.float32), jnp.zeros((dv + BF16_ROWS, tq), jnp.float32))


def _score(k_aug, q_aug):
    return jnp.dot(k_aug, q_aug, preferred_element_type=jnp.float32)


def _causal(s):
    i = lax.broadcasted_iota(jnp.int32, s.shape, 0)
    j = lax.broadcasted_iota(jnp.int32, s.shape, 1)
    return jnp.where(i > j, MASK_VALUE, s)


def _consume(s, v_aug, state):
    m, acc = state
    m_new = jnp.maximum(m, jnp.max(s, axis=0, keepdims=True))
    p = jnp.exp2(s - m_new).astype(v_aug.dtype)
    acc = jnp.exp2(m - m_new) * acc + jnp.dot(v_aug, p, preferred_element_type=jnp.float32)
    return m_new, acc


def _sweep(qb, score, consume, bufs, state):
    sa, sb = bufs
    score(0, sa)

    def pair(u, st):
        kb = 2 * u
        score(kb + 1, sb)
        st = consume(kb, sa, st)
        score(kb + 2, sa)
        return consume(kb + 1, sb, st)

    state = lax.fori_loop(0, qb // 2, pair, state)
    odd = qb % 2

    def odd_tail(_, st):
        score(qb, sb)
        return consume(qb, sb, consume(qb - 1, sa, st))

    state = lax.fori_loop(0, odd, odd_tail, state)
    return lax.fori_loop(0, 1 - odd, lambda _, st: consume(qb, sa, st), state)


def _finish(state, dv):
    _, acc = state
    return (acc[:dv] * (1.0 / acc[dv:dv + 1])).T


def _single_kernel(*refs, rope_keys, meta_only):
    if meta_only:
        q_ref, km_ref, xm_ref, vm_ref, o_ref = refs
    else:
        (q_ref, kr_ref, xr_ref, vr_ref, km_ref, xm_ref, vm_ref, add_ref, o_ref,
         kar_ref, var_ref, kam_ref, vam_ref, sa_ref, sb_ref) = refs
    qb = pl.program_id(2)
    q_aug = q_ref[...].T if rope_keys else _q_aug(q_ref[...])
    dv = vm_ref.shape[-1]
    state = _init_state(q_aug.shape[1], dv)
    if meta_only:
        s = _causal(_score(_k_aug(km_ref[...], xm_ref[...]), q_aug))
        state = _consume(s, _v_aug(vm_ref[...]), state)
    else:
        @pl.when(qb == 0)
        def _():
            kam_ref[...] = _k_aug(km_ref[...], xm_ref[...])
            vam_ref[...] = _v_aug(vm_ref[...])
            for blk in range(kr_ref.shape[0]):
                kar_ref[blk] = _k_aug(kr_ref[blk], xr_ref[blk])
                var_ref[blk] = _v_aug(vr_ref[blk])

        def score(kb, ref):
            ref[...] = _score(kar_ref[kb], q_aug)

        def consume(kb, ref, st):
            return _consume(ref[...] + add_ref[(kb == qb).astype(jnp.int32)], var_ref[kb], st)

        state = _consume(_score(kam_ref[...], q_aug), vam_ref[...], state)
        state = _sweep(qb, score, consume, (sa_ref, sb_ref), state)
    o_ref[...] = _finish(state, dv).astype(o_ref.dtype)


def _diff_kernel(*refs, lam_init, meta_only):
    if meta_only:
        q_ref, km_ref, xm_ref, vm_ref, tm_ref, lam_ref, g_ref, o_ref = refs
    else:
        (q_ref, kr_ref, xr_ref, vr_ref, km_ref, xm_ref, vm_ref, tm_ref, add_ref, lam_ref, g_ref,
         o_ref, kar_ref, var_ref, kam_ref, vam_ref, sa_ref, sb_ref) = refs
    qb = pl.program_id(2)
    d = DIFF_HEAD_DIM
    dv = vm_ref.shape[-1]
    q_augs = (_q_aug(q_ref[:, :d]), _q_aug(q_ref[:, d:]))
    tq = q_augs[0].shape[1]

    def consume(ss, v_aug, st):
        return tuple(_consume(ss[c], v_aug, st[c]) for c in range(2))

    state = (_init_state(tq, dv), _init_state(tq, dv))
    if meta_only:
        km, xm = km_ref[...], xm_ref[...]
        ss = tuple(_score(_k_aug(km[:, c * d:(c + 1) * d], xm), q_augs[c]) + tm_ref[...] for c in range(2))
        state = consume(ss, _v_aug(vm_ref[...]), state)
    else:
        @pl.when(qb == 0)
        def _():
            km, xm, xr = km_ref[...], xm_ref[...], xr_ref[...]
            vam_ref[...] = _v_aug(vm_ref[...])
            for c in range(2):
                kam_ref[c] = _k_aug(km[:, c * d:(c + 1) * d], xm)
            for blk in range(kr_ref.shape[0]):
                var_ref[blk] = _v_aug(vr_ref[blk])
                for c in range(2):
                    kar_ref[c, blk] = _k_aug(kr_ref[blk, :, c * d:(c + 1) * d], xr)

        def score(kb, ref):
            for c in range(2):
                ref[c] = _score(kar_ref[c, kb], q_augs[c])

        def consume_real(kb, ref, st):
            add = add_ref[jnp.clip(kb - qb + 2, 0, 2)]
            return consume(tuple(ref[c] + add for c in range(2)), var_ref[kb], st)

        ss = tuple(_score(kam_ref[c], q_augs[c]) + tm_ref[...] for c in range(2))
        state = consume(ss, vam_ref[...], state)
        state = _sweep(qb, score, consume_real, (sa_ref, sb_ref), state)
    lam = lam_ref[...]
    lam_full = (jnp.exp(jnp.sum(lam[0:1] * lam[1:2], axis=-1, keepdims=True))
                - jnp.exp(jnp.sum(lam[2:3] * lam[3:4], axis=-1, keepdims=True)) + lam_init)
    o = _finish(state[0], dv) - lam_full * _finish(state[1], dv)
    o_ref[...] = (_rms(o, g_ref[...], dv) * (1.0 - lam_init)).astype(o_ref.dtype)


def _attn_call(kernel, name, b, heads, nq, tq, dv, in_specs, args, scratch=()):
    return pl.pallas_call(
        kernel,
        grid=(b, heads, nq),
        in_specs=in_specs,
        out_specs=pl.BlockSpec((tq, dv), lambda bi, h, qi: (bi * nq + qi, h)),
        out_shape=jax.ShapeDtypeStruct((b * nq * tq, heads * dv), jnp.bfloat16),
        scratch_shapes=list(scratch),
        compiler_params=_params("arbitrary", "arbitrary", "arbitrary"),
        name=name,
    )(*args)


def _aug_scratch(n_comp, nkb, t, dv):
    lead = (n_comp,) if n_comp > 1 else ()
    bf = jnp.bfloat16
    return [pltpu.VMEM(lead + (nkb, t, 2 * LANES), bf), pltpu.VMEM((nkb, dv + BF16_ROWS, t), bf),
            pltpu.VMEM(lead + (META_BLOCK, 2 * LANES), bf), pltpu.VMEM((dv + BF16_ROWS, META_BLOCK), bf),
            pltpu.VMEM(lead + (t, t), jnp.float32), pltpu.VMEM(lead + (t, t), jnp.float32)]


def _causal_tiles(t):
    i = lax.broadcasted_iota(jnp.int32, (t, t), 0)
    j = lax.broadcasted_iota(jnp.int32, (t, t), 1)
    return jnp.stack([jnp.zeros((t, t), jnp.float32), jnp.where(i > j, MASK_VALUE, 0.0).astype(jnp.float32)])


def _whole(shape):
    return pl.BlockSpec(shape, lambda bi, h, qi: (0,) * len(shape))


def _qspec(tq, dq, nq, col=lambda h: h):
    return pl.BlockSpec((tq, dq), lambda bi, h, qi: (bi * nq + qi, col(h)))


def _kspec(nkb, tk, d, col):
    return pl.BlockSpec((None, nkb, tk, d), lambda bi, h, qi: (bi, 0, 0, col(h)))


def _mspec(d, col):
    return pl.BlockSpec((META_BLOCK, d), lambda bi, h, qi: (0, col(h)))


def fox_attention(qkv, qkv_meta, extra, extra_meta, b, seq, t=ATTN_TILE):
    hd = FOX_HEADS
    xm_spec = pl.BlockSpec((None, META_BLOCK, LANES), lambda bi, h, qi: (h, 0, 0))
    meta_specs = [_mspec(FOX_DIM, lambda h: hd + h), xm_spec, _mspec(FOX_DIM, lambda h: 2 * hd + h)]
    meta_args = (qkv_meta, extra_meta, qkv_meta)
    if qkv is None:
        return _attn_call(functools.partial(_single_kernel, rope_keys=False, meta_only=True),
                          "fox_attention_meta", 1, hd, 1, META_BLOCK, FOX_DIM,
                          [_qspec(META_BLOCK, FOX_DIM, 1)] + meta_specs, (qkv_meta,) + meta_args)
    t = _tile(seq, t)
    nq = seq // t
    kv4 = qkv.reshape(b, nq, t, qkv.shape[-1])
    return _attn_call(
        functools.partial(_single_kernel, rope_keys=False, meta_only=False), "fox_attention",
        b, hd, nq, t, FOX_DIM,
        [_qspec(t, FOX_DIM, nq), _kspec(nq, t, FOX_DIM, lambda h: hd + h),
         pl.BlockSpec((None, None, nq, t, LANES), lambda bi, h, qi: (bi, h, 0, 0, 0)),
         _kspec(nq, t, FOX_DIM, lambda h: 2 * hd + h)] + meta_specs + [_whole((2, t, t))],
        (qkv, kv4, extra.reshape(b, hd, nq, t, LANES), kv4) + meta_args + (_causal_tiles(t),),
        _aug_scratch(1, nq, t, FOX_DIM))


def mla_attention(q, kv, kr, q_meta, kv_meta, kr_meta, b, seq, t=ATTN_TILE):
    hd = MLA_HEADS
    meta_specs = [_mspec(LANES, lambda h: 2 * h), _mspec(LANES, lambda h: 0), _mspec(LANES, lambda h: 2 * h + 1)]
    meta_args = (kv_meta, kr_meta, kv_meta)
    if q is None:
        return _attn_call(functools.partial(_single_kernel, rope_keys=True, meta_only=True),
                          "mla_attention_meta", 1, hd, 1, META_BLOCK, MLA_V_DIM,
                          [_qspec(META_BLOCK, 2 * LANES, 1)] + meta_specs, (q_meta,) + meta_args)
    t = _tile(seq, t)
    nq = seq // t
    kv4 = kv.reshape(b, nq, t, kv.shape[-1])
    kr4 = kr.reshape(b, nq, t, kr.shape[-1])
    return _attn_call(
        functools.partial(_single_kernel, rope_keys=True, meta_only=False), "mla_attention",
        b, hd, nq, t, MLA_V_DIM,
        [_qspec(t, 2 * LANES, nq), _kspec(nq, t, LANES, lambda h: 2 * h),
         _kspec(nq, t, LANES, lambda h: 0), _kspec(nq, t, LANES, lambda h: 2 * h + 1)] + meta_specs
        + [_whole((2, t, t))],
        (q, kv4, kr4, kv4) + meta_args + (_causal_tiles(t),), _aug_scratch(1, nq, t, MLA_V_DIM))


def diff_attention(qkv, qkv_meta, extra, extra_meta, tiles, lam, subln, lam_init, b, seq, t=ATTN_TILE):
    hd = DIFF_HEADS
    dv = 2 * DIFF_HEAD_DIM
    tile_meta_only, tile_meta, tile_real = tiles
    lam_spec = pl.BlockSpec((4, DIFF_HEAD_DIM), lambda bi, h, qi: (0, 0))
    g_spec = pl.BlockSpec((1, dv), lambda bi, h, qi: (0, 0))
    kcol, vcol = (lambda h: hd + h), (lambda h: 2 * hd + h)
    per_head = lambda r, c: pl.BlockSpec((None, r, c), lambda bi, h, qi: (h, 0, 0))
    meta_specs = [_mspec(dv, kcol), per_head(META_BLOCK, LANES), _mspec(dv, vcol)]
    meta_args = (qkv_meta, extra_meta, qkv_meta)
    tail = (lam, subln.reshape(1, dv))
    if qkv is None:
        return _attn_call(
            functools.partial(_diff_kernel, lam_init=lam_init, meta_only=True),
            "diff_attention_meta", 1, hd, 1, META_BLOCK, dv,
            [_qspec(META_BLOCK, dv, 1)] + meta_specs + [per_head(META_BLOCK, META_BLOCK), lam_spec, g_spec],
            (qkv_meta,) + meta_args + (tile_meta_only,) + tail)
    t = _tile(seq, t)
    nq = seq // t
    kv4 = qkv.reshape(b, nq, t, qkv.shape[-1])
    return _attn_call(
        functools.partial(_diff_kernel, lam_init=lam_init, meta_only=False),
        "diff_attention", b, hd, nq, t, dv,
        [_qspec(t, dv, nq), _kspec(nq, t, dv, kcol), per_head(t, LANES), _kspec(nq, t, dv, vcol)] + meta_specs
        + [pl.BlockSpec((None, None, META_BLOCK, t), lambda bi, h, qi: (h, jnp.minimum(qi, 1), 0, 0)),
           pl.BlockSpec((None, 3, t, t), lambda bi, h, qi: (h, 0, 0, 0)), lam_spec, g_spec],
        (qkv, kv4, extra, kv4) + meta_args + (tile_meta, tile_real) + tail,
        _aug_scratch(2, nq, t, dv))


def _front_pad(x):
    return jnp.pad(x, ((META_BLOCK - x.shape[0], 0), (0, 0)))


def _split3(x):
    def head(v):
        bits = lax.bitcast_convert_type(v, jnp.uint32) & jnp.uint32(0xFFFF0000)
        return lax.bitcast_convert_type(bits, jnp.float32)

    hi = head(x)
    mid = head(x - hi)
    lo = head(x - hi - mid)
    return tuple(p.astype(jnp.bfloat16) for p in (hi, mid, lo))


def _extra_lanes(x, mask=None):
    pieces = list(_split3(x)) + [jnp.zeros(x.shape, jnp.bfloat16) if mask is None else mask.astype(jnp.bfloat16)]
    out = jnp.stack(pieces, axis=-1)
    return jnp.pad(out, [(0, 0)] * x.ndim + [(0, LANES - BIAS_SLOTS)])


def _pad_mask(heads):
    m = jnp.where(jnp.arange(META_BLOCK) < META_BLOCK - N_META, MASK_VALUE, 0.0).astype(jnp.float32)
    return jnp.broadcast_to(m, (heads, META_BLOCK))


def _rope_tables(n_pos):
    inv = ROPE_THETA ** (-jnp.arange(0, MLA_ROPE_DIM, 2, dtype=jnp.float32) / MLA_ROPE_DIM)
    ang = jnp.arange(n_pos, dtype=jnp.float32)[:, None] * inv[None, :]
    cos, sin = jnp.cos(ang), jnp.sin(ang)
    z32 = jnp.zeros_like(cos)
    z64 = jnp.zeros((n_pos, LANES - MLA_ROPE_DIM), jnp.float32)
    c = jnp.concatenate([cos, cos, z64], axis=1)
    sa = jnp.concatenate([-sin, z32, z64], axis=1)
    sb = jnp.concatenate([z32, sin, z64], axis=1)
    return c, sa, sb


def _even_weights(w_in, w_uq, w_ukv, gkv):
    o = [0]
    for s in (FOX_HEADS * FOX_DIM,) * 3 + (FOX_HEADS, MLA_Q_RANK, MLA_KV_RANK, MLA_ROPE_DIM):
        o.append(o[-1] + s)
    bf = jnp.bfloat16
    fox_scale = FOX_DIM ** -0.5 * LOG2E
    mla_scale = (MLA_NOPE_DIM + MLA_ROPE_DIM) ** -0.5 * LOG2E
    w_qkv = jnp.concatenate([w_in[:, :o[1]] * fox_scale, w_in[:, o[1]:o[3]]], axis=1).astype(bf)
    pad = lambda w, n: jnp.pad(w, ((0, 0), (0, n - w.shape[1])))
    w_small = jnp.concatenate([w_in[:, o[4]:o[5]], pad(w_in[:, o[5]:o[6]], KV_RANK_PAD),
                               pad(w_in[:, o[6]:o[7]], LANES), pad(w_in[:, o[3]:o[4]], LANES)],
                              axis=1).astype(bf)
    hq = (w_uq * mla_scale).reshape(MLA_Q_RANK, MLA_HEADS, MLA_NOPE_DIM + MLA_ROPE_DIM)
    hq = jnp.pad(hq, ((0, 0), (0, 0), (0, 2 * LANES - hq.shape[-1])))
    w_uq_p = hq.reshape(MLA_Q_RANK, MLA_HEADS * 2 * LANES).astype(bf)
    w_ukv_p = jnp.pad(w_ukv, ((0, KV_RANK_PAD - MLA_KV_RANK), (0, 0))).astype(bf)
    gkv_p = jnp.pad(gkv, (0, KV_RANK_PAD - MLA_KV_RANK))
    return w_qkv, w_small, w_uq_p, w_ukv_p, gkv_p


def _even_mixer(hn, hn_m, b, seq, w_in, b_f, gq, gkv, w_uq, w_ukv, w_out, tabs, tabs_m):
    w_qkv, w_small, w_uq_p, w_ukv_p, gkv_p = _even_weights(w_in, w_uq, w_ukv, gkv)
    w_out = w_out.astype(jnp.bfloat16)
    f0 = MLA_Q_RANK + KV_RANK_PAD + LANES

    def stream(x, tab, n_pos):
        qkv = matmul(x, w_qkv, jnp.bfloat16)
        small = matmul(x, w_small, jnp.float32, tn=EVEN_SMALL // 2)
        cq, ckv, kr = mla_latent(small, gq, gkv_p, tab, n_pos)
        q = q_rope(matmul(cq, w_uq_p, jnp.float32), tab, n_pos)
        kv = matmul(ckv, w_ukv_p, jnp.bfloat16)
        return qkv, small[:, f0:f0 + FOX_HEADS], q, kv, kr

    qkv, fl, q, kv, kr = stream(hn, tabs, seq)
    qkv_m, fl_m, q_m, kv_m, kr_m = stream(hn_m, tabs_m, N_META)

    cum = forget_cumsum(fl.reshape(b, seq, FOX_HEADS).transpose(0, 2, 1), b_f)
    fl_m = jnp.pad(fl_m.T, ((0, 0), (0, LANES - N_META)))[None]
    cum_m = forget_cumsum(fl_m, b_f)[0, :, :N_META]
    mask = _pad_mask(FOX_HEADS)
    front = lambda x: jnp.pad(x, ((0, 0), (META_BLOCK - N_META, 0)))
    extra = _extra_lanes(-LOG2E * cum)
    extra_meta = _extra_lanes(front(LOG2E * (cum_m[:, -1:] - cum_m)), mask)
    extra_meta_only = _extra_lanes(front(-LOG2E * cum_m), mask)

    qkv_mp, q_mp, kv_mp, kr_mp = map(_front_pad, (qkv_m, q_m, kv_m, kr_m))
    kr_mp = kr_mp.at[:META_BLOCK - N_META, MLA_MASK_LANE].set(MASK_VALUE)
    o_f = fox_attention(qkv, qkv_mp, extra, extra_meta, b, seq)
    o_f_m = fox_attention(None, qkv_mp, None, extra_meta_only, b, seq)
    o_m = mla_attention(q, kv, kr, q_mp, kv_mp, kr_mp, b, seq)
    o_m_m = mla_attention(None, None, None, q_mp, kv_mp, kr_mp, b, seq)
    o = jnp.concatenate([o_f, o_m], axis=1)
    o_meta = jnp.concatenate([o_f_m, o_m_m], axis=1)[-N_META:]
    return matmul(o, w_out, jnp.float32), matmul(o_meta, w_out, jnp.float32)


def _diff_mixer(hn, hn_m, b, seq, w_in, lam, subln, w_out, bias, lam_init):
    n_q = DIFF_HEADS * 2 * DIFF_HEAD_DIM
    w_in = jnp.concatenate([w_in[:, :n_q] * (DIFF_HEAD_DIM ** -0.5 * LOG2E), w_in[:, n_q:]],
                           axis=1).astype(jnp.bfloat16)
    w_out = w_out.astype(jnp.bfloat16)
    extra, extra_meta, tiles = bias
    qkv = matmul(hn, w_in, jnp.bfloat16)
    qkv_mp = _front_pad(matmul(hn_m, w_in, jnp.bfloat16))
    o = diff_attention(qkv, qkv_mp, extra, extra_meta, tiles, lam, subln, lam_init, b, seq)
    o_meta = diff_attention(None, qkv_mp, None, extra_meta, tiles, lam, subln, lam_init, b, seq)
    return matmul(o, w_out, jnp.float32), matmul(o_meta[-N_META:], w_out, jnp.float32)


def _diff_bias(rel_bias, t):
    rb_t = rel_bias.astype(jnp.float32).T
    far = LOG2E * rb_t[:, REL_BUCKETS - 1]
    extra = _extra_lanes(jnp.broadcast_to(far[:, None], (DIFF_HEADS, t)))
    extra_meta = _extra_lanes(jnp.broadcast_to(far[:, None], (DIFF_HEADS, META_BLOCK)), _pad_mask(DIFF_HEADS))
    tile_meta0 = t5_tiles(rb_t, META_BLOCK, t, META_BLOCK, False)
    tile_sub, tile_diag = t5_tiles(rb_t, t, t, t, False), t5_tiles(rb_t, t, t, 0, True)
    tiles = (t5_tiles(rb_t, META_BLOCK, META_BLOCK, 0, True),
             jnp.stack([tile_meta0, jnp.zeros_like(tile_meta0)], axis=1),
             jnp.stack([jnp.zeros_like(tile_sub), tile_sub, tile_diag], axis=1))
    return extra, extra_meta, tiles


def _ffn(hn, wg, wu, wd):
    return matmul_ktiled(ffn_up(hn, wg, wu), wd)


def kernel(x, meta_tokens, rel_bias, ev_w_in, ev_b_f, ev_q_norm, ev_kv_norm, ev_w_uq, ev_w_ukv, ev_w_out,
           od_w_in, od_lambda, od_subln, od_w_out, norm_g, ffn_w_gate, ffn_w_up, ffn_w_down):
    b, seq, d = x.shape
    hs = x.reshape(b * seq, d)
    hs_m = meta_tokens.astype(x.dtype)

    c, sa, sb = _rope_tables(N_META + seq)
    tabs_m = tuple(tb[:N_META] for tb in (c, sa, sb))
    tabs = tuple(tb[N_META:] for tb in (c, sa, sb))
    bias = _diff_bias(rel_bias, _tile(seq, ATTN_TILE))

    hn = rmsnorm_bf16(hs, norm_g[0, 0])
    hn_m = rmsnorm_bf16(hs_m, norm_g[0, 0])
    for layer in range(DEPTH):
        g = norm_g[layer]
        i = layer // 2
        if layer % 2 == 0:
            m, m_m = _even_mixer(hn, hn_m, b, seq, ev_w_in[i], ev_b_f[i], ev_q_norm[i], ev_kv_norm[i],
                                 ev_w_uq[i], ev_w_ukv[i], ev_w_out[i], tabs, tabs_m)
        else:
            lam_init = 0.8 - 0.6 * math.exp(-0.3 * layer)
            m, m_m = _diff_mixer(hn, hn_m, b, seq, od_w_in[i], od_lambda[i], od_subln[i], od_w_out[i],
                                 bias, lam_init)
        hs, hn = resid_norm(hs, m, g[1], g[2])
        hs_m, hn_m = resid_norm(hs_m, m_m, g[1], g[2])
        wg = ffn_w_gate[layer].astype(jnp.bfloat16)
        wu = ffn_w_up[layer].astype(jnp.bfloat16)
        wd = jnp.pad(ffn_w_down[layer], ((0, D_FF_PAD - D_FF), (0, 0))).astype(jnp.bfloat16)
        f, f_m = _ffn(hn, wg, wu, wd), _ffn(hn_m, wg, wu, wd)
        g_next = norm_g[layer + 1, 0] if layer + 1 < DEPTH else None
        hs, hn = resid_norm(hs, f, g[3], g_next)
        if g_next is not None:
            hs_m, hn_m = resid_norm(hs_m, f_m, g[3], g_next)
    return hs.reshape(b, seq, d)
```

```python
import functools
import math

import jax
import jax.numpy as jnp
from jax import lax
from jax.experimental import pallas as pl
from jax.experimental.pallas import tpu as pltpu

D_MODEL = 4096
DEPTH = 4
N_META = 16
EPS = 1e-6
FOX_HEADS = 16
FOX_DIM = 128
MLA_HEADS = 16
MLA_Q_RANK = 896
MLA_KV_RANK = 320
MLA_NOPE_DIM = 128
MLA_ROPE_DIM = 64
MLA_V_DIM = 128
ROPE_THETA = 10000.0
DIFF_HEAD_DIM = 128
DIFF_HEADS = D_MODEL // (2 * DIFF_HEAD_DIM)
REL_BUCKETS = 32
REL_MAX_DIST = 128
D_FF = ((8 * D_MODEL + 3 * 256 - 1) // (3 * 256)) * 256

LANES = 128
BF16_ROWS = 16
META_BLOCK = LANES
MASK_VALUE = -1e30
VMEM_LIMIT_BYTES = 56 * 1024 * 1024
LOG2E = math.log2(math.e)

FF_TILE = 512
D_FF_PAD = ((D_FF + FF_TILE - 1) // FF_TILE) * FF_TILE
KV_RANK_PAD = 384
EVEN_SMALL = MLA_Q_RANK + KV_RANK_PAD + 2 * LANES
ATTN_TILE = 512
BIAS_SLOTS = 4
MLA_MASK_LANE = MLA_ROPE_DIM


def _params(*sem):
    return pltpu.CompilerParams(dimension_semantics=sem, vmem_limit_bytes=VMEM_LIMIT_BYTES)


def _tile(n, t, unit=LANES):
    if n <= t:
        return n
    return max(c for c in range(unit, t + 1, unit) if n % c == 0)


def _rms(x, g, n):
    ms = jnp.sum(x * x, axis=-1, keepdims=True) / n
    return x * lax.rsqrt(ms + EPS) * g


def _norm_kernel(x_ref, g_ref, o_ref):
    o_ref[...] = _rms(x_ref[...], g_ref[...], x_ref.shape[-1]).astype(o_ref.dtype)


def rmsnorm_bf16(x, g, tm=256):
    m, d = x.shape
    tm = _tile(m, tm)
    return pl.pallas_call(
        _norm_kernel,
        grid=(m // tm,),
        in_specs=[pl.BlockSpec((tm, d), lambda i: (i, 0)), pl.BlockSpec((1, d), lambda i: (0, 0))],
        out_specs=pl.BlockSpec((tm, d), lambda i: (i, 0)),
        out_shape=jax.ShapeDtypeStruct((m, d), jnp.bfloat16),
        compiler_params=_params("arbitrary"),
        name="rmsnorm",
    )(x, g.reshape(1, d))


def _resid_norm_kernel(hs_ref, m_ref, g1_ref, g2_ref, hs_out_ref, hn_out_ref):
    d = hs_ref.shape[-1]
    hs = hs_ref[...] + _rms(m_ref[...], g1_ref[...], d)
    hs_out_ref[...] = hs
    hn_out_ref[...] = _rms(hs, g2_ref[...], d).astype(hn_out_ref.dtype)


def _resid_kernel(hs_ref, m_ref, g1_ref, hs_out_ref):
    hs_out_ref[...] = hs_ref[...] + _rms(m_ref[...], g1_ref[...], hs_ref.shape[-1])


def resid_norm(hs, m, g1, g2, tm=256):
    rows, d = hs.shape
    tm = _tile(rows, tm)
    row = pl.BlockSpec((tm, d), lambda i: (i, 0))
    vec = pl.BlockSpec((1, d), lambda i: (0, 0))
    if g2 is None:
        return pl.pallas_call(
            _resid_kernel, grid=(rows // tm,), in_specs=[row, row, vec], out_specs=row,
            out_shape=jax.ShapeDtypeStruct((rows, d), jnp.float32),
            input_output_aliases={0: 0}, compiler_params=_params("arbitrary"), name="resid",
        )(hs, m, g1.reshape(1, d)), None
    return pl.pallas_call(
        _resid_norm_kernel, grid=(rows // tm,), in_specs=[row, row, vec, vec], out_specs=[row, row],
        out_shape=[jax.ShapeDtypeStruct((rows, d), jnp.float32),
                   jax.ShapeDtypeStruct((rows, d), jnp.bfloat16)],
        input_output_aliases={0: 0}, compiler_params=_params("arbitrary"), name="resid_norm",
    )(hs, m, g1.reshape(1, d), g2.reshape(1, d))


def _mm_kernel(x_ref, w_ref, o_ref):
    o_ref[...] = jnp.dot(x_ref[...], w_ref[...], preferred_element_type=jnp.float32).astype(o_ref.dtype)


def matmul(x, w, out_dtype, tm=1024, tn=1024):
    m, k = x.shape
    n = w.shape[1]
    tm, tn = _tile(m, tm), _tile(n, tn)
    return pl.pallas_call(
        _mm_kernel,
        grid=(m // tm, n // tn),
        in_specs=[pl.BlockSpec((tm, k), lambda i, j: (i, 0)), pl.BlockSpec((k, tn), lambda i, j: (0, j))],
        out_specs=pl.BlockSpec((tm, tn), lambda i, j: (i, j)),
        out_shape=jax.ShapeDtypeStruct((m, n), out_dtype),
        compiler_params=_params("arbitrary", "arbitrary"),
        name="matmul",
    )(x, w)


def _mm_ws_kernel(*refs, scaled):
    if scaled:
        x_ref, xm_ref, w_ref, s_ref, o_ref, om_ref, wb_ref = refs
    else:
        x_ref, xm_ref, w_ref, o_ref, om_ref, wb_ref = refs

    @pl.when(pl.program_id(1) == 0)
    def _():
        w = w_ref[...]
        if scaled:
            w = w * s_ref[...]
        wb_ref[...] = w.astype(wb_ref.dtype)
        om_ref[...] = jnp.dot(xm_ref[...], wb_ref[...], preferred_element_type=jnp.float32).astype(om_ref.dtype)

    o_ref[...] = jnp.dot(x_ref[...], wb_ref[...], preferred_element_type=jnp.float32).astype(o_ref.dtype)


def matmul_ws(x, x_meta, w_stack, layer, n, out_dtype, scale=None, tm=1024, tn=512):
    m, k = x.shape
    tm, tn = _tile(m, tm), _tile(n, tn)
    scaled = scale is not None
    in_specs = [pl.BlockSpec((tm, k), lambda j, i: (i, 0)),
                pl.BlockSpec(x_meta.shape, lambda j, i: (0, 0)),
                pl.BlockSpec((None, k, tn), lambda j, i: (layer, 0, j), pipeline_mode=pl.Buffered(1))]
    args = [x, x_meta, w_stack]
    if scaled:
        in_specs.append(pl.BlockSpec((1, tn), lambda j, i: (0, j)))
        args.append(scale)
    mm = x_meta.shape[0]
    return pl.pallas_call(
        functools.partial(_mm_ws_kernel, scaled=scaled),
        grid=(n // tn, m // tm),
        in_specs=in_specs,
        out_specs=[pl.BlockSpec((tm, tn), lambda j, i: (i, j)), pl.BlockSpec((mm, tn), lambda j, i: (0, j))],
        out_shape=[jax.ShapeDtypeStruct((m, n), out_dtype), jax.ShapeDtypeStruct((mm, n), out_dtype)],
        scratch_shapes=[pltpu.VMEM((k, tn), jnp.bfloat16)],
        compiler_params=_params("arbitrary", "arbitrary"),
        name="matmul_ws",
    )(*args)


def _mm_acc_kernel(x_ref, w_ref, o_ref):
    d = jnp.dot(x_ref[...], w_ref[...], preferred_element_type=jnp.float32)
    k = pl.program_id(2)

    @pl.when(k == 0)
    def _():
        o_ref[...] = d

    @pl.when(k > 0)
    def _():
        o_ref[...] += d


def matmul_ktiled(x, w, tm=1024, tn=1024, tk=2816):
    m, k = x.shape
    n = w.shape[1]
    tm, tn, tk = _tile(m, tm), _tile(n, tn), _tile(k, tk)
    return pl.pallas_call(
        _mm_acc_kernel,
        grid=(m // tm, n // tn, k // tk),
        in_specs=[pl.BlockSpec((tm, tk), lambda i, j, kk: (i, kk)),
                  pl.BlockSpec((tk, tn), lambda i, j, kk: (kk, j))],
        out_specs=pl.BlockSpec((tm, tn), lambda i, j, kk: (i, j)),
        out_shape=jax.ShapeDtypeStruct((m, n), jnp.float32),
        compiler_params=_params("arbitrary", "arbitrary", "arbitrary"),
        name="matmul_ktiled",
    )(x, w)


def _ffn_up_kernel(x_ref, xm_ref, wg_ref, wu_ref, o_ref, om_ref, wgb_ref, wub_ref, *, d_ff):
    tf = o_ref.shape[-1]
    first_col = pl.program_id(0) * tf

    def act(x):
        g = jnp.dot(x, wgb_ref[...], preferred_element_type=jnp.float32)
        u = jnp.dot(x, wub_ref[...], preferred_element_type=jnp.float32)
        h = (g / (1.0 + jnp.exp(-g))) * u
        col = first_col + lax.broadcasted_iota(jnp.int32, h.shape, 1)
        return jnp.where(col < d_ff, h, 0.0).astype(o_ref.dtype)

    @pl.when(pl.program_id(1) == 0)
    def _():
        wgb_ref[...] = wg_ref[...].astype(wgb_ref.dtype)
        wub_ref[...] = wu_ref[...].astype(wub_ref.dtype)
        om_ref[...] = act(xm_ref[...])

    o_ref[...] = act(x_ref[...])


def ffn_up(x, x_meta, wg_stack, wu_stack, layer, tm=1024, tf=256):
    m, k = x.shape
    d_ff = wg_stack.shape[-1]
    tm = _tile(m, tm)
    mm = x_meta.shape[0]
    last = d_ff // tf - 1
    wspec = pl.BlockSpec((None, k, tf), lambda j, i: (layer, 0, jnp.minimum(j, last)))
    return pl.pallas_call(
        functools.partial(_ffn_up_kernel, d_ff=d_ff),
        grid=(D_FF_PAD // tf, m // tm),
        in_specs=[pl.BlockSpec((tm, k), lambda j, i: (i, 0)), pl.BlockSpec((mm, k), lambda j, i: (0, 0)),
                  wspec, wspec],
        out_specs=[pl.BlockSpec((tm, tf), lambda j, i: (i, j)), pl.BlockSpec((mm, tf), lambda j, i: (0, j))],
        out_shape=[jax.ShapeDtypeStruct((m, D_FF_PAD), jnp.bfloat16),
                   jax.ShapeDtypeStruct((mm, D_FF_PAD), jnp.bfloat16)],
        scratch_shapes=[pltpu.VMEM((k, tf), jnp.bfloat16), pltpu.VMEM((k, tf), jnp.bfloat16)],
        compiler_params=_params("arbitrary", "arbitrary"),
        name="ffn_up",
    )(x, x_meta, wg_stack, wu_stack)


def _rope128(r, c, sa, sb):
    return r * c + pltpu.roll(r, LANES - MLA_ROPE_DIM // 2, 1) * sa + pltpu.roll(r, MLA_ROPE_DIM // 2, 1) * sb


def _mla_latent_kernel(s_ref, gq_ref, gkv_ref, c_ref, sa_ref, sb_ref, cq_ref, ckv_ref, kr_ref):
    q0, kv0, r0 = 0, MLA_Q_RANK, MLA_Q_RANK + KV_RANK_PAD
    cq_ref[...] = _rms(s_ref[:, q0:kv0], gq_ref[...], MLA_Q_RANK).astype(cq_ref.dtype)
    ckv_ref[...] = _rms(s_ref[:, kv0:r0], gkv_ref[...], MLA_KV_RANK).astype(ckv_ref.dtype)
    kr_ref[...] = _rope128(s_ref[:, r0:r0 + LANES], c_ref[...], sa_ref[...], sb_ref[...]).astype(kr_ref.dtype)


def mla_latent(small, gq, gkv_pad, tabs, seq, tm=512):
    m = small.shape[0]
    tm = _tile(m, tm)
    nseq = seq // tm
    tab = pl.BlockSpec((tm, LANES), lambda i: (i % nseq, 0))
    return pl.pallas_call(
        _mla_latent_kernel,
        grid=(m // tm,),
        in_specs=[pl.BlockSpec((tm, EVEN_SMALL), lambda i: (i, 0)),
                  pl.BlockSpec((1, MLA_Q_RANK), lambda i: (0, 0)),
                  pl.BlockSpec((1, KV_RANK_PAD), lambda i: (0, 0)), tab, tab, tab],
        out_specs=[pl.BlockSpec((tm, MLA_Q_RANK), lambda i: (i, 0)),
                   pl.BlockSpec((tm, KV_RANK_PAD), lambda i: (i, 0)),
                   pl.BlockSpec((tm, LANES), lambda i: (i, 0))],
        out_shape=[jax.ShapeDtypeStruct((m, MLA_Q_RANK), jnp.bfloat16),
                   jax.ShapeDtypeStruct((m, KV_RANK_PAD), jnp.bfloat16),
                   jax.ShapeDtypeStruct((m, LANES), jnp.bfloat16)],
        compiler_params=_params("arbitrary"),
        name="mla_latent",
    )(small, gq.reshape(1, -1), gkv_pad.reshape(1, -1), *tabs)


def _q_rope_kernel(q_ref, c_ref, sa_ref, sb_ref, o_ref):
    c, sa, sb = c_ref[...], sa_ref[...], sb_ref[...]
    one = (lax.broadcasted_iota(jnp.int32, (1, LANES), 1) == MLA_MASK_LANE).astype(jnp.float32)
    for h in range(MLA_HEADS):
        n0, r0 = 2 * h * LANES, (2 * h + 1) * LANES
        o_ref[:, n0:r0] = q_ref[:, n0:r0].astype(o_ref.dtype)
        o_ref[:, r0:r0 + LANES] = (_rope128(q_ref[:, r0:r0 + LANES], c, sa, sb) + one).astype(o_ref.dtype)


def q_rope(q, tabs, seq, tm=512):
    m, d = q.shape
    tm = _tile(m, tm)
    nseq = seq // tm
    tab = pl.BlockSpec((tm, LANES), lambda i: (i % nseq, 0))
    row = pl.BlockSpec((tm, d), lambda i: (i, 0))
    return pl.pallas_call(
        _q_rope_kernel, grid=(m // tm,), in_specs=[row, tab, tab, tab], out_specs=row,
        out_shape=jax.ShapeDtypeStruct((m, d), jnp.bfloat16),
        compiler_params=_params("arbitrary"), name="q_rope",
    )(q, *tabs)


def _forget_cumsum_kernel(x_ref, b_ref, o_ref):
    z = x_ref[...] + b_ref[...]
    log_f = jnp.minimum(z, 0.0) - jnp.log(1.0 + jnp.exp(-jnp.abs(z)))
    n = log_f.shape[-1]
    row = lax.broadcasted_iota(jnp.int32, (LANES, LANES), 0)
    col = lax.broadcasted_iota(jnp.int32, (LANES, LANES), 1)
    upper = (row <= col).astype(jnp.float32)
    carry = jnp.zeros((log_f.shape[0], 1), jnp.float32)
    for c in range(n // LANES):
        chunk = jnp.dot(log_f[:, c * LANES:(c + 1) * LANES], upper,
                        precision=lax.Precision.HIGHEST, preferred_element_type=jnp.float32) + carry
        o_ref[:, c * LANES:(c + 1) * LANES] = chunk
        carry = chunk[:, LANES - 1:LANES]


def forget_cumsum(f_logit, b_f):
    b, h, n = f_logit.shape
    return pl.pallas_call(
        _forget_cumsum_kernel,
        grid=(b,),
        in_specs=[pl.BlockSpec((None, h, n), lambda i: (i, 0, 0)), pl.BlockSpec((h, 1), lambda i: (0, 0))],
        out_specs=pl.BlockSpec((None, h, n), lambda i: (i, 0, 0)),
        out_shape=jax.ShapeDtypeStruct((b, h, n), jnp.float32),
        compiler_params=_params("arbitrary"),
        name="forget_cumsum",
    )(f_logit, b_f.reshape(h, 1))


def _t5_tile_kernel(rb_ref, o_ref, *, offset, causal):
    h = pl.program_id(0)
    shape = o_ref.shape
    i = lax.broadcasted_iota(jnp.int32, shape, 0)
    j = lax.broadcasted_iota(jnp.int32, shape, 1)
    dist = jnp.maximum(j - i + offset, 0)
    max_exact = REL_BUCKETS // 2
    d = jnp.maximum(dist, 1).astype(jnp.float32)
    large = max_exact + (jnp.log(d / max_exact) / math.log(REL_MAX_DIST / max_exact)
                         * (REL_BUCKETS - max_exact)).astype(jnp.int32)
    large = jnp.minimum(large, REL_BUCKETS - 1)
    bucket = jnp.where(dist < max_exact, dist, large)
    far = rb_ref[h, REL_BUCKETS - 1]
    tile = jnp.zeros(shape, jnp.float32)
    for b in range(REL_BUCKETS):
        tile = jnp.where(bucket == b, (rb_ref[h, b] - far) * LOG2E, tile)
    if causal:
        tile = jnp.where(i > j, MASK_VALUE, tile)
    o_ref[...] = tile


def t5_tiles(rel_bias_t, rows, cols, offset, causal):
    h = rel_bias_t.shape[0]
    return pl.pallas_call(
        functools.partial(_t5_tile_kernel, offset=offset, causal=causal),
        grid=(h,),
        in_specs=[pl.BlockSpec(memory_space=pltpu.SMEM)],
        out_specs=pl.BlockSpec((None, rows, cols), lambda i: (i, 0, 0)),
        out_shape=jax.ShapeDtypeStruct((h, rows, cols), jnp.float32),
        compiler_params=_params("arbitrary"),
        name="t5_tiles",
    )(rel_bias_t)


def _q_aug(q):
    ones = (lax.broadcasted_iota(jnp.int32, (LANES, q.shape[0]), 0) < BIAS_SLOTS).astype(q.dtype)
    return jnp.concatenate([q.T, ones], axis=0)


def _v_aug(v):
    return jnp.concatenate([v.T, jnp.ones((BF16_ROWS, v.shape[0]), v.dtype)], axis=0)


def _k_aug(k, extra):
    return jnp.concatenate([k, extra], axis=1)


def _init_state(tq, dv):
    return (jnp.full((1, tq), MASK_VALUE, jnp.float32), jnp.zeros((dv + BF16_ROWS, tq), jnp.float32))


def _score(k_aug, q_aug):
    return jnp.dot(k_aug, q_aug, preferred_element_type=jnp.float32)


def _causal(s):
    i = lax.broadcasted_iota(jnp.int32, s.shape, 0)
    j = lax.broadcasted_iota(jnp.int32, s.shape, 1)
    return jnp.where(i > j, MASK_VALUE, s)


def _consume(s, v_aug, state):
    m, acc = state
    m_new = jnp.maximum(m, jnp.max(s, axis=0, keepdims=True))
    p = jnp.exp2(s - m_new).astype(v_aug.dtype)
    acc = jnp.exp2(m - m_new) * acc + jnp.dot(v_aug, p, preferred_element_type=jnp.float32)
    return m_new, acc


def _sweep(qb, score, consume, bufs, state, prologue):
    sa, sb = bufs
    score(0, sa)
    state = prologue(state)

    def pair(u, st):
        kb = 2 * u
        score(kb + 1, sb)
        st = consume(kb, sa, st)
        score(kb + 2, sa)
        return consume(kb + 1, sb, st)

    state = lax.fori_loop(0, qb // 2, pair, state)
    odd = qb % 2

    def odd_tail(_, st):
        score(qb, sb)
        return consume(qb, sb, consume(qb - 1, sa, st))

    state = lax.fori_loop(0, odd, odd_tail, state)
    return lax.fori_loop(0, 1 - odd, lambda _, st: consume(qb, sa, st), state)


def _finish(state, dv):
    _, acc = state
    return (acc[:dv] * (1.0 / acc[dv:dv + 1])).T


def _single_kernel(*refs, rope_keys, meta_only):
    if meta_only:
        q_ref, km_ref, xm_ref, vm_ref, o_ref = refs
    else:
        (q_ref, kr_ref, xr_ref, vr_ref, km_ref, xm_ref, vm_ref, add_ref, o_ref,
         kar_ref, var_ref, kam_ref, vam_ref, sa_ref, sb_ref) = refs
    qb = pl.program_id(2)
    q_aug = q_ref[...].T if rope_keys else _q_aug(q_ref[...])
    dv = vm_ref.shape[-1]
    state = _init_state(q_aug.shape[1], dv)
    if meta_only:
        s = _causal(_score(_k_aug(km_ref[...], xm_ref[...]), q_aug))
        state = _consume(s, _v_aug(vm_ref[...]), state)
    else:
        @pl.when(qb == 0)
        def _():
            kam_ref[...] = _k_aug(km_ref[...], xm_ref[...])
            vam_ref[...] = _v_aug(vm_ref[...])
            for blk in range(kr_ref.shape[0]):
                kar_ref[blk] = _k_aug(kr_ref[blk], xr_ref[blk])
                var_ref[blk] = _v_aug(vr_ref[blk])

        def score(kb, ref):
            ref[...] = _score(kar_ref[kb], q_aug)

        def consume(kb, ref, st):
            return _consume(ref[...] + add_ref[(kb == qb).astype(jnp.int32)], var_ref[kb], st)

        s_meta = _score(kam_ref[...], q_aug)
        state = _sweep(qb, score, consume, (sa_ref, sb_ref), state,
                       lambda st: _consume(s_meta, vam_ref[...], st))
    o_ref[...] = _finish(state, dv).astype(o_ref.dtype)


def _diff_kernel(*refs, lam_init, meta_only):
    if meta_only:
        q_ref, km_ref, xm_ref, vm_ref, tm_ref, lam_ref, g_ref, o_ref = refs
    else:
        (q_ref, kr_ref, xr_ref, vr_ref, km_ref, xm_ref, vm_ref, tm_ref, add_ref, lam_ref, g_ref,
         o_ref, kar_ref, var_ref, kam_ref, vam_ref, sa_ref, sb_ref) = refs
    qb = pl.program_id(2)
    d = DIFF_HEAD_DIM
    dv = vm_ref.shape[-1]
    q_augs = (_q_aug(q_ref[:, :d]), _q_aug(q_ref[:, d:]))
    tq = q_augs[0].shape[1]

    def consume(ss, v_aug, st):
        return tuple(_consume(ss[c], v_aug, st[c]) for c in range(2))

    state = (_init_state(tq, dv), _init_state(tq, dv))
    if meta_only:
        km, xm = km_ref[...], xm_ref[...]
        ss = tuple(_score(_k_aug(km[:, c * d:(c + 1) * d], xm), q_augs[c]) + tm_ref[...] for c in range(2))
        state = consume(ss, _v_aug(vm_ref[...]), state)
    else:
        @pl.when(qb == 0)
        def _():
            km, xm, xr = km_ref[...], xm_ref[...], xr_ref[...]
            vam_ref[...] = _v_aug(vm_ref[...])
            for c in range(2):
                kam_ref[c] = _k_aug(km[:, c * d:(c + 1) * d], xm)
            for blk in range(kr_ref.shape[0]):
                var_ref[blk] = _v_aug(vr_ref[blk])
                for c in range(2):
                    kar_ref[c, blk] = _k_aug(kr_ref[blk, :, c * d:(c + 1) * d], xr)

        def score(kb, ref):
            for c in range(2):
                ref[c] = _score(kar_ref[c, kb], q_augs[c])

        def consume_real(kb, ref, st):
            add = add_ref[jnp.clip(kb - qb + 2, 0, 2)]
            return consume(tuple(ref[c] + add for c in range(2)), var_ref[kb], st)

        ss_meta = tuple(_score(kam_ref[c], q_augs[c]) + tm_ref[...] for c in range(2))
        state = _sweep(qb, score, consume_real, (sa_ref, sb_ref), state,
                       lambda st: consume(ss_meta, vam_ref[...], st))
    lam = lam_ref[...]
    lam_full = (jnp.exp(jnp.sum(lam[0:1] * lam[1:2], axis=-1, keepdims=True))
                - jnp.exp(jnp.sum(lam[2:3] * lam[3:4], axis=-1, keepdims=True)) + lam_init)
    o = _finish(state[0], dv) - lam_full * _finish(state[1], dv)
    o_ref[...] = (_rms(o, g_ref[...], dv) * (1.0 - lam_init)).astype(o_ref.dtype)


def _attn_call(kernel, name, b, heads, nq, tq, dv, in_specs, args, scratch=()):
    return pl.pallas_call(
        kernel,
        grid=(b, heads, nq),
        in_specs=in_specs,
        out_specs=pl.BlockSpec((tq, dv), lambda bi, h, qi: (bi * nq + qi, h)),
        out_shape=jax.ShapeDtypeStruct((b * nq * tq, heads * dv), jnp.bfloat16),
        scratch_shapes=list(scratch),
        compiler_params=_params("arbitrary", "arbitrary", "arbitrary"),
        name=name,
    )(*args)


def _aug_scratch(n_comp, nkb, t, dv):
    lead = (n_comp,) if n_comp > 1 else ()
    bf = jnp.bfloat16
    return [pltpu.VMEM(lead + (nkb, t, 2 * LANES), bf), pltpu.VMEM((nkb, dv + BF16_ROWS, t), bf),
            pltpu.VMEM(lead + (META_BLOCK, 2 * LANES), bf), pltpu.VMEM((dv + BF16_ROWS, META_BLOCK), bf),
            pltpu.VMEM(lead + (t, t), jnp.float32), pltpu.VMEM(lead + (t, t), jnp.float32)]


def _causal_tiles(t):
    i = lax.broadcasted_iota(jnp.int32, (t, t), 0)
    j = lax.broadcasted_iota(jnp.int32, (t, t), 1)
    return jnp.stack([jnp.zeros((t, t), jnp.float32), jnp.where(i > j, MASK_VALUE, 0.0).astype(jnp.float32)])


def _whole(shape):
    return pl.BlockSpec(shape, lambda bi, h, qi: (0,) * len(shape))


def _qspec(tq, dq, nq, col=lambda h: h):
    return pl.BlockSpec((tq, dq), lambda bi, h, qi: (bi * nq + qi, col(h)))


def _kspec(nkb, tk, d, col):
    return pl.BlockSpec((None, nkb, tk, d), lambda bi, h, qi: (bi, 0, 0, col(h)))


def _mspec(d, col):
    return pl.BlockSpec((META_BLOCK, d), lambda bi, h, qi: (0, col(h)))


def fox_attention(qkv, qkv_meta, extra, extra_meta, b, seq, t=ATTN_TILE):
    hd = FOX_HEADS
    xm_spec = pl.BlockSpec((None, META_BLOCK, LANES), lambda bi, h, qi: (h, 0, 0))
    meta_specs = [_mspec(FOX_DIM, lambda h: hd + h), xm_spec, _mspec(FOX_DIM, lambda h: 2 * hd + h)]
    meta_args = (qkv_meta, extra_meta, qkv_meta)
    if qkv is None:
        return _attn_call(functools.partial(_single_kernel, rope_keys=False, meta_only=True),
                          "fox_attention_meta", 1, hd, 1, META_BLOCK, FOX_DIM,
                          [_qspec(META_BLOCK, FOX_DIM, 1)] + meta_specs, (qkv_meta,) + meta_args)
    t = _tile(seq, t)
    nq = seq // t
    kv4 = qkv.reshape(b, nq, t, qkv.shape[-1])
    return _attn_call(
        functools.partial(_single_kernel, rope_keys=False, meta_only=False), "fox_attention",
        b, hd, nq, t, FOX_DIM,
        [_qspec(t, FOX_DIM, nq), _kspec(nq, t, FOX_DIM, lambda h: hd + h),
         pl.BlockSpec((None, None, nq, t, LANES), lambda bi, h, qi: (bi, h, 0, 0, 0)),
         _kspec(nq, t, FOX_DIM, lambda h: 2 * hd + h)] + meta_specs + [_whole((2, t, t))],
        (qkv, kv4, extra.reshape(b, hd, nq, t, LANES), kv4) + meta_args + (_causal_tiles(t),),
        _aug_scratch(1, nq, t, FOX_DIM))


def mla_attention(q, kv, kr, q_meta, kv_meta, kr_meta, b, seq, t=ATTN_TILE):
    hd = MLA_HEADS
    meta_specs = [_mspec(LANES, lambda h: 2 * h), _mspec(LANES, lambda h: 0), _mspec(LANES, lambda h: 2 * h + 1)]
    meta_args = (kv_meta, kr_meta, kv_meta)
    if q is None:
        return _attn_call(functools.partial(_single_kernel, rope_keys=True, meta_only=True),
                          "mla_attention_meta", 1, hd, 1, META_BLOCK, MLA_V_DIM,
                          [_qspec(META_BLOCK, 2 * LANES, 1)] + meta_specs, (q_meta,) + meta_args)
    t = _tile(seq, t)
    nq = seq // t
    kv4 = kv.reshape(b, nq, t, kv.shape[-1])
    kr4 = kr.reshape(b, nq, t, kr.shape[-1])
    return _attn_call(
        functools.partial(_single_kernel, rope_keys=True, meta_only=False), "mla_attention",
        b, hd, nq, t, MLA_V_DIM,
        [_qspec(t, 2 * LANES, nq), _kspec(nq, t, LANES, lambda h: 2 * h),
         _kspec(nq, t, LANES, lambda h: 0), _kspec(nq, t, LANES, lambda h: 2 * h + 1)] + meta_specs
        + [_whole((2, t, t))],
        (q, kv4, kr4, kv4) + meta_args + (_causal_tiles(t),), _aug_scratch(1, nq, t, MLA_V_DIM))


def diff_attention(qkv, qkv_meta, extra, extra_meta, tiles, lam, subln, lam_init, b, seq, t=ATTN_TILE):
    hd = DIFF_HEADS
    dv = 2 * DIFF_HEAD_DIM
    tile_meta_only, tile_meta, tile_real = tiles
    lam_spec = pl.BlockSpec((4, DIFF_HEAD_DIM), lambda bi, h, qi: (0, 0))
    g_spec = pl.BlockSpec((1, dv), lambda bi, h, qi: (0, 0))
    kcol, vcol = (lambda h: hd + h), (lambda h: 2 * hd + h)
    per_head = lambda r, c: pl.BlockSpec((None, r, c), lambda bi, h, qi: (h, 0, 0))
    meta_specs = [_mspec(dv, kcol), per_head(META_BLOCK, LANES), _mspec(dv, vcol)]
    meta_args = (qkv_meta, extra_meta, qkv_meta)
    tail = (lam, subln.reshape(1, dv))
    if qkv is None:
        return _attn_call(
            functools.partial(_diff_kernel, lam_init=lam_init, meta_only=True),
            "diff_attention_meta", 1, hd, 1, META_BLOCK, dv,
            [_qspec(META_BLOCK, dv, 1)] + meta_specs + [per_head(META_BLOCK, META_BLOCK), lam_spec, g_spec],
            (qkv_meta,) + meta_args + (tile_meta_only,) + tail)
    t = _tile(seq, t)
    nq = seq // t
    kv4 = qkv.reshape(b, nq, t, qkv.shape[-1])
    return _attn_call(
        functools.partial(_diff_kernel, lam_init=lam_init, meta_only=False),
        "diff_attention", b, hd, nq, t, dv,
        [_qspec(t, dv, nq), _kspec(nq, t, dv, kcol), per_head(t, LANES), _kspec(nq, t, dv, vcol)] + meta_specs
        + [pl.BlockSpec((None, None, META_BLOCK, t), lambda bi, h, qi: (h, jnp.minimum(qi, 1), 0, 0)),
           pl.BlockSpec((None, 3, t, t), lambda bi, h, qi: (h, 0, 0, 0)), lam_spec, g_spec],
        (qkv, kv4, extra, kv4) + meta_args + (tile_meta, tile_real) + tail,
        _aug_scratch(2, nq, t, dv))


def _front_pad(x):
    return jnp.pad(x, ((META_BLOCK - x.shape[0], 0), (0, 0)))


def _split3(x):
    def head(v):
        bits = lax.bitcast_convert_type(v, jnp.uint32) & jnp.uint32(0xFFFF0000)
        return lax.bitcast_convert_type(bits, jnp.float32)

    hi = head(x)
    mid = head(x - hi)
    lo = head(x - hi - mid)
    return tuple(p.astype(jnp.bfloat16) for p in (hi, mid, lo))


def _extra_lanes(x, mask=None):
    pieces = list(_split3(x)) + [jnp.zeros(x.shape, jnp.bfloat16) if mask is None else mask.astype(jnp.bfloat16)]
    out = jnp.stack(pieces, axis=-1)
    return jnp.pad(out, [(0, 0)] * x.ndim + [(0, LANES - BIAS_SLOTS)])


def _pad_mask(heads):
    m = jnp.where(jnp.arange(META_BLOCK) < META_BLOCK - N_META, MASK_VALUE, 0.0).astype(jnp.float32)
    return jnp.broadcast_to(m, (heads, META_BLOCK))


def _rope_tables(n_pos):
    inv = ROPE_THETA ** (-jnp.arange(0, MLA_ROPE_DIM, 2, dtype=jnp.float32) / MLA_ROPE_DIM)
    ang = jnp.arange(n_pos, dtype=jnp.float32)[:, None] * inv[None, :]
    cos, sin = jnp.cos(ang), jnp.sin(ang)
    z32 = jnp.zeros_like(cos)
    z64 = jnp.zeros((n_pos, LANES - MLA_ROPE_DIM), jnp.float32)
    c = jnp.concatenate([cos, cos, z64], axis=1)
    sa = jnp.concatenate([-sin, z32, z64], axis=1)
    sb = jnp.concatenate([z32, sin, z64], axis=1)
    return c, sa, sb


def _col_scale(n, n_scaled, value):
    return jnp.where(jnp.arange(n) < n_scaled, value, 1.0).astype(jnp.float32)[None, :]


def _even_weights(w_in, w_uq, w_ukv, gkv):
    o = [0]
    for s in (FOX_HEADS * FOX_DIM,) * 3 + (FOX_HEADS, MLA_Q_RANK, MLA_KV_RANK, MLA_ROPE_DIM):
        o.append(o[-1] + s)
    bf = jnp.bfloat16
    mla_scale = (MLA_NOPE_DIM + MLA_ROPE_DIM) ** -0.5 * LOG2E
    pad = lambda w, n: jnp.pad(w, ((0, 0), (0, n - w.shape[1])))
    w_small = jnp.concatenate([w_in[:, o[4]:o[5]], pad(w_in[:, o[5]:o[6]], KV_RANK_PAD),
                               pad(w_in[:, o[6]:o[7]], LANES), pad(w_in[:, o[3]:o[4]], LANES)],
                              axis=1).astype(bf)
    hq = (w_uq * mla_scale).reshape(MLA_Q_RANK, MLA_HEADS, MLA_NOPE_DIM + MLA_ROPE_DIM)
    hq = jnp.pad(hq, ((0, 0), (0, 0), (0, 2 * LANES - hq.shape[-1])))
    w_uq_p = hq.reshape(MLA_Q_RANK, MLA_HEADS * 2 * LANES).astype(bf)
    w_ukv_p = jnp.pad(w_ukv, ((0, KV_RANK_PAD - MLA_KV_RANK), (0, 0))).astype(bf)
    gkv_p = jnp.pad(gkv, (0, KV_RANK_PAD - MLA_KV_RANK))
    return w_small, w_uq_p, w_ukv_p, gkv_p


def _even_mixer(hn, hn_m, b, seq, i, ev_w_in, b_f, gq, gkv, w_uq, w_ukv, ev_w_out, tabs, tabs_m):
    w_small, w_uq_p, w_ukv_p, gkv_p = _even_weights(ev_w_in[i], w_uq, w_ukv, gkv)
    f0 = MLA_Q_RANK + KV_RANK_PAD + LANES
    n_q = FOX_HEADS * FOX_DIM
    qkv, qkv_m = matmul_ws(hn, hn_m, ev_w_in, i, 3 * n_q, jnp.bfloat16,
                           _col_scale(3 * n_q, n_q, FOX_DIM ** -0.5 * LOG2E))

    def stream(x, tab, n_pos):
        small = matmul(x, w_small, jnp.float32, tn=EVEN_SMALL // 2)
        cq, ckv, kr = mla_latent(small, gq, gkv_p, tab, n_pos)
        q = q_rope(matmul(cq, w_uq_p, jnp.float32), tab, n_pos)
        kv = matmul(ckv, w_ukv_p, jnp.bfloat16)
        return small[:, f0:f0 + FOX_HEADS], q, kv, kr

    fl, q, kv, kr = stream(hn, tabs, seq)
    fl_m, q_m, kv_m, kr_m = stream(hn_m, tabs_m, N_META)

    cum = forget_cumsum(fl.reshape(b, seq, FOX_HEADS).transpose(0, 2, 1), b_f)
    fl_m = jnp.pad(fl_m.T, ((0, 0), (0, LANES - N_META)))[None]
    cum_m = forget_cumsum(fl_m, b_f)[0, :, :N_META]
    mask = _pad_mask(FOX_HEADS)
    front = lambda x: jnp.pad(x, ((0, 0), (META_BLOCK - N_META, 0)))
    extra = _extra_lanes(-LOG2E * cum)
    extra_meta = _extra_lanes(front(LOG2E * (cum_m[:, -1:] - cum_m)), mask)
    extra_meta_only = _extra_lanes(front(-LOG2E * cum_m), mask)

    qkv_mp, q_mp, kv_mp, kr_mp = map(_front_pad, (qkv_m, q_m, kv_m, kr_m))
    kr_mp = kr_mp.at[:META_BLOCK - N_META, MLA_MASK_LANE].set(MASK_VALUE)
    o_f = fox_attention(qkv, qkv_mp, extra, extra_meta, b, seq)
    o_f_m = fox_attention(None, qkv_mp, None, extra_meta_only, b, seq)
    o_m = mla_attention(q, kv, kr, q_mp, kv_mp, kr_mp, b, seq)
    o_m_m = mla_attention(None, None, None, q_mp, kv_mp, kr_mp, b, seq)
    o = jnp.concatenate([o_f, o_m], axis=1)
    o_meta = jnp.concatenate([o_f_m, o_m_m], axis=1)[-N_META:]
    return matmul_ws(o, o_meta, ev_w_out, i, D_MODEL, jnp.float32)


def _diff_mixer(hn, hn_m, b, seq, i, od_w_in, lam, subln, od_w_out, bias, lam_init):
    n_q = DIFF_HEADS * 2 * DIFF_HEAD_DIM
    extra, extra_meta, tiles = bias
    qkv, qkv_m = matmul_ws(hn, hn_m, od_w_in, i, 3 * n_q, jnp.bfloat16,
                           _col_scale(3 * n_q, n_q, DIFF_HEAD_DIM ** -0.5 * LOG2E))
    qkv_mp = _front_pad(qkv_m)
    o = diff_attention(qkv, qkv_mp, extra, extra_meta, tiles, lam, subln, lam_init, b, seq)
    o_meta = diff_attention(None, qkv_mp, None, extra_meta, tiles, lam, subln, lam_init, b, seq)
    return matmul_ws(o, o_meta[-N_META:], od_w_out, i, D_MODEL, jnp.float32)


def _diff_bias(rel_bias, t):
    rb_t = rel_bias.astype(jnp.float32).T
    far = LOG2E * rb_t[:, REL_BUCKETS - 1]
    extra = _extra_lanes(jnp.broadcast_to(far[:, None], (DIFF_HEADS, t)))
    extra_meta = _extra_lanes(jnp.broadcast_to(far[:, None], (DIFF_HEADS, META_BLOCK)), _pad_mask(DIFF_HEADS))
    tile_meta0 = t5_tiles(rb_t, META_BLOCK, t, META_BLOCK, False)
    tile_sub, tile_diag = t5_tiles(rb_t, t, t, t, False), t5_tiles(rb_t, t, t, 0, True)
    tiles = (t5_tiles(rb_t, META_BLOCK, META_BLOCK, 0, True),
             jnp.stack([tile_meta0, jnp.zeros_like(tile_meta0)], axis=1),
             jnp.stack([jnp.zeros_like(tile_sub), tile_sub, tile_diag], axis=1))
    return extra, extra_meta, tiles


def kernel(x, meta_tokens, rel_bias, ev_w_in, ev_b_f, ev_q_norm, ev_kv_norm, ev_w_uq, ev_w_ukv, ev_w_out,
           od_w_in, od_lambda, od_subln, od_w_out, norm_g, ffn_w_gate, ffn_w_up, ffn_w_down):
    b, seq, d = x.shape
    hs = x.reshape(b * seq, d)
    hs_m = meta_tokens.astype(x.dtype)

    c, sa, sb = _rope_tables(N_META + seq)
    tabs_m = tuple(tb[:N_META] for tb in (c, sa, sb))
    tabs = tuple(tb[N_META:] for tb in (c, sa, sb))
    bias = _diff_bias(rel_bias, _tile(seq, ATTN_TILE))

    hn = rmsnorm_bf16(hs, norm_g[0, 0])
    hn_m = rmsnorm_bf16(hs_m, norm_g[0, 0])
    for layer in range(DEPTH):
        g = norm_g[layer]
        i = layer // 2
        if layer % 2 == 0:
            m, m_m = _even_mixer(hn, hn_m, b, seq, i, ev_w_in, ev_b_f[i], ev_q_norm[i], ev_kv_norm[i],
                                 ev_w_uq[i], ev_w_ukv[i], ev_w_out, tabs, tabs_m)
        else:
            lam_init = 0.8 - 0.6 * math.exp(-0.3 * layer)
            m, m_m = _diff_mixer(hn, hn_m, b, seq, i, od_w_in, od_lambda[i], od_subln[i], od_w_out,
                                 bias, lam_init)
        hs, hn = resid_norm(hs, m, g[1], g[2])
        hs_m, hn_m = resid_norm(hs_m, m_m, g[1], g[2])
        h, h_m = ffn_up(hn, hn_m, ffn_w_gate, ffn_w_up, layer)
        wd = jnp.pad(ffn_w_down[layer], ((0, D_FF_PAD - D_FF), (0, 0))).astype(jnp.bfloat16)
        f, f_m = matmul_ktiled(h, wd), matmul_ktiled(h_m, wd)
        g_next = norm_g[layer + 1, 0] if layer + 1 < DEPTH else None
        hs, hn = resid_norm(hs, f, g[3], g_next)
        if g_next is not None:
            hs_m, hn_m = resid_norm(hs_m, f_m, g[3], g_next)
    return hs.reshape(b, seq, d)
```

```python
import functools
import math

import jax
import jax.numpy as jnp
from jax import lax
from jax.experimental import pallas as pl
from jax.experimental.pallas import tpu as pltpu

D_MODEL = 4096
DEPTH = 4
N_META = 16
EPS = 1e-6
FOX_HEADS = 16
FOX_DIM = 128
MLA_HEADS = 16
MLA_Q_RANK = 896
MLA_KV_RANK = 320
MLA_NOPE_DIM = 128
MLA_ROPE_DIM = 64
MLA_V_DIM = 128
ROPE_THETA = 10000.0
DIFF_HEAD_DIM = 128
DIFF_HEADS = D_MODEL // (2 * DIFF_HEAD_DIM)
REL_BUCKETS = 32
REL_MAX_DIST = 128
D_FF = ((8 * D_MODEL + 3 * 256 - 1) // (3 * 256)) * 256

LANES = 128
BF16_ROWS = 16
META_BLOCK = LANES
MASK_VALUE = -1e30
VMEM_LIMIT_BYTES = 56 * 1024 * 1024
LOG2E = math.log2(math.e)

FF_TILE = 512
D_FF_PAD = ((D_FF + FF_TILE - 1) // FF_TILE) * FF_TILE
KV_RANK_PAD = 384
EVEN_SMALL = MLA_Q_RANK + KV_RANK_PAD + 2 * LANES
ATTN_TILE = 512
BIAS_SLOTS = 4
MLA_MASK_LANE = MLA_ROPE_DIM


def _params(*sem):
    return pltpu.CompilerParams(dimension_semantics=sem, vmem_limit_bytes=VMEM_LIMIT_BYTES)


def _tile(n, t, unit=LANES):
    if n <= t:
        return n
    return max(c for c in range(unit, t + 1, unit) if n % c == 0)


def _rms(x, g, n):
    ms = jnp.sum(x * x, axis=-1, keepdims=True) / n
    return x * lax.rsqrt(ms + EPS) * g


def _norm_kernel(x_ref, g_ref, o_ref):
    o_ref[...] = _rms(x_ref[...], g_ref[...], x_ref.shape[-1]).astype(o_ref.dtype)


def rmsnorm_bf16(x, g, tm=256):
    m, d = x.shape
    tm = _tile(m, tm)
    return pl.pallas_call(
        _norm_kernel,
        grid=(m // tm,),
        in_specs=[pl.BlockSpec((tm, d), lambda i: (i, 0)), pl.BlockSpec((1, d), lambda i: (0, 0))],
        out_specs=pl.BlockSpec((tm, d), lambda i: (i, 0)),
        out_shape=jax.ShapeDtypeStruct((m, d), jnp.bfloat16),
        compiler_params=_params("arbitrary"),
        name="rmsnorm",
    )(x, g.reshape(1, d))


def _resid_norm_kernel(hs_ref, m_ref, g1_ref, g2_ref, hs_out_ref, hn_out_ref):
    d = hs_ref.shape[-1]
    hs = hs_ref[...] + _rms(m_ref[...], g1_ref[...], d)
    hs_out_ref[...] = hs
    hn_out_ref[...] = _rms(hs, g2_ref[...], d).astype(hn_out_ref.dtype)


def _resid_kernel(hs_ref, m_ref, g1_ref, hs_out_ref):
    hs_out_ref[...] = hs_ref[...] + _rms(m_ref[...], g1_ref[...], hs_ref.shape[-1])


def resid_norm(hs, m, g1, g2, tm=256):
    rows, d = hs.shape
    tm = _tile(rows, tm)
    row = pl.BlockSpec((tm, d), lambda i: (i, 0))
    vec = pl.BlockSpec((1, d), lambda i: (0, 0))
    if g2 is None:
        return pl.pallas_call(
            _resid_kernel, grid=(rows // tm,), in_specs=[row, row, vec], out_specs=row,
            out_shape=jax.ShapeDtypeStruct((rows, d), jnp.float32),
            input_output_aliases={0: 0}, compiler_params=_params("arbitrary"), name="resid",
        )(hs, m, g1.reshape(1, d)), None
    return pl.pallas_call(
        _resid_norm_kernel, grid=(rows // tm,), in_specs=[row, row, vec, vec], out_specs=[row, row],
        out_shape=[jax.ShapeDtypeStruct((rows, d), jnp.float32),
                   jax.ShapeDtypeStruct((rows, d), jnp.bfloat16)],
        input_output_aliases={0: 0}, compiler_params=_params("arbitrary"), name="resid_norm",
    )(hs, m, g1.reshape(1, d), g2.reshape(1, d))


def _mm_kernel(x_ref, w_ref, o_ref):
    o_ref[...] = jnp.dot(x_ref[...], w_ref[...], preferred_element_type=jnp.float32).astype(o_ref.dtype)


def matmul(x, w, out_dtype, tm=1024, tn=1024):
    m, k = x.shape
    n = w.shape[1]
    tm, tn = _tile(m, tm), _tile(n, tn)
    return pl.pallas_call(
        _mm_kernel,
        grid=(m // tm, n // tn),
        in_specs=[pl.BlockSpec((tm, k), lambda i, j: (i, 0)), pl.BlockSpec((k, tn), lambda i, j: (0, j))],
        out_specs=pl.BlockSpec((tm, tn), lambda i, j: (i, j)),
        out_shape=jax.ShapeDtypeStruct((m, n), out_dtype),
        compiler_params=_params("arbitrary", "arbitrary"),
        name="matmul",
    )(x, w)


def _mm_ws_kernel(*refs, scaled):
    if scaled:
        x_ref, xm_ref, w_ref, s_ref, o_ref, om_ref, wb_ref = refs
    else:
        x_ref, xm_ref, w_ref, o_ref, om_ref, wb_ref = refs

    @pl.when(pl.program_id(1) == 0)
    def _():
        w = w_ref[...]
        if scaled:
            w = w * s_ref[...]
        wb_ref[...] = w.astype(wb_ref.dtype)
        om_ref[...] = jnp.dot(xm_ref[...], wb_ref[...], preferred_element_type=jnp.float32).astype(om_ref.dtype)

    o_ref[...] = jnp.dot(x_ref[...], wb_ref[...], preferred_element_type=jnp.float32).astype(o_ref.dtype)


def matmul_ws(x, x_meta, w_stack, layer, n, out_dtype, scale=None, tm=1024, tn=1024):
    m, k = x.shape
    tm, tn = _tile(m, tm), _tile(n, tn)
    scaled = scale is not None
    in_specs = [pl.BlockSpec((tm, k), lambda j, i: (i, 0)),
                pl.BlockSpec(x_meta.shape, lambda j, i: (0, 0)),
                pl.BlockSpec((None, k, tn), lambda j, i: (layer, 0, j), pipeline_mode=pl.Buffered(1))]
    args = [x, x_meta, w_stack]
    if scaled:
        in_specs.append(pl.BlockSpec((1, tn), lambda j, i: (0, j)))
        args.append(scale)
    mm = x_meta.shape[0]
    return pl.pallas_call(
        functools.partial(_mm_ws_kernel, scaled=scaled),
        grid=(n // tn, m // tm),
        in_specs=in_specs,
        out_specs=[pl.BlockSpec((tm, tn), lambda j, i: (i, j)), pl.BlockSpec((mm, tn), lambda j, i: (0, j))],
        out_shape=[jax.ShapeDtypeStruct((m, n), out_dtype), jax.ShapeDtypeStruct((mm, n), out_dtype)],
        scratch_shapes=[pltpu.VMEM((k, tn), jnp.bfloat16)],
        compiler_params=_params("arbitrary", "arbitrary"),
        name="matmul_ws",
    )(*args)


def _mm_acc_kernel(x_ref, w_ref, o_ref):
    d = jnp.dot(x_ref[...], w_ref[...], preferred_element_type=jnp.float32)
    k = pl.program_id(2)

    @pl.when(k == 0)
    def _():
        o_ref[...] = d

    @pl.when(k > 0)
    def _():
        o_ref[...] += d


def matmul_ktiled(x, w, tm=1024, tn=1024, tk=2816):
    m, k = x.shape
    n = w.shape[1]
    tm, tn, tk = _tile(m, tm), _tile(n, tn), _tile(k, tk)
    return pl.pallas_call(
        _mm_acc_kernel,
        grid=(m // tm, n // tn, k // tk),
        in_specs=[pl.BlockSpec((tm, tk), lambda i, j, kk: (i, kk)),
                  pl.BlockSpec((tk, tn), lambda i, j, kk: (kk, j))],
        out_specs=pl.BlockSpec((tm, tn), lambda i, j, kk: (i, j)),
        out_shape=jax.ShapeDtypeStruct((m, n), jnp.float32),
        compiler_params=_params("arbitrary", "arbitrary", "arbitrary"),
        name="matmul_ktiled",
    )(x, w)


def _ffn_up_kernel(x_ref, xm_ref, wg_ref, wu_ref, o_ref, om_ref, wgb_ref, wub_ref, *, d_ff):
    tf = o_ref.shape[-1]
    first_col = pl.program_id(0) * tf

    def act(x):
        g = jnp.dot(x, wgb_ref[...], preferred_element_type=jnp.float32)
        u = jnp.dot(x, wub_ref[...], preferred_element_type=jnp.float32)
        h = (g / (1.0 + jnp.exp(-g))) * u
        col = first_col + lax.broadcasted_iota(jnp.int32, h.shape, 1)
        return jnp.where(col < d_ff, h, 0.0).astype(o_ref.dtype)

    @pl.when(pl.program_id(1) == 0)
    def _():
        wgb_ref[...] = wg_ref[...].astype(wgb_ref.dtype)
        wub_ref[...] = wu_ref[...].astype(wub_ref.dtype)
        om_ref[...] = act(xm_ref[...])

    o_ref[...] = act(x_ref[...])


def ffn_up(x, x_meta, wg_stack, wu_stack, layer, tm=1024, tf=FF_TILE):
    m, k = x.shape
    d_ff = wg_stack.shape[-1]
    tm = _tile(m, tm)
    mm = x_meta.shape[0]
    last = pl.cdiv(d_ff, tf) - 1
    wspec = pl.BlockSpec((None, k, tf), lambda j, i: (layer, 0, jnp.minimum(j, last)), pipeline_mode=pl.Buffered(1))
    return pl.pallas_call(
        functools.partial(_ffn_up_kernel, d_ff=d_ff),
        grid=(D_FF_PAD // tf, m // tm),
        in_specs=[pl.BlockSpec((tm, k), lambda j, i: (i, 0)), pl.BlockSpec((mm, k), lambda j, i: (0, 0)),
                  wspec, wspec],
        out_specs=[pl.BlockSpec((tm, tf), lambda j, i: (i, j)), pl.BlockSpec((mm, tf), lambda j, i: (0, j))],
        out_shape=[jax.ShapeDtypeStruct((m, D_FF_PAD), jnp.bfloat16),
                   jax.ShapeDtypeStruct((mm, D_FF_PAD), jnp.bfloat16)],
        scratch_shapes=[pltpu.VMEM((k, tf), jnp.bfloat16), pltpu.VMEM((k, tf), jnp.bfloat16)],
        compiler_params=_params("arbitrary", "arbitrary"),
        name="ffn_up",
    )(x, x_meta, wg_stack, wu_stack)


def _rope128(r, c, sa, sb):
    return r * c + pltpu.roll(r, LANES - MLA_ROPE_DIM // 2, 1) * sa + pltpu.roll(r, MLA_ROPE_DIM // 2, 1) * sb


def _mla_latent_kernel(s_ref, gq_ref, gkv_ref, c_ref, sa_ref, sb_ref, cq_ref, ckv_ref, kr_ref):
    q0, kv0, r0 = 0, MLA_Q_RANK, MLA_Q_RANK + KV_RANK_PAD
    cq_ref[...] = _rms(s_ref[:, q0:kv0], gq_ref[...], MLA_Q_RANK).astype(cq_ref.dtype)
    ckv_ref[...] = _rms(s_ref[:, kv0:r0], gkv_ref[...], MLA_KV_RANK).astype(ckv_ref.dtype)
    kr_ref[...] = _rope128(s_ref[:, r0:r0 + LANES], c_ref[...], sa_ref[...], sb_ref[...]).astype(kr_ref.dtype)


def mla_latent(small, gq, gkv_pad, tabs, seq, tm=512):
    m = small.shape[0]
    tm = _tile(m, tm)
    nseq = seq // tm
    tab = pl.BlockSpec((tm, LANES), lambda i: (i % nseq, 0))
    return pl.pallas_call(
        _mla_latent_kernel,
        grid=(m // tm,),
        in_specs=[pl.BlockSpec((tm, EVEN_SMALL), lambda i: (i, 0)),
                  pl.BlockSpec((1, MLA_Q_RANK), lambda i: (0, 0)),
                  pl.BlockSpec((1, KV_RANK_PAD), lambda i: (0, 0)), tab, tab, tab],
        out_specs=[pl.BlockSpec((tm, MLA_Q_RANK), lambda i: (i, 0)),
                   pl.BlockSpec((tm, KV_RANK_PAD), lambda i: (i, 0)),
                   pl.BlockSpec((tm, LANES), lambda i: (i, 0))],
        out_shape=[jax.ShapeDtypeStruct((m, MLA_Q_RANK), jnp.bfloat16),
                   jax.ShapeDtypeStruct((m, KV_RANK_PAD), jnp.bfloat16),
                   jax.ShapeDtypeStruct((m, LANES), jnp.bfloat16)],
        compiler_params=_params("arbitrary"),
        name="mla_latent",
    )(small, gq.reshape(1, -1), gkv_pad.reshape(1, -1), *tabs)


def _q_rope_kernel(q_ref, c_ref, sa_ref, sb_ref, o_ref):
    c, sa, sb = c_ref[...], sa_ref[...], sb_ref[...]
    one = (lax.broadcasted_iota(jnp.int32, (1, LANES), 1) == MLA_MASK_LANE).astype(jnp.float32)
    for h in range(MLA_HEADS):
        n0, r0 = 2 * h * LANES, (2 * h + 1) * LANES
        o_ref[:, n0:r0] = q_ref[:, n0:r0].astype(o_ref.dtype)
        o_ref[:, r0:r0 + LANES] = (_rope128(q_ref[:, r0:r0 + LANES], c, sa, sb) + one).astype(o_ref.dtype)


def q_rope(q, tabs, seq, tm=512):
    m, d = q.shape
    tm = _tile(m, tm)
    nseq = seq // tm
    tab = pl.BlockSpec((tm, LANES), lambda i: (i % nseq, 0))
    row = pl.BlockSpec((tm, d), lambda i: (i, 0))
    return pl.pallas_call(
        _q_rope_kernel, grid=(m // tm,), in_specs=[row, tab, tab, tab], out_specs=row,
        out_shape=jax.ShapeDtypeStruct((m, d), jnp.bfloat16),
        compiler_params=_params("arbitrary"), name="q_rope",
    )(q, *tabs)


def _forget_cumsum_kernel(x_ref, b_ref, o_ref):
    z = x_ref[...] + b_ref[...]
    log_f = jnp.minimum(z, 0.0) - jnp.log(1.0 + jnp.exp(-jnp.abs(z)))
    n = log_f.shape[-1]
    row = lax.broadcasted_iota(jnp.int32, (LANES, LANES), 0)
    col = lax.broadcasted_iota(jnp.int32, (LANES, LANES), 1)
    upper = (row <= col).astype(jnp.float32)
    carry = jnp.zeros((log_f.shape[0], 1), jnp.float32)
    for c in range(n // LANES):
        chunk = jnp.dot(log_f[:, c * LANES:(c + 1) * LANES], upper,
                        precision=lax.Precision.HIGHEST, preferred_element_type=jnp.float32) + carry
        o_ref[:, c * LANES:(c + 1) * LANES] = chunk
        carry = chunk[:, LANES - 1:LANES]


def forget_cumsum(f_logit, b_f):
    b, h, n = f_logit.shape
    return pl.pallas_call(
        _forget_cumsum_kernel,
        grid=(b,),
        in_specs=[pl.BlockSpec((None, h, n), lambda i: (i, 0, 0)), pl.BlockSpec((h, 1), lambda i: (0, 0))],
        out_specs=pl.BlockSpec((None, h, n), lambda i: (i, 0, 0)),
        out_shape=jax.ShapeDtypeStruct((b, h, n), jnp.float32),
        compiler_params=_params("arbitrary"),
        name="forget_cumsum",
    )(f_logit, b_f.reshape(h, 1))


def _t5_tile_kernel(rb_ref, o_ref, *, offset, causal):
    h = pl.program_id(0)
    shape = o_ref.shape
    i = lax.broadcasted_iota(jnp.int32, shape, 0)
    j = lax.broadcasted_iota(jnp.int32, shape, 1)
    dist = jnp.maximum(j - i + offset, 0)
    max_exact = REL_BUCKETS // 2
    d = jnp.maximum(dist, 1).astype(jnp.float32)
    large = max_exact + (jnp.log(d / max_exact) / math.log(REL_MAX_DIST / max_exact)
                         * (REL_BUCKETS - max_exact)).astype(jnp.int32)
    large = jnp.minimum(large, REL_BUCKETS - 1)
    bucket = jnp.where(dist < max_exact, dist, large)
    far = rb_ref[h, REL_BUCKETS - 1]
    tile = jnp.zeros(shape, jnp.float32)
    for b in range(REL_BUCKETS):
        tile = jnp.where(bucket == b, (rb_ref[h, b] - far) * LOG2E, tile)
    if causal:
        tile = jnp.where(i > j, MASK_VALUE, tile)
    o_ref[...] = tile


def t5_tiles(rel_bias_t, rows, cols, offset, causal):
    h = rel_bias_t.shape[0]
    return pl.pallas_call(
        functools.partial(_t5_tile_kernel, offset=offset, causal=causal),
        grid=(h,),
        in_specs=[pl.BlockSpec(memory_space=pltpu.SMEM)],
        out_specs=pl.BlockSpec((None, rows, cols), lambda i: (i, 0, 0)),
        out_shape=jax.ShapeDtypeStruct((h, rows, cols), jnp.float32),
        compiler_params=_params("arbitrary"),
        name="t5_tiles",
    )(rel_bias_t)


def _q_aug(q):
    ones = (lax.broadcasted_iota(jnp.int32, (LANES, q.shape[0]), 0) < BIAS_SLOTS).astype(q.dtype)
    return jnp.concatenate([q.T, ones], axis=0)


def _v_aug(v):
    return jnp.concatenate([v.T, jnp.ones((BF16_ROWS, v.shape[0]), v.dtype)], axis=0)


def _k_aug(k, extra):
    return jnp.concatenate([k, extra], axis=1)


def _init_state(tq, dv):
    return (jnp.full((1, tq), MASK_VALUE, jnp.float32), jnp.zeros((dv + BF16_ROWS, tq), jnp.float32))


def _score(k_aug, q_aug):
    return jnp.dot(k_aug, q_aug, preferred_element_type=jnp.float32)


def _causal(s):
    i = lax.broadcasted_iota(jnp.int32, s.shape, 0)
    j = lax.broadcasted_iota(jnp.int32, s.shape, 1)
    return jnp.where(i > j, MASK_VALUE, s)


def _consume(s, v_aug, state):
    m, acc = state
    m_new = jnp.maximum(m, jnp.max(s, axis=0, keepdims=True))
    p = jnp.exp2(s - m_new).astype(v_aug.dtype)
    acc = jnp.exp2(m - m_new) * acc + jnp.dot(v_aug, p, preferred_element_type=jnp.float32)
    return m_new, acc


def _next_tile(qb, kb):
    wrap = kb == qb
    return jnp.where(wrap, qb + 1, qb), jnp.where(wrap, 0, kb + 1)


def _flat_sweep(nq, score, consume, bufs):
    sa, sb = bufs
    n_tiles = nq * (nq + 1) // 2
    zero = jnp.int32(0)
    score(zero, zero, sa)

    def pair(_, tile):
        t1 = _next_tile(*tile)
        t2 = _next_tile(*t1)
        score(*t1, sb)
        consume(*tile, sa)
        score(*t2, sa)
        consume(*t1, sb)
        return t2

    tile = lax.fori_loop(0, (n_tiles - 1) // 2, pair, (zero, zero))
    if n_tiles % 2 == 0:
        t1 = _next_tile(*tile)
        score(*t1, sb)
        consume(*tile, sa)
        consume(*t1, sb)
    else:
        consume(*tile, sa)


def _finish(state, dv):
    _, acc = state
    return (acc[:dv] * (1.0 / acc[dv:dv + 1])).T


def _resume(first, m, acc):
    return jnp.where(first, MASK_VALUE, m), jnp.where(first, 0.0, acc)


def _single_meta_kernel(q_ref, km_ref, xm_ref, vm_ref, o_ref, *, rope_keys):
    q_aug = q_ref[...].T if rope_keys else _q_aug(q_ref[...])
    dv = vm_ref.shape[-1]
    s = _causal(_score(_k_aug(km_ref[...], xm_ref[...]), q_aug))
    state = _consume(s, _v_aug(vm_ref[...]), _init_state(q_aug.shape[1], dv))
    o_ref[...] = _finish(state, dv).astype(o_ref.dtype)


def _single_kernel(q_ref, kr_ref, xr_ref, vr_ref, km_ref, xm_ref, vm_ref, add_ref, o_ref,
                   kar_ref, var_ref, kam_ref, vam_ref, sa_ref, sb_ref, qa_ref, m_ref, acc_ref, *, rope_keys):
    nq, t = kr_ref.shape[0], kr_ref.shape[1]
    dv = vm_ref.shape[-1]
    kam_ref[...] = _k_aug(km_ref[...], xm_ref[...])
    vam_ref[...] = _v_aug(vm_ref[...])
    for blk in range(nq):
        kar_ref[blk] = _k_aug(kr_ref[blk], xr_ref[blk])
        var_ref[blk] = _v_aug(vr_ref[blk])
        q = q_ref[blk * t:(blk + 1) * t, :]
        qa_ref[blk] = q.T if rope_keys else _q_aug(q)

    def score(qb, kb, buf):
        buf[...] = _score(kar_ref[kb], qa_ref[qb])

    def consume(qb, kb, buf):
        s = buf[...] + add_ref[(kb == qb).astype(jnp.int32)]
        m_ref[qb], acc_ref[qb] = _consume(s, var_ref[kb], _resume(kb == 0, m_ref[qb], acc_ref[qb]))

    _flat_sweep(nq, score, consume, (sa_ref, sb_ref))
    s_meta = [_score(kam_ref[...], qa_ref[qb]) for qb in range(nq)]
    for qb in range(nq):
        state = _consume(s_meta[qb], vam_ref[...], (m_ref[qb], acc_ref[qb]))
        o_ref[qb * t:(qb + 1) * t, :] = _finish(state, dv).astype(o_ref.dtype)


def _diff_out(outs, lam_ref, g_ref, lam_init, dv):
    lam = lam_ref[...]
    lam_full = (jnp.exp(jnp.sum(lam[0:1] * lam[1:2], axis=-1, keepdims=True))
                - jnp.exp(jnp.sum(lam[2:3] * lam[3:4], axis=-1, keepdims=True)) + lam_init)
    return _rms(outs[0] - lam_full * outs[1], g_ref[...], dv) * (1.0 - lam_init)


def _diff_meta_kernel(q_ref, km_ref, xm_ref, vm_ref, tm_ref, lam_ref, g_ref, o_ref, *, lam_init):
    d = DIFF_HEAD_DIM
    dv = vm_ref.shape[-1]
    km, xm, v_aug = km_ref[...], xm_ref[...], _v_aug(vm_ref[...])
    outs = []
    for c in range(2):
        q_aug = _q_aug(q_ref[:, c * d:(c + 1) * d])
        s = _score(_k_aug(km[:, c * d:(c + 1) * d], xm), q_aug) + tm_ref[...]
        outs.append(_finish(_consume(s, v_aug, _init_state(q_aug.shape[1], dv)), dv))
    o_ref[...] = _diff_out(outs, lam_ref, g_ref, lam_init, dv).astype(o_ref.dtype)


def _diff_kernel(q_ref, kr_ref, xr_ref, vr_ref, km_ref, xm_ref, vm_ref, tm_ref, add_ref, lam_ref, g_ref, o_ref,
                 kar_ref, var_ref, kam_ref, vam_ref, sa_ref, sb_ref, qa_ref, m_ref, acc_ref, *, lam_init):
    nq, t = kr_ref.shape[0], kr_ref.shape[1]
    d = DIFF_HEAD_DIM
    dv = vm_ref.shape[-1]
    km, xm, xr = km_ref[...], xm_ref[...], xr_ref[...]
    vam_ref[...] = _v_aug(vm_ref[...])
    for c in range(2):
        kam_ref[c] = _k_aug(km[:, c * d:(c + 1) * d], xm)
    for blk in range(nq):
        var_ref[blk] = _v_aug(vr_ref[blk])
        for c in range(2):
            kar_ref[c, blk] = _k_aug(kr_ref[blk, :, c * d:(c + 1) * d], xr)
            qa_ref[c, blk] = _q_aug(q_ref[blk * t:(blk + 1) * t, c * d:(c + 1) * d])

    def score(qb, kb, buf):
        for c in range(2):
            buf[c] = _score(kar_ref[c, kb], qa_ref[c, qb])

    def consume(qb, kb, buf):
        add = add_ref[jnp.clip(kb - qb + 2, 0, 2)]
        for c in range(2):
            m_ref[c, qb], acc_ref[c, qb] = _consume(buf[c] + add, var_ref[kb],
                                                    _resume(kb == 0, m_ref[c, qb], acc_ref[c, qb]))

    _flat_sweep(nq, score, consume, (sa_ref, sb_ref))
    for qb in range(nq):
        outs = []
        for c in range(2):
            s = _score(kam_ref[c], qa_ref[c, qb]) + tm_ref[min(qb, 1)]
            outs.append(_finish(_consume(s, vam_ref[...], (m_ref[c, qb], acc_ref[c, qb])), dv))
        o_ref[qb * t:(qb + 1) * t, :] = _diff_out(outs, lam_ref, g_ref, lam_init, dv).astype(o_ref.dtype)


def _attn_call(kernel, name, b, heads, nq, tq, dv, in_specs, args, scratch=()):
    return pl.pallas_call(
        kernel,
        grid=(b, heads, nq),
        in_specs=in_specs,
        out_specs=pl.BlockSpec((tq, dv), lambda bi, h, qi: (bi * nq + qi, h)),
        out_shape=jax.ShapeDtypeStruct((b * nq * tq, heads * dv), jnp.bfloat16),
        scratch_shapes=list(scratch),
        compiler_params=_params("arbitrary", "arbitrary", "arbitrary"),
        name=name,
    )(*args)


def _aug_scratch(n_comp, nkb, t, dv):
    lead = (n_comp,) if n_comp > 1 else ()
    bf = jnp.bfloat16
    return [pltpu.VMEM(lead + (nkb, t, 2 * LANES), bf), pltpu.VMEM((nkb, dv + BF16_ROWS, t), bf),
            pltpu.VMEM(lead + (META_BLOCK, 2 * LANES), bf), pltpu.VMEM((dv + BF16_ROWS, META_BLOCK), bf),
            pltpu.VMEM(lead + (t, t), jnp.float32), pltpu.VMEM(lead + (t, t), jnp.float32),
            pltpu.VMEM(lead + (nkb, 2 * LANES, t), bf), pltpu.VMEM(lead + (nkb, 1, t), jnp.float32),
            pltpu.VMEM(lead + (nkb, dv + BF16_ROWS, t), jnp.float32)]


def _causal_tiles(t):
    i = lax.broadcasted_iota(jnp.int32, (t, t), 0)
    j = lax.broadcasted_iota(jnp.int32, (t, t), 1)
    return jnp.stack([jnp.zeros((t, t), jnp.float32), jnp.where(i > j, MASK_VALUE, 0.0).astype(jnp.float32)])


def _whole(shape):
    return pl.BlockSpec(shape, lambda bi, h, qi: (0,) * len(shape))


def _qspec(tq, dq, nq, col=lambda h: h):
    return pl.BlockSpec((tq, dq), lambda bi, h, qi: (bi * nq + qi, col(h)))


def _kspec(nkb, tk, d, col):
    return pl.BlockSpec((None, nkb, tk, d), lambda bi, h, qi: (bi, 0, 0, col(h)))


def _mspec(d, col):
    return pl.BlockSpec((META_BLOCK, d), lambda bi, h, qi: (0, col(h)))


def fox_attention(qkv, qkv_meta, extra, extra_meta, b, seq, t=ATTN_TILE):
    hd = FOX_HEADS
    xm_spec = pl.BlockSpec((None, META_BLOCK, LANES), lambda bi, h, qi: (h, 0, 0))
    meta_specs = [_mspec(FOX_DIM, lambda h: hd + h), xm_spec, _mspec(FOX_DIM, lambda h: 2 * hd + h)]
    meta_args = (qkv_meta, extra_meta, qkv_meta)
    if qkv is None:
        return _attn_call(functools.partial(_single_meta_kernel, rope_keys=False),
                          "fox_attention_meta", 1, hd, 1, META_BLOCK, FOX_DIM,
                          [_qspec(META_BLOCK, FOX_DIM, 1)] + meta_specs, (qkv_meta,) + meta_args)
    t = _tile(seq, t)
    nq = seq // t
    kv4 = qkv.reshape(b, nq, t, qkv.shape[-1])
    return _attn_call(
        functools.partial(_single_kernel, rope_keys=False), "fox_attention",
        b, hd, 1, seq, FOX_DIM,
        [_qspec(seq, FOX_DIM, 1), _kspec(nq, t, FOX_DIM, lambda h: hd + h),
         pl.BlockSpec((None, None, nq, t, LANES), lambda bi, h, qi: (bi, h, 0, 0, 0)),
         _kspec(nq, t, FOX_DIM, lambda h: 2 * hd + h)] + meta_specs + [_whole((2, t, t))],
        (qkv, kv4, extra.reshape(b, hd, nq, t, LANES), kv4) + meta_args + (_causal_tiles(t),),
        _aug_scratch(1, nq, t, FOX_DIM))


def mla_attention(q, kv, kr, q_meta, kv_meta, kr_meta, b, seq, t=ATTN_TILE):
    hd = MLA_HEADS
    meta_specs = [_mspec(LANES, lambda h: 2 * h), _mspec(LANES, lambda h: 0), _mspec(LANES, lambda h: 2 * h + 1)]
    meta_args = (kv_meta, kr_meta, kv_meta)
    if q is None:
        return _attn_call(functools.partial(_single_meta_kernel, rope_keys=True),
                          "mla_attention_meta", 1, hd, 1, META_BLOCK, MLA_V_DIM,
                          [_qspec(META_BLOCK, 2 * LANES, 1)] + meta_specs, (q_meta,) + meta_args)
    t = _tile(seq, t)
    nq = seq // t
    kv4 = kv.reshape(b, nq, t, kv.shape[-1])
    kr4 = kr.reshape(b, nq, t, kr.shape[-1])
    return _attn_call(
        functools.partial(_single_kernel, rope_keys=True), "mla_attention",
        b, hd, 1, seq, MLA_V_DIM,
        [_qspec(seq, 2 * LANES, 1), _kspec(nq, t, LANES, lambda h: 2 * h),
         _kspec(nq, t, LANES, lambda h: 0), _kspec(nq, t, LANES, lambda h: 2 * h + 1)] + meta_specs
        + [_whole((2, t, t))],
        (q, kv4, kr4, kv4) + meta_args + (_causal_tiles(t),), _aug_scratch(1, nq, t, MLA_V_DIM))


def diff_attention(qkv, qkv_meta, extra, extra_meta, tiles, lam, subln, lam_init, b, seq, t=ATTN_TILE):
    hd = DIFF_HEADS
    dv = 2 * DIFF_HEAD_DIM
    tile_meta_only, tile_meta, tile_real = tiles
    lam_spec = pl.BlockSpec((4, DIFF_HEAD_DIM), lambda bi, h, qi: (0, 0))
    g_spec = pl.BlockSpec((1, dv), lambda bi, h, qi: (0, 0))
    kcol, vcol = (lambda h: hd + h), (lambda h: 2 * hd + h)
    per_head = lambda r, c: pl.BlockSpec((None, r, c), lambda bi, h, qi: (h, 0, 0))
    meta_specs = [_mspec(dv, kcol), per_head(META_BLOCK, LANES), _mspec(dv, vcol)]
    meta_args = (qkv_meta, extra_meta, qkv_meta)
    tail = (lam, subln.reshape(1, dv))
    if qkv is None:
        return _attn_call(
            functools.partial(_diff_meta_kernel, lam_init=lam_init),
            "diff_attention_meta", 1, hd, 1, META_BLOCK, dv,
            [_qspec(META_BLOCK, dv, 1)] + meta_specs + [per_head(META_BLOCK, META_BLOCK), lam_spec, g_spec],
            (qkv_meta,) + meta_args + (tile_meta_only,) + tail)
    t = _tile(seq, t)
    nq = seq // t
    kv4 = qkv.reshape(b, nq, t, qkv.shape[-1])
    return _attn_call(
        functools.partial(_diff_kernel, lam_init=lam_init),
        "diff_attention", b, hd, 1, seq, dv,
        [_qspec(seq, dv, 1), _kspec(nq, t, dv, kcol), per_head(t, LANES), _kspec(nq, t, dv, vcol)] + meta_specs
        + [pl.BlockSpec((None, 2, META_BLOCK, t), lambda bi, h, qi: (h, 0, 0, 0)),
           pl.BlockSpec((None, 3, t, t), lambda bi, h, qi: (h, 0, 0, 0)), lam_spec, g_spec],
        (qkv, kv4, extra, kv4) + meta_args + (tile_meta, tile_real) + tail,
        _aug_scratch(2, nq, t, dv))


def _front_pad(x):
    return jnp.pad(x, ((META_BLOCK - x.shape[0], 0), (0, 0)))


def _split3(x):
    def head(v):
        bits = lax.bitcast_convert_type(v, jnp.uint32) & jnp.uint32(0xFFFF0000)
        return lax.bitcast_convert_type(bits, jnp.float32)

    hi = head(x)
    mid = head(x - hi)
    lo = head(x - hi - mid)
    return tuple(p.astype(jnp.bfloat16) for p in (hi, mid, lo))


def _extra_lanes(x, mask=None):
    pieces = list(_split3(x)) + [jnp.zeros(x.shape, jnp.bfloat16) if mask is None else mask.astype(jnp.bfloat16)]
    out = jnp.stack(pieces, axis=-1)
    return jnp.pad(out, [(0, 0)] * x.ndim + [(0, LANES - BIAS_SLOTS)])


def _pad_mask(heads):
    m = jnp.where(jnp.arange(META_BLOCK) < META_BLOCK - N_META, MASK_VALUE, 0.0).astype(jnp.float32)
    return jnp.broadcast_to(m, (heads, META_BLOCK))


def _rope_tables(n_pos):
    inv = ROPE_THETA ** (-jnp.arange(0, MLA_ROPE_DIM, 2, dtype=jnp.float32) / MLA_ROPE_DIM)
    ang = jnp.arange(n_pos, dtype=jnp.float32)[:, None] * inv[None, :]
    cos, sin = jnp.cos(ang), jnp.sin(ang)
    z32 = jnp.zeros_like(cos)
    z64 = jnp.zeros((n_pos, LANES - MLA_ROPE_DIM), jnp.float32)
    c = jnp.concatenate([cos, cos, z64], axis=1)
    sa = jnp.concatenate([-sin, z32, z64], axis=1)
    sb = jnp.concatenate([z32, sin, z64], axis=1)
    return c, sa, sb


def _col_scale(n, n_scaled, value):
    return jnp.where(jnp.arange(n) < n_scaled, value, 1.0).astype(jnp.float32)[None, :]


def _even_weights(w_in, w_uq, w_ukv, gkv):
    o = [0]
    for s in (FOX_HEADS * FOX_DIM,) * 3 + (FOX_HEADS, MLA_Q_RANK, MLA_KV_RANK, MLA_ROPE_DIM):
        o.append(o[-1] + s)
    bf = jnp.bfloat16
    mla_scale = (MLA_NOPE_DIM + MLA_ROPE_DIM) ** -0.5 * LOG2E
    pad = lambda w, n: jnp.pad(w, ((0, 0), (0, n - w.shape[1])))
    w_small = jnp.concatenate([w_in[:, o[4]:o[5]], pad(w_in[:, o[5]:o[6]], KV_RANK_PAD),
                               pad(w_in[:, o[6]:o[7]], LANES), pad(w_in[:, o[3]:o[4]], LANES)],
                              axis=1).astype(bf)
    hq = (w_uq * mla_scale).reshape(MLA_Q_RANK, MLA_HEADS, MLA_NOPE_DIM + MLA_ROPE_DIM)
    hq = jnp.pad(hq, ((0, 0), (0, 0), (0, 2 * LANES - hq.shape[-1])))
    w_uq_p = hq.reshape(MLA_Q_RANK, MLA_HEADS * 2 * LANES).astype(bf)
    w_ukv_p = jnp.pad(w_ukv, ((0, KV_RANK_PAD - MLA_KV_RANK), (0, 0))).astype(bf)
    gkv_p = jnp.pad(gkv, (0, KV_RANK_PAD - MLA_KV_RANK))
    return w_small, w_uq_p, w_ukv_p, gkv_p


def _even_mixer(hn, hn_m, b, seq, i, ev_w_in, b_f, gq, gkv, w_uq, w_ukv, ev_w_out, tabs, tabs_m):
    w_small, w_uq_p, w_ukv_p, gkv_p = _even_weights(ev_w_in[i], w_uq, w_ukv, gkv)
    f0 = MLA_Q_RANK + KV_RANK_PAD + LANES
    n_q = FOX_HEADS * FOX_DIM
    qkv, qkv_m = matmul_ws(hn, hn_m, ev_w_in, i, 3 * n_q, jnp.bfloat16,
                           _col_scale(3 * n_q, n_q, FOX_DIM ** -0.5 * LOG2E))

    def stream(x, tab, n_pos):
        small = matmul(x, w_small, jnp.float32, tn=EVEN_SMALL // 2)
        cq, ckv, kr = mla_latent(small, gq, gkv_p, tab, n_pos)
        q = q_rope(matmul(cq, w_uq_p, jnp.float32), tab, n_pos)
        kv = matmul(ckv, w_ukv_p, jnp.bfloat16)
        return small[:, f0:f0 + FOX_HEADS], q, kv, kr

    fl, q, kv, kr = stream(hn, tabs, seq)
    fl_m, q_m, kv_m, kr_m = stream(hn_m, tabs_m, N_META)

    cum = forget_cumsum(fl.reshape(b, seq, FOX_HEADS).transpose(0, 2, 1), b_f)
    fl_m = jnp.pad(fl_m.T, ((0, 0), (0, LANES - N_META)))[None]
    cum_m = forget_cumsum(fl_m, b_f)[0, :, :N_META]
    mask = _pad_mask(FOX_HEADS)
    front = lambda x: jnp.pad(x, ((0, 0), (META_BLOCK - N_META, 0)))
    extra = _extra_lanes(-LOG2E * cum)
    extra_meta = _extra_lanes(front(LOG2E * (cum_m[:, -1:] - cum_m)), mask)
    extra_meta_only = _extra_lanes(front(-LOG2E * cum_m), mask)

    qkv_mp, q_mp, kv_mp, kr_mp = map(_front_pad, (qkv_m, q_m, kv_m, kr_m))
    kr_mp = kr_mp.at[:META_BLOCK - N_META, MLA_MASK_LANE].set(MASK_VALUE)
    o_f = fox_attention(qkv, qkv_mp, extra, extra_meta, b, seq)
    o_f_m = fox_attention(None, qkv_mp, None, extra_meta_only, b, seq)
    o_m = mla_attention(q, kv, kr, q_mp, kv_mp, kr_mp, b, seq)
    o_m_m = mla_attention(None, None, None, q_mp, kv_mp, kr_mp, b, seq)
    o = jnp.concatenate([o_f, o_m], axis=1)
    o_meta = jnp.concatenate([o_f_m, o_m_m], axis=1)[-N_META:]
    return matmul_ws(o, o_meta, ev_w_out, i, D_MODEL, jnp.float32, tm=512)


def _diff_mixer(hn, hn_m, b, seq, i, od_w_in, lam, subln, od_w_out, bias, lam_init):
    n_q = DIFF_HEADS * 2 * DIFF_HEAD_DIM
    extra, extra_meta, tiles = bias
    qkv, qkv_m = matmul_ws(hn, hn_m, od_w_in, i, 3 * n_q, jnp.bfloat16,
                           _col_scale(3 * n_q, n_q, DIFF_HEAD_DIM ** -0.5 * LOG2E))
    qkv_mp = _front_pad(qkv_m)
    o = diff_attention(qkv, qkv_mp, extra, extra_meta, tiles, lam, subln, lam_init, b, seq)
    o_meta = diff_attention(None, qkv_mp, None, extra_meta, tiles, lam, subln, lam_init, b, seq)
    return matmul_ws(o, o_meta[-N_META:], od_w_out, i, D_MODEL, jnp.float32, tm=512)


def _diff_bias(rel_bias, t):
    rb_t = rel_bias.astype(jnp.float32).T
    far = LOG2E * rb_t[:, REL_BUCKETS - 1]
    extra = _extra_lanes(jnp.broadcast_to(far[:, None], (DIFF_HEADS, t)))
    extra_meta = _extra_lanes(jnp.broadcast_to(far[:, None], (DIFF_HEADS, META_BLOCK)), _pad_mask(DIFF_HEADS))
    tile_meta0 = t5_tiles(rb_t, META_BLOCK, t, META_BLOCK, False)
    tile_sub, tile_diag = t5_tiles(rb_t, t, t, t, False), t5_tiles(rb_t, t, t, 0, True)
    tiles = (t5_tiles(rb_t, META_BLOCK, META_BLOCK, 0, True),
             jnp.stack([tile_meta0, jnp.zeros_like(tile_meta0)], axis=1),
             jnp.stack([jnp.zeros_like(tile_sub), tile_sub, tile_diag], axis=1))
    return extra, extra_meta, tiles


def kernel(x, meta_tokens, rel_bias, ev_w_in, ev_b_f, ev_q_norm, ev_kv_norm, ev_w_uq, ev_w_ukv, ev_w_out,
           od_w_in, od_lambda, od_subln, od_w_out, norm_g, ffn_w_gate, ffn_w_up, ffn_w_down):
    b, seq, d = x.shape
    hs = x.reshape(b * seq, d)
    hs_m = meta_tokens.astype(x.dtype)

    c, sa, sb = _rope_tables(N_META + seq)
    tabs_m = tuple(tb[:N_META] for tb in (c, sa, sb))
    tabs = tuple(tb[N_META:] for tb in (c, sa, sb))
    bias = _diff_bias(rel_bias, _tile(seq, ATTN_TILE))

    hn = rmsnorm_bf16(hs, norm_g[0, 0])
    hn_m = rmsnorm_bf16(hs_m, norm_g[0, 0])
    for layer in range(DEPTH):
        g = norm_g[layer]
        i = layer // 2
        if layer % 2 == 0:
            m, m_m = _even_mixer(hn, hn_m, b, seq, i, ev_w_in, ev_b_f[i], ev_q_norm[i], ev_kv_norm[i],
                                 ev_w_uq[i], ev_w_ukv[i], ev_w_out, tabs, tabs_m)
        else:
            lam_init = 0.8 - 0.6 * math.exp(-0.3 * layer)
            m, m_m = _diff_mixer(hn, hn_m, b, seq, i, od_w_in, od_lambda[i], od_subln[i], od_w_out,
                                 bias, lam_init)
        hs, hn = resid_norm(hs, m, g[1], g[2])
        hs_m, hn_m = resid_norm(hs_m, m_m, g[1], g[2])
        h, h_m = ffn_up(hn, hn_m, ffn_w_gate, ffn_w_up, layer)
        wd = jnp.pad(ffn_w_down[layer], ((0, D_FF_PAD - D_FF), (0, 0))).astype(jnp.bfloat16)
        f, f_m = matmul_ktiled(h, wd), matmul_ktiled(h_m, wd)
        g_next = norm_g[layer + 1, 0] if layer + 1 < DEPTH else None
        hs, hn = resid_norm(hs, f, g[3], g_next)
        if g_next is not None:
            hs_m, hn_m = resid_norm(hs_m, f_m, g[3], g_next)
    return hs.reshape(b, seq, d)
```

```python
import functools
import math

import jax
import jax.numpy as jnp
from jax import lax
from jax.experimental import pallas as pl
from jax.experimental.pallas import tpu as pltpu

D_MODEL = 4096
DEPTH = 4
N_META = 16
EPS = 1e-6
FOX_HEADS = 16
FOX_DIM = 128
MLA_HEADS = 16
MLA_Q_RANK = 896
MLA_KV_RANK = 320
MLA_NOPE_DIM = 128
MLA_ROPE_DIM = 64
MLA_V_DIM = 128
ROPE_THETA = 10000.0
DIFF_HEAD_DIM = 128
DIFF_HEADS = D_MODEL // (2 * DIFF_HEAD_DIM)
REL_BUCKETS = 32
REL_MAX_DIST = 128
D_FF = ((8 * D_MODEL + 3 * 256 - 1) // (3 * 256)) * 256

LANES = 128
BF16_ROWS = 16
META_BLOCK = LANES
MASK_VALUE = -1e30
VMEM_LIMIT_BYTES = 56 * 1024 * 1024
LOG2E = math.log2(math.e)

FF_TILE = 512
D_FF_PAD = ((D_FF + FF_TILE - 1) // FF_TILE) * FF_TILE
KV_RANK_PAD = 384
EVEN_SMALL = MLA_Q_RANK + KV_RANK_PAD + 2 * LANES
ATTN_TILE = 512
BIAS_SLOTS = 4
MLA_MASK_LANE = MLA_ROPE_DIM
SWEEP_PAIRS_PER_TRIP = 2


def _params(*sem):
    return pltpu.CompilerParams(dimension_semantics=sem, vmem_limit_bytes=VMEM_LIMIT_BYTES)


def _tile(n, t, unit=LANES):
    if n <= t:
        return n
    return max(c for c in range(unit, t + 1, unit) if n % c == 0)


def _rms(x, g, n):
    ms = jnp.sum(x * x, axis=-1, keepdims=True) / n
    return x * lax.rsqrt(ms + EPS) * g


def _norm_kernel(x_ref, g_ref, o_ref):
    o_ref[...] = _rms(x_ref[...], g_ref[...], x_ref.shape[-1]).astype(o_ref.dtype)


def rmsnorm_bf16(x, g, tm=256):
    m, d = x.shape
    tm = _tile(m, tm)
    return pl.pallas_call(
        _norm_kernel,
        grid=(m // tm,),
        in_specs=[pl.BlockSpec((tm, d), lambda i: (i, 0)), pl.BlockSpec((1, d), lambda i: (0, 0))],
        out_specs=pl.BlockSpec((tm, d), lambda i: (i, 0)),
        out_shape=jax.ShapeDtypeStruct((m, d), jnp.bfloat16),
        compiler_params=_params("arbitrary"),
        name="rmsnorm",
    )(x, g.reshape(1, d))


def _resid_norm_kernel(hs_ref, m_ref, g1_ref, g2_ref, hs_out_ref, hn_out_ref):
    d = hs_ref.shape[-1]
    hs = hs_ref[...] + _rms(m_ref[...], g1_ref[...], d)
    hs_out_ref[...] = hs
    hn_out_ref[...] = _rms(hs, g2_ref[...], d).astype(hn_out_ref.dtype)


def _resid_kernel(hs_ref, m_ref, g1_ref, hs_out_ref):
    hs_out_ref[...] = hs_ref[...] + _rms(m_ref[...], g1_ref[...], hs_ref.shape[-1])


def resid_norm(hs, m, g1, g2, tm=256):
    rows, d = hs.shape
    tm = _tile(rows, tm)
    row = pl.BlockSpec((tm, d), lambda i: (i, 0))
    vec = pl.BlockSpec((1, d), lambda i: (0, 0))
    if g2 is None:
        return pl.pallas_call(
            _resid_kernel, grid=(rows // tm,), in_specs=[row, row, vec], out_specs=row,
            out_shape=jax.ShapeDtypeStruct((rows, d), jnp.float32),
            input_output_aliases={0: 0}, compiler_params=_params("arbitrary"), name="resid",
        )(hs, m, g1.reshape(1, d)), None
    return pl.pallas_call(
        _resid_norm_kernel, grid=(rows // tm,), in_specs=[row, row, vec, vec], out_specs=[row, row],
        out_shape=[jax.ShapeDtypeStruct((rows, d), jnp.float32),
                   jax.ShapeDtypeStruct((rows, d), jnp.bfloat16)],
        input_output_aliases={0: 0}, compiler_params=_params("arbitrary"), name="resid_norm",
    )(hs, m, g1.reshape(1, d), g2.reshape(1, d))


def _mm_kernel(x_ref, w_ref, o_ref):
    o_ref[...] = jnp.dot(x_ref[...], w_ref[...], preferred_element_type=jnp.float32).astype(o_ref.dtype)


def matmul(x, w, out_dtype, tm=1024, tn=1024):
    m, k = x.shape
    n = w.shape[1]
    tm, tn = _tile(m, tm), _tile(n, tn)
    return pl.pallas_call(
        _mm_kernel,
        grid=(m // tm, n // tn),
        in_specs=[pl.BlockSpec((tm, k), lambda i, j: (i, 0)), pl.BlockSpec((k, tn), lambda i, j: (0, j))],
        out_specs=pl.BlockSpec((tm, tn), lambda i, j: (i, j)),
        out_shape=jax.ShapeDtypeStruct((m, n), out_dtype),
        compiler_params=_params("arbitrary", "arbitrary"),
        name="matmul",
    )(x, w)


def _mm_ws_kernel(*refs, scaled):
    if scaled:
        x_ref, xm_ref, w_ref, s_ref, o_ref, om_ref, wb_ref = refs
    else:
        x_ref, xm_ref, w_ref, o_ref, om_ref, wb_ref = refs

    @pl.when(pl.program_id(1) == 0)
    def _():
        w = w_ref[...]
        if scaled:
            w = w * s_ref[...]
        wb_ref[...] = w.astype(wb_ref.dtype)
        om_ref[...] = jnp.dot(xm_ref[...], wb_ref[...], preferred_element_type=jnp.float32).astype(om_ref.dtype)

    o_ref[...] = jnp.dot(x_ref[...], wb_ref[...], preferred_element_type=jnp.float32).astype(o_ref.dtype)


def matmul_ws(x, x_meta, w_stack, layer, n, out_dtype, scale=None, tm=1024, tn=1024):
    m, k = x.shape
    tm, tn = _tile(m, tm), _tile(n, tn)
    scaled = scale is not None
    in_specs = [pl.BlockSpec((tm, k), lambda j, i: (i, 0)),
                pl.BlockSpec(x_meta.shape, lambda j, i: (0, 0)),
                pl.BlockSpec((None, k, tn), lambda j, i: (layer, 0, j), pipeline_mode=pl.Buffered(1))]
    args = [x, x_meta, w_stack]
    if scaled:
        in_specs.append(pl.BlockSpec((1, tn), lambda j, i: (0, j)))
        args.append(scale)
    mm = x_meta.shape[0]
    return pl.pallas_call(
        functools.partial(_mm_ws_kernel, scaled=scaled),
        grid=(n // tn, m // tm),
        in_specs=in_specs,
        out_specs=[pl.BlockSpec((tm, tn), lambda j, i: (i, j)), pl.BlockSpec((mm, tn), lambda j, i: (0, j))],
        out_shape=[jax.ShapeDtypeStruct((m, n), out_dtype), jax.ShapeDtypeStruct((mm, n), out_dtype)],
        scratch_shapes=[pltpu.VMEM((k, tn), jnp.bfloat16)],
        compiler_params=_params("arbitrary", "arbitrary"),
        name="matmul_ws",
    )(*args)


def _mm_acc_kernel(x_ref, w_ref, o_ref):
    d = jnp.dot(x_ref[...], w_ref[...], preferred_element_type=jnp.float32)
    k = pl.program_id(2)

    @pl.when(k == 0)
    def _():
        o_ref[...] = d

    @pl.when(k > 0)
    def _():
        o_ref[...] += d


def matmul_ktiled(x, w, tm=1024, tn=1024, tk=2816):
    m, k = x.shape
    n = w.shape[1]
    tm, tn, tk = _tile(m, tm), _tile(n, tn), _tile(k, tk)
    return pl.pallas_call(
        _mm_acc_kernel,
        grid=(m // tm, n // tn, k // tk),
        in_specs=[pl.BlockSpec((tm, tk), lambda i, j, kk: (i, kk)),
                  pl.BlockSpec((tk, tn), lambda i, j, kk: (kk, j))],
        out_specs=pl.BlockSpec((tm, tn), lambda i, j, kk: (i, j)),
        out_shape=jax.ShapeDtypeStruct((m, n), jnp.float32),
        compiler_params=_params("arbitrary", "arbitrary", "arbitrary"),
        name="matmul_ktiled",
    )(x, w)


def _ffn_up_kernel(x_ref, xm_ref, wg_ref, wu_ref, o_ref, om_ref, wgb_ref, wub_ref, *, d_ff):
    tf = o_ref.shape[-1]
    first_col = pl.program_id(0) * tf

    def act(x):
        g = jnp.dot(x, wgb_ref[...], preferred_element_type=jnp.float32)
        u = jnp.dot(x, wub_ref[...], preferred_element_type=jnp.float32)
        h = (g / (1.0 + jnp.exp(-g))) * u
        col = first_col + lax.broadcasted_iota(jnp.int32, h.shape, 1)
        return jnp.where(col < d_ff, h, 0.0).astype(o_ref.dtype)

    @pl.when(pl.program_id(1) == 0)
    def _():
        wgb_ref[...] = wg_ref[...].astype(wgb_ref.dtype)
        wub_ref[...] = wu_ref[...].astype(wub_ref.dtype)
        om_ref[...] = act(xm_ref[...])

    o_ref[...] = act(x_ref[...])


def ffn_up(x, x_meta, wg_stack, wu_stack, layer, tm=1024, tf=FF_TILE):
    m, k = x.shape
    d_ff = wg_stack.shape[-1]
    tm = _tile(m, tm)
    mm = x_meta.shape[0]
    last = pl.cdiv(d_ff, tf) - 1
    wspec = pl.BlockSpec((None, k, tf), lambda j, i: (layer, 0, jnp.minimum(j, last)), pipeline_mode=pl.Buffered(1))
    return pl.pallas_call(
        functools.partial(_ffn_up_kernel, d_ff=d_ff),
        grid=(D_FF_PAD // tf, m // tm),
        in_specs=[pl.BlockSpec((tm, k), lambda j, i: (i, 0)), pl.BlockSpec((mm, k), lambda j, i: (0, 0)),
                  wspec, wspec],
        out_specs=[pl.BlockSpec((tm, tf), lambda j, i: (i, j)), pl.BlockSpec((mm, tf), lambda j, i: (0, j))],
        out_shape=[jax.ShapeDtypeStruct((m, D_FF_PAD), jnp.bfloat16),
                   jax.ShapeDtypeStruct((mm, D_FF_PAD), jnp.bfloat16)],
        scratch_shapes=[pltpu.VMEM((k, tf), jnp.bfloat16), pltpu.VMEM((k, tf), jnp.bfloat16)],
        compiler_params=_params("arbitrary", "arbitrary"),
        name="ffn_up",
    )(x, x_meta, wg_stack, wu_stack)


def _rope128(r, c, sa, sb):
    return r * c + pltpu.roll(r, LANES - MLA_ROPE_DIM // 2, 1) * sa + pltpu.roll(r, MLA_ROPE_DIM // 2, 1) * sb


def _mla_latent_kernel(s_ref, gq_ref, gkv_ref, c_ref, sa_ref, sb_ref, cq_ref, ckv_ref, kr_ref):
    q0, kv0, r0 = 0, MLA_Q_RANK, MLA_Q_RANK + KV_RANK_PAD
    cq_ref[...] = _rms(s_ref[:, q0:kv0], gq_ref[...], MLA_Q_RANK).astype(cq_ref.dtype)
    ckv_ref[...] = _rms(s_ref[:, kv0:r0], gkv_ref[...], MLA_KV_RANK).astype(ckv_ref.dtype)
    kr_ref[...] = _rope128(s_ref[:, r0:r0 + LANES], c_ref[...], sa_ref[...], sb_ref[...]).astype(kr_ref.dtype)


def mla_latent(small, gq, gkv_pad, tabs, seq, tm=512):
    m = small.shape[0]
    tm = _tile(m, tm)
    nseq = seq // tm
    tab = pl.BlockSpec((tm, LANES), lambda i: (i % nseq, 0))
    return pl.pallas_call(
        _mla_latent_kernel,
        grid=(m // tm,),
        in_specs=[pl.BlockSpec((tm, EVEN_SMALL), lambda i: (i, 0)),
                  pl.BlockSpec((1, MLA_Q_RANK), lambda i: (0, 0)),
                  pl.BlockSpec((1, KV_RANK_PAD), lambda i: (0, 0)), tab, tab, tab],
        out_specs=[pl.BlockSpec((tm, MLA_Q_RANK), lambda i: (i, 0)),
                   pl.BlockSpec((tm, KV_RANK_PAD), lambda i: (i, 0)),
                   pl.BlockSpec((tm, LANES), lambda i: (i, 0))],
        out_shape=[jax.ShapeDtypeStruct((m, MLA_Q_RANK), jnp.bfloat16),
                   jax.ShapeDtypeStruct((m, KV_RANK_PAD), jnp.bfloat16),
                   jax.ShapeDtypeStruct((m, LANES), jnp.bfloat16)],
        compiler_params=_params("arbitrary"),
        name="mla_latent",
    )(small, gq.reshape(1, -1), gkv_pad.reshape(1, -1), *tabs)


def _uq_rope_kernel(x_ref, w_ref, c_ref, sa_ref, sb_ref, o_ref):
    q = jnp.dot(x_ref[...], w_ref[...], preferred_element_type=jnp.float32)
    c, sa, sb = c_ref[...], sa_ref[...], sb_ref[...]
    one = (lax.broadcasted_iota(jnp.int32, (1, LANES), 1) == MLA_MASK_LANE).astype(jnp.float32)
    for h in range(q.shape[1] // (2 * LANES)):
        n0, r0 = 2 * h * LANES, (2 * h + 1) * LANES
        o_ref[:, n0:r0] = q[:, n0:r0].astype(o_ref.dtype)
        o_ref[:, r0:r0 + LANES] = (_rope128(q[:, r0:r0 + LANES], c, sa, sb) + one).astype(o_ref.dtype)


def uq_rope(cq, w_uq_p, tabs, seq, tm=1024, tn=1024):
    m, k = cq.shape
    n = w_uq_p.shape[1]
    tm, tn = _tile(seq, tm), _tile(n, tn, 2 * LANES)
    nseq = seq // tm
    tab = pl.BlockSpec((tm, LANES), lambda i, j: (i % nseq, 0))
    return pl.pallas_call(
        _uq_rope_kernel, grid=(m // tm, n // tn),
        in_specs=[pl.BlockSpec((tm, k), lambda i, j: (i, 0)), pl.BlockSpec((k, tn), lambda i, j: (0, j)),
                  tab, tab, tab],
        out_specs=pl.BlockSpec((tm, tn), lambda i, j: (i, j)),
        out_shape=jax.ShapeDtypeStruct((m, n), jnp.bfloat16),
        compiler_params=_params("arbitrary", "arbitrary"), name="uq_rope",
    )(cq, w_uq_p, *tabs)


def _forget_cumsum_kernel(x_ref, b_ref, o_ref):
    z = x_ref[...] + b_ref[...]
    log_f = jnp.minimum(z, 0.0) - jnp.log(1.0 + jnp.exp(-jnp.abs(z)))
    n = log_f.shape[-1]
    row = lax.broadcasted_iota(jnp.int32, (LANES, LANES), 0)
    col = lax.broadcasted_iota(jnp.int32, (LANES, LANES), 1)
    upper = (row <= col).astype(jnp.float32)
    carry = jnp.zeros((log_f.shape[0], 1), jnp.float32)
    for c in range(n // LANES):
        chunk = jnp.dot(log_f[:, c * LANES:(c + 1) * LANES], upper,
                        precision=lax.Precision.HIGHEST, preferred_element_type=jnp.float32) + carry
        o_ref[:, c * LANES:(c + 1) * LANES] = chunk
        carry = chunk[:, LANES - 1:LANES]


def forget_cumsum(f_logit, b_f):
    b, h, n = f_logit.shape
    return pl.pallas_call(
        _forget_cumsum_kernel,
        grid=(b,),
        in_specs=[pl.BlockSpec((None, h, n), lambda i: (i, 0, 0)), pl.BlockSpec((h, 1), lambda i: (0, 0))],
        out_specs=pl.BlockSpec((None, h, n), lambda i: (i, 0, 0)),
        out_shape=jax.ShapeDtypeStruct((b, h, n), jnp.float32),
        compiler_params=_params("arbitrary"),
        name="forget_cumsum",
    )(f_logit, b_f.reshape(h, 1))


def _t5_tile_kernel(rb_ref, o_ref, *, offset, causal):
    h = pl.program_id(0)
    shape = o_ref.shape
    i = lax.broadcasted_iota(jnp.int32, shape, 0)
    j = lax.broadcasted_iota(jnp.int32, shape, 1)
    dist = jnp.maximum(j - i + offset, 0)
    max_exact = REL_BUCKETS // 2
    d = jnp.maximum(dist, 1).astype(jnp.float32)
    large = max_exact + (jnp.log(d / max_exact) / math.log(REL_MAX_DIST / max_exact)
                         * (REL_BUCKETS - max_exact)).astype(jnp.int32)
    large = jnp.minimum(large, REL_BUCKETS - 1)
    bucket = jnp.where(dist < max_exact, dist, large)
    far = rb_ref[h, REL_BUCKETS - 1]
    tile = jnp.zeros(shape, jnp.float32)
    for b in range(REL_BUCKETS):
        tile = jnp.where(bucket == b, (rb_ref[h, b] - far) * LOG2E, tile)
    if causal:
        tile = jnp.where(i > j, MASK_VALUE, tile)
    o_ref[...] = tile


def t5_tiles(rel_bias_t, rows, cols, offset, causal):
    h = rel_bias_t.shape[0]
    return pl.pallas_call(
        functools.partial(_t5_tile_kernel, offset=offset, causal=causal),
        grid=(h,),
        in_specs=[pl.BlockSpec(memory_space=pltpu.SMEM)],
        out_specs=pl.BlockSpec((None, rows, cols), lambda i: (i, 0, 0)),
        out_shape=jax.ShapeDtypeStruct((h, rows, cols), jnp.float32),
        compiler_params=_params("arbitrary"),
        name="t5_tiles",
    )(rel_bias_t)


def _q_aug(q):
    ones = (lax.broadcasted_iota(jnp.int32, (LANES, q.shape[0]), 0) < BIAS_SLOTS).astype(q.dtype)
    return jnp.concatenate([q.T, ones], axis=0)


def _v_aug(v):
    return jnp.concatenate([v.T, jnp.ones((BF16_ROWS, v.shape[0]), v.dtype)], axis=0)


def _k_aug(k, extra):
    return jnp.concatenate([k, extra], axis=1)


def _init_state(tq, dv):
    return (jnp.full((1, tq), MASK_VALUE, jnp.float32), jnp.zeros((dv + BF16_ROWS, tq), jnp.float32))


def _score(k_aug, q_aug):
    return jnp.dot(k_aug, q_aug, preferred_element_type=jnp.float32)


def _causal(s):
    i = lax.broadcasted_iota(jnp.int32, s.shape, 0)
    j = lax.broadcasted_iota(jnp.int32, s.shape, 1)
    return jnp.where(i > j, MASK_VALUE, s)


def _consume(s, v_aug, state):
    m, acc = state
    m_new = jnp.maximum(m, jnp.max(s, axis=0, keepdims=True))
    p = jnp.exp2(s - m_new).astype(v_aug.dtype)
    acc = jnp.exp2(m - m_new) * acc + jnp.dot(v_aug, p, preferred_element_type=jnp.float32)
    return m_new, acc


def _next_tile(qb, kb):
    wrap = kb == qb
    return jnp.where(wrap, qb + 1, qb), jnp.where(wrap, 0, kb + 1)


def _flat_sweep(nq, score, consume, bufs):
    sa, sb = bufs
    n_steps = nq * (nq + 1) // 2 - 1
    zero = jnp.int32(0)
    score(zero, zero, sa)

    def pair(tile):
        t1 = _next_tile(*tile)
        t2 = _next_tile(*t1)
        score(*t1, sb)
        consume(*tile, sa)
        score(*t2, sa)
        consume(*t1, sb)
        return t2

    def trip(_, tile):
        for _ in range(SWEEP_PAIRS_PER_TRIP):
            tile = pair(tile)
        return tile

    tile = lax.fori_loop(0, n_steps // (2 * SWEEP_PAIRS_PER_TRIP), trip, (zero, zero))
    rest = n_steps % (2 * SWEEP_PAIRS_PER_TRIP)
    for _ in range(rest // 2):
        tile = pair(tile)
    if rest % 2:
        t1 = _next_tile(*tile)
        score(*t1, sb)
        consume(*tile, sa)
        consume(*t1, sb)
    else:
        consume(*tile, sa)


def _finish(state, dv):
    _, acc = state
    return (acc[:dv] * (1.0 / acc[dv:dv + 1])).T


def _resume(first, m, acc):
    return jnp.where(first, MASK_VALUE, m), jnp.where(first, 0.0, acc)


def _single_meta_kernel(q_ref, km_ref, xm_ref, vm_ref, o_ref, *, rope_keys):
    q_aug = q_ref[...].T if rope_keys else _q_aug(q_ref[...])
    dv = vm_ref.shape[-1]
    s = _causal(_score(_k_aug(km_ref[...], xm_ref[...]), q_aug))
    state = _consume(s, _v_aug(vm_ref[...]), _init_state(q_aug.shape[1], dv))
    o_ref[...] = _finish(state, dv).astype(o_ref.dtype)


def _single_kernel(q_ref, kr_ref, xr_ref, vr_ref, km_ref, xm_ref, vm_ref, add_ref, o_ref,
                   kar_ref, var_ref, kam_ref, vam_ref, sa_ref, sb_ref, qa_ref, m_ref, acc_ref, *, rope_keys):
    nq, t = kr_ref.shape[0], kr_ref.shape[1]
    dv = vm_ref.shape[-1]
    kam_ref[...] = _k_aug(km_ref[...], xm_ref[...])
    vam_ref[...] = _v_aug(vm_ref[...])
    for blk in range(nq):
        kar_ref[blk] = _k_aug(kr_ref[blk], xr_ref[blk])
        var_ref[blk] = _v_aug(vr_ref[blk])
        q = q_ref[blk * t:(blk + 1) * t, :]
        qa_ref[blk] = q.T if rope_keys else _q_aug(q)

    def score(qb, kb, buf):
        buf[...] = _score(kar_ref[kb], qa_ref[qb])

    def consume(qb, kb, buf):
        s = buf[...] + add_ref[(kb == qb).astype(jnp.int32)]
        m_ref[qb], acc_ref[qb] = _consume(s, var_ref[kb], _resume(kb == 0, m_ref[qb], acc_ref[qb]))

    _flat_sweep(nq, score, consume, (sa_ref, sb_ref))
    s_meta = [_score(kam_ref[...], qa_ref[qb]) for qb in range(nq)]
    for qb in range(nq):
        state = _consume(s_meta[qb], vam_ref[...], (m_ref[qb], acc_ref[qb]))
        o_ref[qb * t:(qb + 1) * t, :] = _finish(state, dv).astype(o_ref.dtype)


def _diff_out(outs, lam_ref, g_ref, lam_init, dv):
    lam = lam_ref[...]
    lam_full = (jnp.exp(jnp.sum(lam[0:1] * lam[1:2], axis=-1, keepdims=True))
                - jnp.exp(jnp.sum(lam[2:3] * lam[3:4], axis=-1, keepdims=True)) + lam_init)
    return _rms(outs[0] - lam_full * outs[1], g_ref[...], dv) * (1.0 - lam_init)


def _diff_meta_kernel(q_ref, km_ref, xm_ref, vm_ref, tm_ref, lam_ref, g_ref, o_ref, *, lam_init):
    d = DIFF_HEAD_DIM
    dv = vm_ref.shape[-1]
    km, xm, v_aug = km_ref[...], xm_ref[...], _v_aug(vm_ref[...])
    outs = []
    for c in range(2):
        q_aug = _q_aug(q_ref[:, c * d:(c + 1) * d])
        s = _score(_k_aug(km[:, c * d:(c + 1) * d], xm), q_aug) + tm_ref[...]
        outs.append(_finish(_consume(s, v_aug, _init_state(q_aug.shape[1], dv)), dv))
    o_ref[...] = _diff_out(outs, lam_ref, g_ref, lam_init, dv).astype(o_ref.dtype)


def _diff_kernel(q_ref, kr_ref, xr_ref, vr_ref, km_ref, xm_ref, vm_ref, tm_ref, add_ref, lam_ref, g_ref, o_ref,
                 kar_ref, var_ref, kam_ref, vam_ref, sa_ref, sb_ref, qa_ref, m_ref, acc_ref, *, lam_init):
    nq, t = kr_ref.shape[0], kr_ref.shape[1]
    d = DIFF_HEAD_DIM
    dv = vm_ref.shape[-1]
    km, xm, xr = km_ref[...], xm_ref[...], xr_ref[...]
    vam_ref[...] = _v_aug(vm_ref[...])
    for c in range(2):
        kam_ref[c] = _k_aug(km[:, c * d:(c + 1) * d], xm)
    for blk in range(nq):
        var_ref[blk] = _v_aug(vr_ref[blk])
        for c in range(2):
            kar_ref[c, blk] = _k_aug(kr_ref[blk, :, c * d:(c + 1) * d], xr)
            qa_ref[c, blk] = _q_aug(q_ref[blk * t:(blk + 1) * t, c * d:(c + 1) * d])

    def score(qb, kb, buf):
        for c in range(2):
            buf[c] = _score(kar_ref[c, kb], qa_ref[c, qb])

    def consume(qb, kb, buf):
        add = add_ref[jnp.clip(kb - qb + 2, 0, 2)]
        for c in range(2):
            m_ref[c, qb], acc_ref[c, qb] = _consume(buf[c] + add, var_ref[kb],
                                                    _resume(kb == 0, m_ref[c, qb], acc_ref[c, qb]))

    _flat_sweep(nq, score, consume, (sa_ref, sb_ref))
    for qb in range(nq):
        outs = []
        for c in range(2):
            s = _score(kam_ref[c], qa_ref[c, qb]) + tm_ref[min(qb, 1)]
            outs.append(_finish(_consume(s, vam_ref[...], (m_ref[c, qb], acc_ref[c, qb])), dv))
        o_ref[qb * t:(qb + 1) * t, :] = _diff_out(outs, lam_ref, g_ref, lam_init, dv).astype(o_ref.dtype)


def _attn_call(kernel, name, b, heads, nq, tq, dv, in_specs, args, scratch=()):
    return pl.pallas_call(
        kernel,
        grid=(b, heads, nq),
        in_specs=in_specs,
        out_specs=pl.BlockSpec((tq, dv), lambda bi, h, qi: (bi * nq + qi, h)),
        out_shape=jax.ShapeDtypeStruct((b * nq * tq, heads * dv), jnp.bfloat16),
        scratch_shapes=list(scratch),
        compiler_params=_params("arbitrary", "arbitrary", "arbitrary"),
        name=name,
    )(*args)


def _aug_scratch(n_comp, nkb, t, dv):
    lead = (n_comp,) if n_comp > 1 else ()
    bf = jnp.bfloat16
    return [pltpu.VMEM(lead + (nkb, t, 2 * LANES), bf), pltpu.VMEM((nkb, dv + BF16_ROWS, t), bf),
            pltpu.VMEM(lead + (META_BLOCK, 2 * LANES), bf), pltpu.VMEM((dv + BF16_ROWS, META_BLOCK), bf),
            pltpu.VMEM(lead + (t, t), jnp.float32), pltpu.VMEM(lead + (t, t), jnp.float32),
            pltpu.VMEM(lead + (nkb, 2 * LANES, t), bf), pltpu.VMEM(lead + (nkb, 1, t), jnp.float32),
            pltpu.VMEM(lead + (nkb, dv + BF16_ROWS, t), jnp.float32)]


def _causal_tiles(t):
    i = lax.broadcasted_iota(jnp.int32, (t, t), 0)
    j = lax.broadcasted_iota(jnp.int32, (t, t), 1)
    return jnp.stack([jnp.zeros((t, t), jnp.float32), jnp.where(i > j, MASK_VALUE, 0.0).astype(jnp.float32)])


def _whole(shape):
    return pl.BlockSpec(shape, lambda bi, h, qi: (0,) * len(shape))


def _qspec(tq, dq, nq, col=lambda h: h):
    return pl.BlockSpec((tq, dq), lambda bi, h, qi: (bi * nq + qi, col(h)))


def _kspec(nkb, tk, d, col):
    return pl.BlockSpec((None, nkb, tk, d), lambda bi, h, qi: (bi, 0, 0, col(h)))


def _mspec(d, col):
    return pl.BlockSpec((META_BLOCK, d), lambda bi, h, qi: (0, col(h)))


def fox_attention(qkv, qkv_meta, extra, extra_meta, b, seq, t=ATTN_TILE):
    hd = FOX_HEADS
    xm_spec = pl.BlockSpec((None, META_BLOCK, LANES), lambda bi, h, qi: (h, 0, 0))
    meta_specs = [_mspec(FOX_DIM, lambda h: hd + h), xm_spec, _mspec(FOX_DIM, lambda h: 2 * hd + h)]
    meta_args = (qkv_meta, extra_meta, qkv_meta)
    if qkv is None:
        return _attn_call(functools.partial(_single_meta_kernel, rope_keys=False),
                          "fox_attention_meta", 1, hd, 1, META_BLOCK, FOX_DIM,
                          [_qspec(META_BLOCK, FOX_DIM, 1)] + meta_specs, (qkv_meta,) + meta_args)
    t = _tile(seq, t)
    nq = seq // t
    kv4 = qkv.reshape(b, nq, t, qkv.shape[-1])
    return _attn_call(
        functools.partial(_single_kernel, rope_keys=False), "fox_attention",
        b, hd, 1, seq, FOX_DIM,
        [_qspec(seq, FOX_DIM, 1), _kspec(nq, t, FOX_DIM, lambda h: hd + h),
         pl.BlockSpec((None, None, nq, t, LANES), lambda bi, h, qi: (bi, h, 0, 0, 0)),
         _kspec(nq, t, FOX_DIM, lambda h: 2 * hd + h)] + meta_specs + [_whole((2, t, t))],
        (qkv, kv4, extra.reshape(b, hd, nq, t, LANES), kv4) + meta_args + (_causal_tiles(t),),
        _aug_scratch(1, nq, t, FOX_DIM))


def mla_attention(q, kv, kr, q_meta, kv_meta, kr_meta, b, seq, t=ATTN_TILE):
    hd = MLA_HEADS
    meta_specs = [_mspec(LANES, lambda h: 2 * h), _mspec(LANES, lambda h: 0), _mspec(LANES, lambda h: 2 * h + 1)]
    meta_args = (kv_meta, kr_meta, kv_meta)
    if q is None:
        return _attn_call(functools.partial(_single_meta_kernel, rope_keys=True),
                          "mla_attention_meta", 1, hd, 1, META_BLOCK, MLA_V_DIM,
                          [_qspec(META_BLOCK, 2 * LANES, 1)] + meta_specs, (q_meta,) + meta_args)
    t = _tile(seq, t)
    nq = seq // t
    kv4 = kv.reshape(b, nq, t, kv.shape[-1])
    kr4 = kr.reshape(b, nq, t, kr.shape[-1])
    return _attn_call(
        functools.partial(_single_kernel, rope_keys=True), "mla_attention",
        b, hd, 1, seq, MLA_V_DIM,
        [_qspec(seq, 2 * LANES, 1), _kspec(nq, t, LANES, lambda h: 2 * h),
         _kspec(nq, t, LANES, lambda h: 0), _kspec(nq, t, LANES, lambda h: 2 * h + 1)] + meta_specs
        + [_whole((2, t, t))],
        (q, kv4, kr4, kv4) + meta_args + (_causal_tiles(t),), _aug_scratch(1, nq, t, MLA_V_DIM))


def diff_attention(qkv, qkv_meta, extra, extra_meta, tiles, lam, subln, lam_init, b, seq, t=ATTN_TILE):
    hd = DIFF_HEADS
    dv = 2 * DIFF_HEAD_DIM
    tile_meta_only, tile_meta, tile_real = tiles
    lam_spec = pl.BlockSpec((4, DIFF_HEAD_DIM), lambda bi, h, qi: (0, 0))
    g_spec = pl.BlockSpec((1, dv), lambda bi, h, qi: (0, 0))
    kcol, vcol = (lambda h: hd + h), (lambda h: 2 * hd + h)
    per_head = lambda r, c: pl.BlockSpec((None, r, c), lambda bi, h, qi: (h, 0, 0))
    meta_specs = [_mspec(dv, kcol), per_head(META_BLOCK, LANES), _mspec(dv, vcol)]
    meta_args = (qkv_meta, extra_meta, qkv_meta)
    tail = (lam, subln.reshape(1, dv))
    if qkv is None:
        return _attn_call(
            functools.partial(_diff_meta_kernel, lam_init=lam_init),
            "diff_attention_meta", 1, hd, 1, META_BLOCK, dv,
            [_qspec(META_BLOCK, dv, 1)] + meta_specs + [per_head(META_BLOCK, META_BLOCK), lam_spec, g_spec],
            (qkv_meta,) + meta_args + (tile_meta_only,) + tail)
    t = _tile(seq, t)
    nq = seq // t
    kv4 = qkv.reshape(b, nq, t, qkv.shape[-1])
    return _attn_call(
        functools.partial(_diff_kernel, lam_init=lam_init),
        "diff_attention", b, hd, 1, seq, dv,
        [_qspec(seq, dv, 1), _kspec(nq, t, dv, kcol), per_head(t, LANES), _kspec(nq, t, dv, vcol)] + meta_specs
        + [pl.BlockSpec((None, 2, META_BLOCK, t), lambda bi, h, qi: (h, 0, 0, 0)),
           pl.BlockSpec((None, 3, t, t), lambda bi, h, qi: (h, 0, 0, 0)), lam_spec, g_spec],
        (qkv, kv4, extra, kv4) + meta_args + (tile_meta, tile_real) + tail,
        _aug_scratch(2, nq, t, dv))


def _front_pad(x):
    return jnp.pad(x, ((META_BLOCK - x.shape[0], 0), (0, 0)))


def _split3(x):
    def head(v):
        bits = lax.bitcast_convert_type(v, jnp.uint32) & jnp.uint32(0xFFFF0000)
        return lax.bitcast_convert_type(bits, jnp.float32)

    hi = head(x)
    mid = head(x - hi)
    lo = head(x - hi - mid)
    return tuple(p.astype(jnp.bfloat16) for p in (hi, mid, lo))


def _extra_lanes(x, mask=None):
    pieces = list(_split3(x)) + [jnp.zeros(x.shape, jnp.bfloat16) if mask is None else mask.astype(jnp.bfloat16)]
    out = jnp.stack(pieces, axis=-1)
    return jnp.pad(out, [(0, 0)] * x.ndim + [(0, LANES - BIAS_SLOTS)])


def _pad_mask(heads):
    m = jnp.where(jnp.arange(META_BLOCK) < META_BLOCK - N_META, MASK_VALUE, 0.0).astype(jnp.float32)
    return jnp.broadcast_to(m, (heads, META_BLOCK))


def _rope_tables(n_pos):
    inv = ROPE_THETA ** (-jnp.arange(0, MLA_ROPE_DIM, 2, dtype=jnp.float32) / MLA_ROPE_DIM)
    ang = jnp.arange(n_pos, dtype=jnp.float32)[:, None] * inv[None, :]
    cos, sin = jnp.cos(ang), jnp.sin(ang)
    z32 = jnp.zeros_like(cos)
    z64 = jnp.zeros((n_pos, LANES - MLA_ROPE_DIM), jnp.float32)
    c = jnp.concatenate([cos, cos, z64], axis=1)
    sa = jnp.concatenate([-sin, z32, z64], axis=1)
    sb = jnp.concatenate([z32, sin, z64], axis=1)
    return c, sa, sb


def _col_scale(n, n_scaled, value):
    return jnp.where(jnp.arange(n) < n_scaled, value, 1.0).astype(jnp.float32)[None, :]


def _even_weights(w_in, w_uq, w_ukv, gkv):
    o = [0]
    for s in (FOX_HEADS * FOX_DIM,) * 3 + (FOX_HEADS, MLA_Q_RANK, MLA_KV_RANK, MLA_ROPE_DIM):
        o.append(o[-1] + s)
    bf = jnp.bfloat16
    mla_scale = (MLA_NOPE_DIM + MLA_ROPE_DIM) ** -0.5 * LOG2E
    pad = lambda w, n: jnp.pad(w, ((0, 0), (0, n - w.shape[1])))
    w_small = jnp.concatenate([w_in[:, o[4]:o[5]], pad(w_in[:, o[5]:o[6]], KV_RANK_PAD),
                               pad(w_in[:, o[6]:o[7]], LANES), pad(w_in[:, o[3]:o[4]], LANES)],
                              axis=1).astype(bf)
    hq = (w_uq * mla_scale).reshape(MLA_Q_RANK, MLA_HEADS, MLA_NOPE_DIM + MLA_ROPE_DIM)
    hq = jnp.pad(hq, ((0, 0), (0, 0), (0, 2 * LANES - hq.shape[-1])))
    w_uq_p = hq.reshape(MLA_Q_RANK, MLA_HEADS * 2 * LANES).astype(bf)
    w_ukv_p = jnp.pad(w_ukv, ((0, KV_RANK_PAD - MLA_KV_RANK), (0, 0))).astype(bf)
    gkv_p = jnp.pad(gkv, (0, KV_RANK_PAD - MLA_KV_RANK))
    return w_small, w_uq_p, w_ukv_p, gkv_p


def _even_mixer(hn, hn_m, b, seq, i, ev_w_in, b_f, gq, gkv, w_uq, w_ukv, ev_w_out, tabs, tabs_m):
    w_small, w_uq_p, w_ukv_p, gkv_p = _even_weights(ev_w_in[i], w_uq, w_ukv, gkv)
    f0 = MLA_Q_RANK + KV_RANK_PAD + LANES
    n_q = FOX_HEADS * FOX_DIM
    qkv, qkv_m = matmul_ws(hn, hn_m, ev_w_in, i, 3 * n_q, jnp.bfloat16,
                           _col_scale(3 * n_q, n_q, FOX_DIM ** -0.5 * LOG2E))

    def stream(x, tab, n_pos):
        small = matmul(x, w_small, jnp.float32, tn=EVEN_SMALL // 2)
        cq, ckv, kr = mla_latent(small, gq, gkv_p, tab, n_pos)
        q = uq_rope(cq, w_uq_p, tab, n_pos)
        kv = matmul(ckv, w_ukv_p, jnp.bfloat16)
        return small[:, f0:f0 + FOX_HEADS], q, kv, kr

    fl, q, kv, kr = stream(hn, tabs, seq)
    fl_m, q_m, kv_m, kr_m = stream(hn_m, tabs_m, N_META)

    cum = forget_cumsum(fl.reshape(b, seq, FOX_HEADS).transpose(0, 2, 1), b_f)
    fl_m = jnp.pad(fl_m.T, ((0, 0), (0, LANES - N_META)))[None]
    cum_m = forget_cumsum(fl_m, b_f)[0, :, :N_META]
    mask = _pad_mask(FOX_HEADS)
    front = lambda x: jnp.pad(x, ((0, 0), (META_BLOCK - N_META, 0)))
    extra = _extra_lanes(-LOG2E * cum)
    extra_meta = _extra_lanes(front(LOG2E * (cum_m[:, -1:] - cum_m)), mask)
    extra_meta_only = _extra_lanes(front(-LOG2E * cum_m), mask)

    qkv_mp, q_mp, kv_mp, kr_mp = map(_front_pad, (qkv_m, q_m, kv_m, kr_m))
    kr_mp = kr_mp.at[:META_BLOCK - N_META, MLA_MASK_LANE].set(MASK_VALUE)
    o_f = fox_attention(qkv, qkv_mp, extra, extra_meta, b, seq)
    o_f_m = fox_attention(None, qkv_mp, None, extra_meta_only, b, seq)
    o_m = mla_attention(q, kv, kr, q_mp, kv_mp, kr_mp, b, seq)
    o_m_m = mla_attention(None, None, None, q_mp, kv_mp, kr_mp, b, seq)
    o = jnp.concatenate([o_f, o_m], axis=1)
    o_meta = jnp.concatenate([o_f_m, o_m_m], axis=1)[-N_META:]
    return matmul_ws(o, o_meta, ev_w_out, i, D_MODEL, jnp.float32, tm=512)


def _diff_mixer(hn, hn_m, b, seq, i, od_w_in, lam, subln, od_w_out, bias, lam_init):
    n_q = DIFF_HEADS * 2 * DIFF_HEAD_DIM
    extra, extra_meta, tiles = bias
    qkv, qkv_m = matmul_ws(hn, hn_m, od_w_in, i, 3 * n_q, jnp.bfloat16,
                           _col_scale(3 * n_q, n_q, DIFF_HEAD_DIM ** -0.5 * LOG2E))
    qkv_mp = _front_pad(qkv_m)
    o = diff_attention(qkv, qkv_mp, extra, extra_meta, tiles, lam, subln, lam_init, b, seq)
    o_meta = diff_attention(None, qkv_mp, None, extra_meta, tiles, lam, subln, lam_init, b, seq)
    return matmul_ws(o, o_meta[-N_META:], od_w_out, i, D_MODEL, jnp.float32, tm=512)


def _diff_bias(rel_bias, t):
    rb_t = rel_bias.astype(jnp.float32).T
    far = LOG2E * rb_t[:, REL_BUCKETS - 1]
    extra = _extra_lanes(jnp.broadcast_to(far[:, None], (DIFF_HEADS, t)))
    extra_meta = _extra_lanes(jnp.broadcast_to(far[:, None], (DIFF_HEADS, META_BLOCK)), _pad_mask(DIFF_HEADS))
    tile_meta0 = t5_tiles(rb_t, META_BLOCK, t, META_BLOCK, False)
    tile_sub, tile_diag = t5_tiles(rb_t, t, t, t, False), t5_tiles(rb_t, t, t, 0, True)
    tiles = (t5_tiles(rb_t, META_BLOCK, META_BLOCK, 0, True),
             jnp.stack([tile_meta0, jnp.zeros_like(tile_meta0)], axis=1),
             jnp.stack([jnp.zeros_like(tile_sub), tile_sub, tile_diag], axis=1))
    return extra, extra_meta, tiles


def kernel(x, meta_tokens, rel_bias, ev_w_in, ev_b_f, ev_q_norm, ev_kv_norm, ev_w_uq, ev_w_ukv, ev_w_out,
           od_w_in, od_lambda, od_subln, od_w_out, norm_g, ffn_w_gate, ffn_w_up, ffn_w_down):
    b, seq, d = x.shape
    hs = x.reshape(b * seq, d)
    hs_m = meta_tokens.astype(x.dtype)

    c, sa, sb = _rope_tables(N_META + seq)
    tabs_m = tuple(tb[:N_META] for tb in (c, sa, sb))
    tabs = tuple(tb[N_META:] for tb in (c, sa, sb))
    bias = _diff_bias(rel_bias, _tile(seq, ATTN_TILE))

    hn = rmsnorm_bf16(hs, norm_g[0, 0])
    hn_m = rmsnorm_bf16(hs_m, norm_g[0, 0])
    for layer in range(DEPTH):
        g = norm_g[layer]
        i = layer // 2
        if layer % 2 == 0:
            m, m_m = _even_mixer(hn, hn_m, b, seq, i, ev_w_in, ev_b_f[i], ev_q_norm[i], ev_kv_norm[i],
                                 ev_w_uq[i], ev_w_ukv[i], ev_w_out, tabs, tabs_m)
        else:
            lam_init = 0.8 - 0.6 * math.exp(-0.3 * layer)
            m, m_m = _diff_mixer(hn, hn_m, b, seq, i, od_w_in, od_lambda[i], od_subln[i], od_w_out,
                                 bias, lam_init)
        hs, hn = resid_norm(hs, m, g[1], g[2])
        hs_m, hn_m = resid_norm(hs_m, m_m, g[1], g[2])
        h, h_m = ffn_up(hn, hn_m, ffn_w_gate, ffn_w_up, layer)
        wd = jnp.pad(ffn_w_down[layer], ((0, D_FF_PAD - D_FF), (0, 0))).astype(jnp.bfloat16)
        f, f_m = matmul_ktiled(h, wd), matmul_ktiled(h_m, wd)
        g_next = norm_g[layer + 1, 0] if layer + 1 < DEPTH else None
        hs, hn = resid_norm(hs, f, g[3], g_next)
        if g_next is not None:
            hs_m, hn_m = resid_norm(hs_m, f_m, g[3], g_next)
    return hs.reshape(b, seq, d)
```

```python
import functools
import math

import jax
import jax.numpy as jnp
from jax import lax
from jax.experimental import pallas as pl
from jax.experimental.pallas import tpu as pltpu

D_MODEL = 4096
DEPTH = 4
N_META = 16
EPS = 1e-6
FOX_HEADS = 16
FOX_DIM = 128
MLA_HEADS = 16
MLA_Q_RANK = 896
MLA_KV_RANK = 320
MLA_NOPE_DIM = 128
MLA_ROPE_DIM = 64
MLA_V_DIM = 128
ROPE_THETA = 10000.0
DIFF_HEAD_DIM = 128
DIFF_HEADS = D_MODEL // (2 * DIFF_HEAD_DIM)
REL_BUCKETS = 32
REL_MAX_DIST = 128
D_FF = ((8 * D_MODEL + 3 * 256 - 1) // (3 * 256)) * 256

LANES = 128
BF16_ROWS = 16
META_BLOCK = LANES
MASK_VALUE = -1e30
VMEM_LIMIT_BYTES = 56 * 1024 * 1024
LOG2E = math.log2(math.e)

FF_TILE = 512
KV_RANK_PAD = 384
EVEN_SMALL = MLA_Q_RANK + KV_RANK_PAD + 2 * LANES
ATTN_TILE = 512
BIAS_SLOTS = 4
MLA_MASK_LANE = MLA_ROPE_DIM
SWEEP_PAIRS_PER_TRIP = 4


def _params(*sem):
    return pltpu.CompilerParams(dimension_semantics=sem, vmem_limit_bytes=VMEM_LIMIT_BYTES)


def _tile(n, t, unit=LANES):
    if n <= t:
        return n
    return max(c for c in range(unit, t + 1, unit) if n % c == 0)


def _rms(x, g, n):
    ms = jnp.sum(x * x, axis=-1, keepdims=True) / n
    return x * lax.rsqrt(ms + EPS) * g


def _norm_kernel(x_ref, g_ref, o_ref):
    o_ref[...] = _rms(x_ref[...], g_ref[...], x_ref.shape[-1]).astype(o_ref.dtype)


def rmsnorm_bf16(x, g, tm=256):
    m, d = x.shape
    tm = _tile(m, tm)
    return pl.pallas_call(
        _norm_kernel,
        grid=(m // tm,),
        in_specs=[pl.BlockSpec((tm, d), lambda i: (i, 0)), pl.BlockSpec((1, d), lambda i: (0, 0))],
        out_specs=pl.BlockSpec((tm, d), lambda i: (i, 0)),
        out_shape=jax.ShapeDtypeStruct((m, d), jnp.bfloat16),
        compiler_params=_params("arbitrary"),
        name="rmsnorm",
    )(x, g.reshape(1, d))


def _resid_norm_kernel(hs_ref, m_ref, g1_ref, g2_ref, hs_out_ref, hn_out_ref):
    d = hs_ref.shape[-1]
    hs = hs_ref[...] + _rms(m_ref[...], g1_ref[...], d)
    hs_out_ref[...] = hs
    hn_out_ref[...] = _rms(hs, g2_ref[...], d).astype(hn_out_ref.dtype)


def _resid_kernel(hs_ref, m_ref, g1_ref, hs_out_ref):
    hs_out_ref[...] = hs_ref[...] + _rms(m_ref[...], g1_ref[...], hs_ref.shape[-1])


def resid_norm(hs, m, g1, g2, tm=256):
    rows, d = hs.shape
    tm = _tile(rows, tm)
    row = pl.BlockSpec((tm, d), lambda i: (i, 0))
    vec = pl.BlockSpec((1, d), lambda i: (0, 0))
    if g2 is None:
        return pl.pallas_call(
            _resid_kernel, grid=(rows // tm,), in_specs=[row, row, vec], out_specs=row,
            out_shape=jax.ShapeDtypeStruct((rows, d), jnp.float32),
            compiler_params=_params("arbitrary"), name="resid",
        )(hs, m, g1.reshape(1, d)), None
    return pl.pallas_call(
        _resid_norm_kernel, grid=(rows // tm,), in_specs=[row, row, vec, vec], out_specs=[row, row],
        out_shape=[jax.ShapeDtypeStruct((rows, d), jnp.float32),
                   jax.ShapeDtypeStruct((rows, d), jnp.bfloat16)],
        compiler_params=_params("arbitrary"), name="resid_norm",
    )(hs, m, g1.reshape(1, d), g2.reshape(1, d))


def _mm_kernel(x_ref, w_ref, o_ref):
    o_ref[...] = jnp.dot(x_ref[...], w_ref[...], preferred_element_type=jnp.float32).astype(o_ref.dtype)


def matmul(x, w, out_dtype, tm=1024, tn=1024):
    m, k = x.shape
    n = w.shape[1]
    tm, tn = _tile(m, tm), _tile(n, tn)
    return pl.pallas_call(
        _mm_kernel,
        grid=(m // tm, n // tn),
        in_specs=[pl.BlockSpec((tm, k), lambda i, j: (i, 0)), pl.BlockSpec((k, tn), lambda i, j: (0, j))],
        out_specs=pl.BlockSpec((tm, tn), lambda i, j: (i, j)),
        out_shape=jax.ShapeDtypeStruct((m, n), out_dtype),
        compiler_params=_params("arbitrary", "arbitrary"),
        name="matmul",
    )(x, w)


def _mm_ws_kernel(*refs, scaled):
    if scaled:
        x_ref, xm_ref, w_ref, s_ref, o_ref, om_ref, wb_ref = refs
    else:
        x_ref, xm_ref, w_ref, o_ref, om_ref, wb_ref = refs

    @pl.when(pl.program_id(1) == 0)
    def _():
        w = w_ref[...]
        if scaled:
            w = w * s_ref[...]
        wb_ref[...] = w.astype(wb_ref.dtype)
        om_ref[...] = jnp.dot(xm_ref[...], wb_ref[...], preferred_element_type=jnp.float32).astype(om_ref.dtype)

    o_ref[...] = jnp.dot(x_ref[...], wb_ref[...], preferred_element_type=jnp.float32).astype(o_ref.dtype)


def matmul_ws(x, x_meta, w_stack, layer, n, out_dtype, scale=None, tm=1024, tn=1024):
    m, k = x.shape
    tm, tn = _tile(m, tm), _tile(n, tn)
    scaled = scale is not None
    in_specs = [pl.BlockSpec((tm, k), lambda j, i: (i, 0)),
                pl.BlockSpec(x_meta.shape, lambda j, i: (0, 0)),
                pl.BlockSpec((None, k, tn), lambda j, i: (layer, 0, j), pipeline_mode=pl.Buffered(1))]
    args = [x, x_meta, w_stack]
    if scaled:
        in_specs.append(pl.BlockSpec((1, tn), lambda j, i: (0, j)))
        args.append(scale)
    mm = x_meta.shape[0]
    return pl.pallas_call(
        functools.partial(_mm_ws_kernel, scaled=scaled),
        grid=(n // tn, m // tm),
        in_specs=in_specs,
        out_specs=[pl.BlockSpec((tm, tn), lambda j, i: (i, j)), pl.BlockSpec((mm, tn), lambda j, i: (0, j))],
        out_shape=[jax.ShapeDtypeStruct((m, n), out_dtype), jax.ShapeDtypeStruct((mm, n), out_dtype)],
        scratch_shapes=[pltpu.VMEM((k, tn), jnp.bfloat16)],
        compiler_params=_params("arbitrary", "arbitrary"),
        name="matmul_ws",
    )(*args)


def _mm_acc_kernel(x_ref, w_ref, o_ref):
    d = jnp.dot(x_ref[...], w_ref[...], preferred_element_type=jnp.float32)
    k = pl.program_id(2)

    @pl.when(k == 0)
    def _():
        o_ref[...] = d

    @pl.when(k > 0)
    def _():
        o_ref[...] += d


def matmul_ktiled(x, w, tm=512, tn=1024, tk=5504):
    m, k = x.shape
    n = w.shape[1]
    tm, tn, tk = _tile(m, tm), _tile(n, tn), _tile(k, tk)
    return pl.pallas_call(
        _mm_acc_kernel,
        grid=(m // tm, n // tn, k // tk),
        in_specs=[pl.BlockSpec((tm, tk), lambda i, j, kk: (i, kk)),
                  pl.BlockSpec((tk, tn), lambda i, j, kk: (kk, j))],
        out_specs=pl.BlockSpec((tm, tn), lambda i, j, kk: (i, j)),
        out_shape=jax.ShapeDtypeStruct((m, n), jnp.float32),
        compiler_params=_params("arbitrary", "arbitrary", "arbitrary"),
        name="matmul_ktiled",
    )(x, w)


def _ffn_up_kernel(x_ref, xm_ref, wg_ref, wu_ref, o_ref, om_ref, wgb_ref, wub_ref):
    def act(x):
        g = jnp.dot(x, wgb_ref[...], preferred_element_type=jnp.float32)
        u = jnp.dot(x, wub_ref[...], preferred_element_type=jnp.float32)
        return ((g / (1.0 + jnp.exp(-g))) * u).astype(o_ref.dtype)

    @pl.when(pl.program_id(1) == 0)
    def _():
        wgb_ref[...] = wg_ref[...].astype(wgb_ref.dtype)
        wub_ref[...] = wu_ref[...].astype(wub_ref.dtype)
        om_ref[...] = act(xm_ref[...])

    o_ref[...] = act(x_ref[...])


def ffn_up(x, x_meta, wg_stack, wu_stack, layer, tm=1024, tf=FF_TILE):
    m, k = x.shape
    d_ff = wg_stack.shape[-1]
    tm = _tile(m, tm)
    mm = x_meta.shape[0]
    wspec = pl.BlockSpec((None, k, tf), lambda j, i: (layer, 0, j), pipeline_mode=pl.Buffered(1))
    return pl.pallas_call(
        _ffn_up_kernel,
        grid=(pl.cdiv(d_ff, tf), m // tm),
        in_specs=[pl.BlockSpec((tm, k), lambda j, i: (i, 0)), pl.BlockSpec((mm, k), lambda j, i: (0, 0)),
                  wspec, wspec],
        out_specs=[pl.BlockSpec((tm, tf), lambda j, i: (i, j)), pl.BlockSpec((mm, tf), lambda j, i: (0, j))],
        out_shape=[jax.ShapeDtypeStruct((m, d_ff), jnp.bfloat16),
                   jax.ShapeDtypeStruct((mm, d_ff), jnp.bfloat16)],
        scratch_shapes=[pltpu.VMEM((k, tf), jnp.bfloat16), pltpu.VMEM((k, tf), jnp.bfloat16)],
        compiler_params=_params("arbitrary", "arbitrary"),
        name="ffn_up",
    )(x, x_meta, wg_stack, wu_stack)


def _rope128(r, c, sa, sb):
    return r * c + pltpu.roll(r, LANES - MLA_ROPE_DIM // 2, 1) * sa + pltpu.roll(r, MLA_ROPE_DIM // 2, 1) * sb


def _mla_latent_kernel(s_ref, gq_ref, gkv_ref, c_ref, sa_ref, sb_ref, cq_ref, ckv_ref, kr_ref):
    q0, kv0, r0 = 0, MLA_Q_RANK, MLA_Q_RANK + KV_RANK_PAD
    cq_ref[...] = _rms(s_ref[:, q0:kv0], gq_ref[...], MLA_Q_RANK).astype(cq_ref.dtype)
    ckv_ref[...] = _rms(s_ref[:, kv0:r0], gkv_ref[...], MLA_KV_RANK).astype(ckv_ref.dtype)
    kr_ref[...] = _rope128(s_ref[:, r0:r0 + LANES], c_ref[...], sa_ref[...], sb_ref[...]).astype(kr_ref.dtype)


def mla_latent(small, gq, gkv_pad, tabs, seq, tm=512):
    m = small.shape[0]
    tm = _tile(m, tm)
    nseq = seq // tm
    tab = pl.BlockSpec((tm, LANES), lambda i: (i % nseq, 0))
    return pl.pallas_call(
        _mla_latent_kernel,
        grid=(m // tm,),
        in_specs=[pl.BlockSpec((tm, EVEN_SMALL), lambda i: (i, 0)),
                  pl.BlockSpec((1, MLA_Q_RANK), lambda i: (0, 0)),
                  pl.BlockSpec((1, KV_RANK_PAD), lambda i: (0, 0)), tab, tab, tab],
        out_specs=[pl.BlockSpec((tm, MLA_Q_RANK), lambda i: (i, 0)),
                   pl.BlockSpec((tm, KV_RANK_PAD), lambda i: (i, 0)),
                   pl.BlockSpec((tm, LANES), lambda i: (i, 0))],
        out_shape=[jax.ShapeDtypeStruct((m, MLA_Q_RANK), jnp.bfloat16),
                   jax.ShapeDtypeStruct((m, KV_RANK_PAD), jnp.bfloat16),
                   jax.ShapeDtypeStruct((m, LANES), jnp.bfloat16)],
        compiler_params=_params("arbitrary"),
        name="mla_latent",
    )(small, gq.reshape(1, -1), gkv_pad.reshape(1, -1), *tabs)


def _uq_rope_kernel(x_ref, w_ref, c_ref, sa_ref, sb_ref, o_ref):
    q = jnp.dot(x_ref[...], w_ref[...], preferred_element_type=jnp.float32)
    c, sa, sb = c_ref[...], sa_ref[...], sb_ref[...]
    one = (lax.broadcasted_iota(jnp.int32, (1, LANES), 1) == MLA_MASK_LANE).astype(jnp.float32)
    for h in range(q.shape[1] // (2 * LANES)):
        n0, r0 = 2 * h * LANES, (2 * h + 1) * LANES
        o_ref[:, n0:r0] = q[:, n0:r0].astype(o_ref.dtype)
        o_ref[:, r0:r0 + LANES] = (_rope128(q[:, r0:r0 + LANES], c, sa, sb) + one).astype(o_ref.dtype)


def uq_rope(cq, w_uq_p, tabs, seq, tm=1024, tn=1024):
    m, k = cq.shape
    n = w_uq_p.shape[1]
    tm, tn = _tile(seq, tm), _tile(n, tn, 2 * LANES)
    nseq = seq // tm
    tab = pl.BlockSpec((tm, LANES), lambda i, j: (i % nseq, 0))
    return pl.pallas_call(
        _uq_rope_kernel, grid=(m // tm, n // tn),
        in_specs=[pl.BlockSpec((tm, k), lambda i, j: (i, 0)), pl.BlockSpec((k, tn), lambda i, j: (0, j)),
                  tab, tab, tab],
        out_specs=pl.BlockSpec((tm, tn), lambda i, j: (i, j)),
        out_shape=jax.ShapeDtypeStruct((m, n), jnp.bfloat16),
        compiler_params=_params("arbitrary", "arbitrary"), name="uq_rope",
    )(cq, w_uq_p, *tabs)


def _forget_cumsum_kernel(x_ref, b_ref, o_ref):
    z = x_ref[...] + b_ref[...]
    log_f = jnp.minimum(z, 0.0) - jnp.log(1.0 + jnp.exp(-jnp.abs(z)))
    n = log_f.shape[-1]
    row = lax.broadcasted_iota(jnp.int32, (LANES, LANES), 0)
    col = lax.broadcasted_iota(jnp.int32, (LANES, LANES), 1)
    upper = (row <= col).astype(jnp.float32)
    carry = jnp.zeros((log_f.shape[0], 1), jnp.float32)
    for c in range(n // LANES):
        chunk = jnp.dot(log_f[:, c * LANES:(c + 1) * LANES], upper,
                        precision=lax.Precision.HIGHEST, preferred_element_type=jnp.float32) + carry
        o_ref[:, c * LANES:(c + 1) * LANES] = chunk
        carry = chunk[:, LANES - 1:LANES]


def forget_cumsum(f_logit, b_f):
    b, h, n = f_logit.shape
    return pl.pallas_call(
        _forget_cumsum_kernel,
        grid=(b,),
        in_specs=[pl.BlockSpec((None, h, n), lambda i: (i, 0, 0)), pl.BlockSpec((h, 1), lambda i: (0, 0))],
        out_specs=pl.BlockSpec((None, h, n), lambda i: (i, 0, 0)),
        out_shape=jax.ShapeDtypeStruct((b, h, n), jnp.float32),
        compiler_params=_params("arbitrary"),
        name="forget_cumsum",
    )(f_logit, b_f.reshape(h, 1))


def _t5_tile_kernel(rb_ref, o_ref, *, offset, causal):
    h = pl.program_id(0)
    shape = o_ref.shape
    i = lax.broadcasted_iota(jnp.int32, shape, 0)
    j = lax.broadcasted_iota(jnp.int32, shape, 1)
    dist = jnp.maximum(j - i + offset, 0)
    max_exact = REL_BUCKETS // 2
    d = jnp.maximum(dist, 1).astype(jnp.float32)
    large = max_exact + (jnp.log(d / max_exact) / math.log(REL_MAX_DIST / max_exact)
                         * (REL_BUCKETS - max_exact)).astype(jnp.int32)
    large = jnp.minimum(large, REL_BUCKETS - 1)
    bucket = jnp.where(dist < max_exact, dist, large)
    far = rb_ref[h, REL_BUCKETS - 1]
    tile = jnp.zeros(shape, jnp.float32)
    for b in range(REL_BUCKETS):
        tile = jnp.where(bucket == b, (rb_ref[h, b] - far) * LOG2E, tile)
    if causal:
        tile = jnp.where(i > j, MASK_VALUE, tile)
    o_ref[...] = tile


def t5_tiles(rel_bias_t, rows, cols, offset, causal):
    h = rel_bias_t.shape[0]
    return pl.pallas_call(
        functools.partial(_t5_tile_kernel, offset=offset, causal=causal),
        grid=(h,),
        in_specs=[pl.BlockSpec(memory_space=pltpu.SMEM)],
        out_specs=pl.BlockSpec((None, rows, cols), lambda i: (i, 0, 0)),
        out_shape=jax.ShapeDtypeStruct((h, rows, cols), jnp.float32),
        compiler_params=_params("arbitrary"),
        name="t5_tiles",
    )(rel_bias_t)


def _q_aug(q):
    ones = (lax.broadcasted_iota(jnp.int32, (LANES, q.shape[0]), 0) < BIAS_SLOTS).astype(q.dtype)
    return jnp.concatenate([q.T, ones], axis=0)


def _v_aug(v):
    return jnp.concatenate([v.T, jnp.ones((BF16_ROWS, v.shape[0]), v.dtype)], axis=0)


def _k_aug(k, extra):
    return jnp.concatenate([k, extra], axis=1)


def _init_state(tq, dv):
    return (jnp.full((1, tq), MASK_VALUE, jnp.float32), jnp.zeros((dv + BF16_ROWS, tq), jnp.float32))


def _score(k_aug, q_aug):
    return jnp.dot(k_aug, q_aug, preferred_element_type=jnp.float32)


def _causal(s):
    i = lax.broadcasted_iota(jnp.int32, s.shape, 0)
    j = lax.broadcasted_iota(jnp.int32, s.shape, 1)
    return jnp.where(i > j, MASK_VALUE, s)


def _consume(s, v_aug, state):
    m, acc = state
    m_new = jnp.maximum(m, jnp.max(s, axis=0, keepdims=True))
    p = jnp.exp2(s - m_new).astype(v_aug.dtype)
    acc = jnp.exp2(m - m_new) * acc + jnp.dot(v_aug, p, preferred_element_type=jnp.float32)
    return m_new, acc


def _next_tile(qb, kb):
    wrap = kb == qb
    return jnp.where(wrap, qb + 1, qb), jnp.where(wrap, 0, kb + 1)


def _flat_sweep(nq, score, consume, bufs):
    sa, sb = bufs
    n_steps = nq * (nq + 1) // 2 - 1
    zero = jnp.int32(0)
    score(zero, zero, sa)

    def pair(tile):
        t1 = _next_tile(*tile)
        t2 = _next_tile(*t1)
        score(*t1, sb)
        consume(*tile, sa)
        score(*t2, sa)
        consume(*t1, sb)
        return t2

    def trip(_, tile):
        for _ in range(SWEEP_PAIRS_PER_TRIP):
            tile = pair(tile)
        return tile

    tile = lax.fori_loop(0, n_steps // (2 * SWEEP_PAIRS_PER_TRIP), trip, (zero, zero))
    rest = n_steps % (2 * SWEEP_PAIRS_PER_TRIP)
    for _ in range(rest // 2):
        tile = pair(tile)
    if rest % 2:
        t1 = _next_tile(*tile)
        score(*t1, sb)
        consume(*tile, sa)
        consume(*t1, sb)
    else:
        consume(*tile, sa)


def _finish(state, dv):
    _, acc = state
    return (acc[:dv] * (1.0 / acc[dv:dv + 1])).T


def _resume(first, m, acc):
    return jnp.where(first, MASK_VALUE, m), jnp.where(first, 0.0, acc)


def _single_meta_kernel(q_ref, km_ref, xm_ref, vm_ref, o_ref, *, rope_keys):
    q_aug = q_ref[...].T if rope_keys else _q_aug(q_ref[...])
    dv = vm_ref.shape[-1]
    s = _causal(_score(_k_aug(km_ref[...], xm_ref[...]), q_aug))
    state = _consume(s, _v_aug(vm_ref[...]), _init_state(q_aug.shape[1], dv))
    o_ref[...] = _finish(state, dv).astype(o_ref.dtype)


def _single_kernel(q_ref, kr_ref, xr_ref, vr_ref, km_ref, xm_ref, vm_ref, add_ref, o_ref,
                   kar_ref, var_ref, kam_ref, vam_ref, sa_ref, sb_ref, qa_ref, m_ref, acc_ref, *, rope_keys):
    nq, t = kr_ref.shape[0], kr_ref.shape[1]
    dv = vm_ref.shape[-1]
    kam_ref[...] = _k_aug(km_ref[...], xm_ref[...])
    vam_ref[...] = _v_aug(vm_ref[...])
    for blk in range(nq):
        kar_ref[blk] = _k_aug(kr_ref[blk], xr_ref[blk])
        var_ref[blk] = _v_aug(vr_ref[blk])
        q = q_ref[blk * t:(blk + 1) * t, :]
        qa_ref[blk] = q.T if rope_keys else _q_aug(q)

    def score(qb, kb, buf):
        buf[...] = _score(kar_ref[kb], qa_ref[qb])

    def consume(qb, kb, buf):
        s = buf[...] + add_ref[(kb == qb).astype(jnp.int32)]
        m_ref[qb], acc_ref[qb] = _consume(s, var_ref[kb], _resume(kb == 0, m_ref[qb], acc_ref[qb]))

    _flat_sweep(nq, score, consume, (sa_ref, sb_ref))
    s_meta = [_score(kam_ref[...], qa_ref[qb]) for qb in range(nq)]
    for qb in range(nq):
        state = _consume(s_meta[qb], vam_ref[...], (m_ref[qb], acc_ref[qb]))
        o_ref[qb * t:(qb + 1) * t, :] = _finish(state, dv).astype(o_ref.dtype)


def _diff_out(outs, lam_ref, g_ref, lam_init, dv):
    lam = lam_ref[...]
    lam_full = (jnp.exp(jnp.sum(lam[0:1] * lam[1:2], axis=-1, keepdims=True))
                - jnp.exp(jnp.sum(lam[2:3] * lam[3:4], axis=-1, keepdims=True)) + lam_init)
    return _rms(outs[0] - lam_full * outs[1], g_ref[...], dv) * (1.0 - lam_init)


def _diff_meta_kernel(q_ref, km_ref, xm_ref, vm_ref, tm_ref, lam_ref, g_ref, o_ref, *, lam_init):
    d = DIFF_HEAD_DIM
    dv = vm_ref.shape[-1]
    km, xm, v_aug = km_ref[...], xm_ref[...], _v_aug(vm_ref[...])
    outs = []
    for c in range(2):
        q_aug = _q_aug(q_ref[:, c * d:(c + 1) * d])
        s = _score(_k_aug(km[:, c * d:(c + 1) * d], xm), q_aug) + tm_ref[...]
        outs.append(_finish(_consume(s, v_aug, _init_state(q_aug.shape[1], dv)), dv))
    o_ref[...] = _diff_out(outs, lam_ref, g_ref, lam_init, dv).astype(o_ref.dtype)


def _diff_kernel(q_ref, kr_ref, xr_ref, vr_ref, km_ref, xm_ref, vm_ref, tm_ref, add_ref, lam_ref, g_ref, o_ref,
                 kar_ref, var_ref, kam_ref, vam_ref, sa_ref, sb_ref, qa_ref, m_ref, acc_ref, *, lam_init):
    nq, t = kr_ref.shape[0], kr_ref.shape[1]
    d = DIFF_HEAD_DIM
    dv = vm_ref.shape[-1]
    km, xm, xr = km_ref[...], xm_ref[...], xr_ref[...]
    vam_ref[...] = _v_aug(vm_ref[...])
    for c in range(2):
        kam_ref[c] = _k_aug(km[:, c * d:(c + 1) * d], xm)
    for blk in range(nq):
        var_ref[blk] = _v_aug(vr_ref[blk])
        for c in range(2):
            kar_ref[c, blk] = _k_aug(kr_ref[blk, :, c * d:(c + 1) * d], xr)
            qa_ref[c, blk] = _q_aug(q_ref[blk * t:(blk + 1) * t, c * d:(c + 1) * d])

    def score(qb, kb, buf):
        for c in range(2):
            buf[c] = _score(kar_ref[c, kb], qa_ref[c, qb])

    def consume(qb, kb, buf):
        add = add_ref[jnp.clip(kb - qb + 2, 0, 2)]
        for c in range(2):
            m_ref[c, qb], acc_ref[c, qb] = _consume(buf[c] + add, var_ref[kb],
                                                    _resume(kb == 0, m_ref[c, qb], acc_ref[c, qb]))

    _flat_sweep(nq, score, consume, (sa_ref, sb_ref))
    for qb in range(nq):
        outs = []
        for c in range(2):
            s = _score(kam_ref[c], qa_ref[c, qb]) + tm_ref[min(qb, 1)]
            outs.append(_finish(_consume(s, vam_ref[...], (m_ref[c, qb], acc_ref[c, qb])), dv))
        o_ref[qb * t:(qb + 1) * t, :] = _diff_out(outs, lam_ref, g_ref, lam_init, dv).astype(o_ref.dtype)


def _attn_call(kernel, name, b, heads, nq, tq, dv, in_specs, args, scratch=()):
    return pl.pallas_call(
        kernel,
        grid=(b, heads, nq),
        in_specs=in_specs,
        out_specs=pl.BlockSpec((tq, dv), lambda bi, h, qi: (bi * nq + qi, h)),
        out_shape=jax.ShapeDtypeStruct((b * nq * tq, heads * dv), jnp.bfloat16),
        scratch_shapes=list(scratch),
        compiler_params=_params("arbitrary", "arbitrary", "arbitrary"),
        name=name,
    )(*args)


def _aug_scratch(n_comp, nkb, t, dv):
    lead = (n_comp,) if n_comp > 1 else ()
    bf = jnp.bfloat16
    return [pltpu.VMEM(lead + (nkb, t, 2 * LANES), bf), pltpu.VMEM((nkb, dv + BF16_ROWS, t), bf),
            pltpu.VMEM(lead + (META_BLOCK, 2 * LANES), bf), pltpu.VMEM((dv + BF16_ROWS, META_BLOCK), bf),
            pltpu.VMEM(lead + (t, t), jnp.float32), pltpu.VMEM(lead + (t, t), jnp.float32),
            pltpu.VMEM(lead + (nkb, 2 * LANES, t), bf), pltpu.VMEM(lead + (nkb, 1, t), jnp.float32),
            pltpu.VMEM(lead + (nkb, dv + BF16_ROWS, t), jnp.float32)]


def _causal_tiles(t):
    i = lax.broadcasted_iota(jnp.int32, (t, t), 0)
    j = lax.broadcasted_iota(jnp.int32, (t, t), 1)
    return jnp.stack([jnp.zeros((t, t), jnp.float32), jnp.where(i > j, MASK_VALUE, 0.0).astype(jnp.float32)])


def _whole(shape):
    return pl.BlockSpec(shape, lambda bi, h, qi: (0,) * len(shape))


def _qspec(tq, dq, nq, col=lambda h: h):
    return pl.BlockSpec((tq, dq), lambda bi, h, qi: (bi * nq + qi, col(h)))


def _kspec(nkb, tk, d, col):
    return pl.BlockSpec((None, nkb, tk, d), lambda bi, h, qi: (bi, 0, 0, col(h)))


def _mspec(d, col):
    return pl.BlockSpec((META_BLOCK, d), lambda bi, h, qi: (0, col(h)))


def fox_attention(qkv, qkv_meta, extra, extra_meta, b, seq, t=ATTN_TILE):
    hd = FOX_HEADS
    xm_spec = pl.BlockSpec((None, META_BLOCK, LANES), lambda bi, h, qi: (h, 0, 0))
    meta_specs = [_mspec(FOX_DIM, lambda h: hd + h), xm_spec, _mspec(FOX_DIM, lambda h: 2 * hd + h)]
    meta_args = (qkv_meta, extra_meta, qkv_meta)
    if qkv is None:
        return _attn_call(functools.partial(_single_meta_kernel, rope_keys=False),
                          "fox_attention_meta", 1, hd, 1, META_BLOCK, FOX_DIM,
                          [_qspec(META_BLOCK, FOX_DIM, 1)] + meta_specs, (qkv_meta,) + meta_args)
    t = _tile(seq, t)
    nq = seq // t
    kv4 = qkv.reshape(b, nq, t, qkv.shape[-1])
    return _attn_call(
        functools.partial(_single_kernel, rope_keys=False), "fox_attention",
        b, hd, 1, seq, FOX_DIM,
        [_qspec(seq, FOX_DIM, 1), _kspec(nq, t, FOX_DIM, lambda h: hd + h),
         pl.BlockSpec((None, None, nq, t, LANES), lambda bi, h, qi: (bi, h, 0, 0, 0)),
         _kspec(nq, t, FOX_DIM, lambda h: 2 * hd + h)] + meta_specs + [_whole((2, t, t))],
        (qkv, kv4, extra.reshape(b, hd, nq, t, LANES), kv4) + meta_args + (_causal_tiles(t),),
        _aug_scratch(1, nq, t, FOX_DIM))


def mla_attention(q, kv, kr, q_meta, kv_meta, kr_meta, b, seq, t=ATTN_TILE):
    hd = MLA_HEADS
    meta_specs = [_mspec(LANES, lambda h: 2 * h), _mspec(LANES, lambda h: 0), _mspec(LANES, lambda h: 2 * h + 1)]
    meta_args = (kv_meta, kr_meta, kv_meta)
    if q is None:
        return _attn_call(functools.partial(_single_meta_kernel, rope_keys=True),
                          "mla_attention_meta", 1, hd, 1, META_BLOCK, MLA_V_DIM,
                          [_qspec(META_BLOCK, 2 * LANES, 1)] + meta_specs, (q_meta,) + meta_args)
    t = _tile(seq, t)
    nq = seq // t
    kv4 = kv.reshape(b, nq, t, kv.shape[-1])
    kr4 = kr.reshape(b, nq, t, kr.shape[-1])
    return _attn_call(
        functools.partial(_single_kernel, rope_keys=True), "mla_attention",
        b, hd, 1, seq, MLA_V_DIM,
        [_qspec(seq, 2 * LANES, 1), _kspec(nq, t, LANES, lambda h: 2 * h),
         _kspec(nq, t, LANES, lambda h: 0), _kspec(nq, t, LANES, lambda h: 2 * h + 1)] + meta_specs
        + [_whole((2, t, t))],
        (q, kv4, kr4, kv4) + meta_args + (_causal_tiles(t),), _aug_scratch(1, nq, t, MLA_V_DIM))


def diff_attention(qkv, qkv_meta, extra, extra_meta, tiles, lam, subln, lam_init, b, seq, t=ATTN_TILE):
    hd = DIFF_HEADS
    dv = 2 * DIFF_HEAD_DIM
    tile_meta_only, tile_meta, tile_real = tiles
    lam_spec = pl.BlockSpec((4, DIFF_HEAD_DIM), lambda bi, h, qi: (0, 0))
    g_spec = pl.BlockSpec((1, dv), lambda bi, h, qi: (0, 0))
    kcol, vcol = (lambda h: hd + h), (lambda h: 2 * hd + h)
    per_head = lambda r, c: pl.BlockSpec((None, r, c), lambda bi, h, qi: (h, 0, 0))
    meta_specs = [_mspec(dv, kcol), per_head(META_BLOCK, LANES), _mspec(dv, vcol)]
    meta_args = (qkv_meta, extra_meta, qkv_meta)
    tail = (lam, subln.reshape(1, dv))
    if qkv is None:
        return _attn_call(
            functools.partial(_diff_meta_kernel, lam_init=lam_init),
            "diff_attention_meta", 1, hd, 1, META_BLOCK, dv,
            [_qspec(META_BLOCK, dv, 1)] + meta_specs + [per_head(META_BLOCK, META_BLOCK), lam_spec, g_spec],
            (qkv_meta,) + meta_args + (tile_meta_only,) + tail)
    t = _tile(seq, t)
    nq = seq // t
    kv4 = qkv.reshape(b, nq, t, qkv.shape[-1])
    return _attn_call(
        functools.partial(_diff_kernel, lam_init=lam_init),
        "diff_attention", b, hd, 1, seq, dv,
        [_qspec(seq, dv, 1), _kspec(nq, t, dv, kcol), per_head(t, LANES), _kspec(nq, t, dv, vcol)] + meta_specs
        + [pl.BlockSpec((None, 2, META_BLOCK, t), lambda bi, h, qi: (h, 0, 0, 0)),
           pl.BlockSpec((None, 3, t, t), lambda bi, h, qi: (h, 0, 0, 0)), lam_spec, g_spec],
        (qkv, kv4, extra, kv4) + meta_args + (tile_meta, tile_real) + tail,
        _aug_scratch(2, nq, t, dv))


def _front_pad(x):
    return jnp.pad(x, ((META_BLOCK - x.shape[0], 0), (0, 0)))


def _split3(x):
    def head(v):
        bits = lax.bitcast_convert_type(v, jnp.uint32) & jnp.uint32(0xFFFF0000)
        return lax.bitcast_convert_type(bits, jnp.float32)

    hi = head(x)
    mid = head(x - hi)
    lo = head(x - hi - mid)
    return tuple(p.astype(jnp.bfloat16) for p in (hi, mid, lo))


def _extra_lanes(x, mask=None):
    pieces = list(_split3(x)) + [jnp.zeros(x.shape, jnp.bfloat16) if mask is None else mask.astype(jnp.bfloat16)]
    out = jnp.stack(pieces, axis=-1)
    return jnp.pad(out, [(0, 0)] * x.ndim + [(0, LANES - BIAS_SLOTS)])


def _pad_mask(heads):
    m = jnp.where(jnp.arange(META_BLOCK) < META_BLOCK - N_META, MASK_VALUE, 0.0).astype(jnp.float32)
    return jnp.broadcast_to(m, (heads, META_BLOCK))


def _rope_tables(n_pos):
    inv = ROPE_THETA ** (-jnp.arange(0, MLA_ROPE_DIM, 2, dtype=jnp.float32) / MLA_ROPE_DIM)
    ang = jnp.arange(n_pos, dtype=jnp.float32)[:, None] * inv[None, :]
    cos, sin = jnp.cos(ang), jnp.sin(ang)
    z32 = jnp.zeros_like(cos)
    z64 = jnp.zeros((n_pos, LANES - MLA_ROPE_DIM), jnp.float32)
    c = jnp.concatenate([cos, cos, z64], axis=1)
    sa = jnp.concatenate([-sin, z32, z64], axis=1)
    sb = jnp.concatenate([z32, sin, z64], axis=1)
    return c, sa, sb


def _col_scale(n, n_scaled, value):
    return jnp.where(jnp.arange(n) < n_scaled, value, 1.0).astype(jnp.float32)[None, :]


def _even_weights(w_in, w_uq, w_ukv, gkv):
    o = [0]
    for s in (FOX_HEADS * FOX_DIM,) * 3 + (FOX_HEADS, MLA_Q_RANK, MLA_KV_RANK, MLA_ROPE_DIM):
        o.append(o[-1] + s)
    bf = jnp.bfloat16
    mla_scale = (MLA_NOPE_DIM + MLA_ROPE_DIM) ** -0.5 * LOG2E
    pad = lambda w, n: jnp.pad(w, ((0, 0), (0, n - w.shape[1])))
    w_small = jnp.concatenate([w_in[:, o[4]:o[5]], pad(w_in[:, o[5]:o[6]], KV_RANK_PAD),
                               pad(w_in[:, o[6]:o[7]], LANES), pad(w_in[:, o[3]:o[4]], LANES)],
                              axis=1).astype(bf)
    hq = (w_uq * mla_scale).reshape(MLA_Q_RANK, MLA_HEADS, MLA_NOPE_DIM + MLA_ROPE_DIM)
    hq = jnp.pad(hq, ((0, 0), (0, 0), (0, 2 * LANES - hq.shape[-1])))
    w_uq_p = hq.reshape(MLA_Q_RANK, MLA_HEADS * 2 * LANES).astype(bf)
    w_ukv_p = jnp.pad(w_ukv, ((0, KV_RANK_PAD - MLA_KV_RANK), (0, 0))).astype(bf)
    gkv_p = jnp.pad(gkv, (0, KV_RANK_PAD - MLA_KV_RANK))
    return w_small, w_uq_p, w_ukv_p, gkv_p


def _even_mixer(hn, hn_m, b, seq, i, ev_w_in, b_f, gq, gkv, w_uq, w_ukv, ev_w_out, tabs, tabs_m):
    w_small, w_uq_p, w_ukv_p, gkv_p = _even_weights(ev_w_in[i], w_uq, w_ukv, gkv)
    f0 = MLA_Q_RANK + KV_RANK_PAD + LANES
    n_q = FOX_HEADS * FOX_DIM
    qkv, qkv_m = matmul_ws(hn, hn_m, ev_w_in, i, 3 * n_q, jnp.bfloat16,
                           _col_scale(3 * n_q, n_q, FOX_DIM ** -0.5 * LOG2E))

    def stream(x, tab, n_pos):
        small = matmul(x, w_small, jnp.float32, tn=EVEN_SMALL // 2)
        cq, ckv, kr = mla_latent(small, gq, gkv_p, tab, n_pos)
        q = uq_rope(cq, w_uq_p, tab, n_pos)
        kv = matmul(ckv, w_ukv_p, jnp.bfloat16)
        return small[:, f0:f0 + FOX_HEADS], q, kv, kr

    fl, q, kv, kr = stream(hn, tabs, seq)
    fl_m, q_m, kv_m, kr_m = stream(hn_m, tabs_m, N_META)

    cum = forget_cumsum(fl.reshape(b, seq, FOX_HEADS).transpose(0, 2, 1), b_f)
    fl_m = jnp.pad(fl_m.T, ((0, 0), (0, LANES - N_META)))[None]
    cum_m = forget_cumsum(fl_m, b_f)[0, :, :N_META]
    mask = _pad_mask(FOX_HEADS)
    front = lambda x: jnp.pad(x, ((0, 0), (META_BLOCK - N_META, 0)))
    extra = _extra_lanes(-LOG2E * cum)
    extra_meta = _extra_lanes(front(LOG2E * (cum_m[:, -1:] - cum_m)), mask)
    extra_meta_only = _extra_lanes(front(-LOG2E * cum_m), mask)

    qkv_mp, q_mp, kv_mp, kr_mp = map(_front_pad, (qkv_m, q_m, kv_m, kr_m))
    kr_mp = kr_mp.at[:META_BLOCK - N_META, MLA_MASK_LANE].set(MASK_VALUE)
    o_f = fox_attention(qkv, qkv_mp, extra, extra_meta, b, seq)
    o_f_m = fox_attention(None, qkv_mp, None, extra_meta_only, b, seq)
    o_m = mla_attention(q, kv, kr, q_mp, kv_mp, kr_mp, b, seq)
    o_m_m = mla_attention(None, None, None, q_mp, kv_mp, kr_mp, b, seq)
    o = jnp.concatenate([o_f, o_m], axis=1)
    o_meta = jnp.concatenate([o_f_m, o_m_m], axis=1)[-N_META:]
    return matmul_ws(o, o_meta, ev_w_out, i, D_MODEL, jnp.float32, tm=512)


def _diff_mixer(hn, hn_m, b, seq, i, od_w_in, lam, subln, od_w_out, bias, lam_init):
    n_q = DIFF_HEADS * 2 * DIFF_HEAD_DIM
    extra, extra_meta, tiles = bias
    qkv, qkv_m = matmul_ws(hn, hn_m, od_w_in, i, 3 * n_q, jnp.bfloat16,
                           _col_scale(3 * n_q, n_q, DIFF_HEAD_DIM ** -0.5 * LOG2E))
    qkv_mp = _front_pad(qkv_m)
    o = diff_attention(qkv, qkv_mp, extra, extra_meta, tiles, lam, subln, lam_init, b, seq)
    o_meta = diff_attention(None, qkv_mp, None, extra_meta, tiles, lam, subln, lam_init, b, seq)
    return matmul_ws(o, o_meta[-N_META:], od_w_out, i, D_MODEL, jnp.float32, tm=512)


def _diff_bias(rel_bias, t):
    rb_t = rel_bias.astype(jnp.float32).T
    far = LOG2E * rb_t[:, REL_BUCKETS - 1]
    extra = _extra_lanes(jnp.broadcast_to(far[:, None], (DIFF_HEADS, t)))
    extra_meta = _extra_lanes(jnp.broadcast_to(far[:, None], (DIFF_HEADS, META_BLOCK)), _pad_mask(DIFF_HEADS))
    tile_meta0 = t5_tiles(rb_t, META_BLOCK, t, META_BLOCK, False)
    tile_sub, tile_diag = t5_tiles(rb_t, t, t, t, False), t5_tiles(rb_t, t, t, 0, True)
    tiles = (t5_tiles(rb_t, META_BLOCK, META_BLOCK, 0, True),
             jnp.stack([tile_meta0, jnp.zeros_like(tile_meta0)], axis=1),
             jnp.stack([jnp.zeros_like(tile_sub), tile_sub, tile_diag], axis=1))
    return extra, extra_meta, tiles


def kernel(x, meta_tokens, rel_bias, ev_w_in, ev_b_f, ev_q_norm, ev_kv_norm, ev_w_uq, ev_w_ukv, ev_w_out,
           od_w_in, od_lambda, od_subln, od_w_out, norm_g, ffn_w_gate, ffn_w_up, ffn_w_down):
    b, seq, d = x.shape
    hs = x.reshape(b * seq, d)
    hs_m = meta_tokens.astype(x.dtype)

    c, sa, sb = _rope_tables(N_META + seq)
    tabs_m = tuple(tb[:N_META] for tb in (c, sa, sb))
    tabs = tuple(tb[N_META:] for tb in (c, sa, sb))
    bias = _diff_bias(rel_bias, _tile(seq, ATTN_TILE))

    hn = rmsnorm_bf16(hs, norm_g[0, 0])
    hn_m = rmsnorm_bf16(hs_m, norm_g[0, 0])
    for layer in range(DEPTH):
        g = norm_g[layer]
        i = layer // 2
        if layer % 2 == 0:
            m, m_m = _even_mixer(hn, hn_m, b, seq, i, ev_w_in, ev_b_f[i], ev_q_norm[i], ev_kv_norm[i],
                                 ev_w_uq[i], ev_w_ukv[i], ev_w_out, tabs, tabs_m)
        else:
            lam_init = 0.8 - 0.6 * math.exp(-0.3 * layer)
            m, m_m = _diff_mixer(hn, hn_m, b, seq, i, od_w_in, od_lambda[i], od_subln[i], od_w_out,
                                 bias, lam_init)
        hs, hn = resid_norm(hs, m, g[1], g[2])
        hs_m, hn_m = resid_norm(hs_m, m_m, g[1], g[2])
        h, h_m = ffn_up(hn, hn_m, ffn_w_gate, ffn_w_up, layer)
        wd = ffn_w_down[layer].astype(jnp.bfloat16)
        f, f_m = matmul_ktiled(h, wd), matmul_ktiled(h_m, wd)
        g_next = norm_g[layer + 1, 0] if layer + 1 < DEPTH else None
        hs, hn = resid_norm(hs, f, g[3], g_next)
        if g_next is not None:
            hs_m, hn_m = resid_norm(hs_m, f_m, g[3], g_next)
    return hs.reshape(b, seq, d)
```

```python
import functools
import math

import jax
import jax.numpy as jnp
from jax import lax
from jax.experimental import pallas as pl
from jax.experimental.pallas import tpu as pltpu

D_MODEL = 4096
DEPTH = 4
N_META = 16
EPS = 1e-6
FOX_HEADS = 16
FOX_DIM = 128
MLA_HEADS = 16
MLA_Q_RANK = 896
MLA_KV_RANK = 320
MLA_NOPE_DIM = 128
MLA_ROPE_DIM = 64
MLA_V_DIM = 128
ROPE_THETA = 10000.0
DIFF_HEAD_DIM = 128
DIFF_HEADS = D_MODEL // (2 * DIFF_HEAD_DIM)
REL_BUCKETS = 32
REL_MAX_DIST = 128
D_FF = ((8 * D_MODEL + 3 * 256 - 1) // (3 * 256)) * 256

LANES = 128
BF16_ROWS = 16
META_BLOCK = LANES
MASK_VALUE = -1e30
VMEM_LIMIT_BYTES = 56 * 1024 * 1024
LOG2E = math.log2(math.e)

FF_TILE = 512
KV_RANK_PAD = 384
EVEN_SMALL = MLA_Q_RANK + KV_RANK_PAD + 2 * LANES
ATTN_TILE = 512
BIAS_SLOTS = 4
MLA_MASK_LANE = MLA_ROPE_DIM
SWEEP_PAIRS_PER_TRIP = 4


def _params(*sem):
    return pltpu.CompilerParams(dimension_semantics=sem, vmem_limit_bytes=VMEM_LIMIT_BYTES)


def _tile(n, t, unit=LANES):
    if n <= t:
        return n
    return max(c for c in range(unit, t + 1, unit) if n % c == 0)


def _rms(x, g, n):
    ms = jnp.sum(x * x, axis=-1, keepdims=True) / n
    return x * lax.rsqrt(ms + EPS) * g


def _norm_kernel(x_ref, g_ref, o_ref):
    o_ref[...] = _rms(x_ref[...], g_ref[...], x_ref.shape[-1]).astype(o_ref.dtype)


def rmsnorm_bf16(x, g, tm=256):
    m, d = x.shape
    tm = _tile(m, tm)
    return pl.pallas_call(
        _norm_kernel,
        grid=(m // tm,),
        in_specs=[pl.BlockSpec((tm, d), lambda i: (i, 0)), pl.BlockSpec((1, d), lambda i: (0, 0))],
        out_specs=pl.BlockSpec((tm, d), lambda i: (i, 0)),
        out_shape=jax.ShapeDtypeStruct((m, d), jnp.bfloat16),
        compiler_params=_params("arbitrary"),
        name="rmsnorm",
    )(x, g.reshape(1, d))


def _resid_norm_kernel(hs_ref, m_ref, g1_ref, g2_ref, hs_out_ref, hn_out_ref):
    d = hs_ref.shape[-1]
    hs = hs_ref[...] + _rms(m_ref[...], g1_ref[...], d)
    hs_out_ref[...] = hs
    hn_out_ref[...] = _rms(hs, g2_ref[...], d).astype(hn_out_ref.dtype)


def _resid_kernel(hs_ref, m_ref, g1_ref, hs_out_ref):
    hs_out_ref[...] = hs_ref[...] + _rms(m_ref[...], g1_ref[...], hs_ref.shape[-1])


def resid_norm(hs, m, g1, g2, tm=256):
    rows, d = hs.shape
    tm = _tile(rows, tm)
    row = pl.BlockSpec((tm, d), lambda i: (i, 0))
    vec = pl.BlockSpec((1, d), lambda i: (0, 0))
    if g2 is None:
        return pl.pallas_call(
            _resid_kernel, grid=(rows // tm,), in_specs=[row, row, vec], out_specs=row,
            out_shape=jax.ShapeDtypeStruct((rows, d), jnp.float32),
            compiler_params=_params("arbitrary"), name="resid",
        )(hs, m, g1.reshape(1, d)), None
    return pl.pallas_call(
        _resid_norm_kernel, grid=(rows // tm,), in_specs=[row, row, vec, vec], out_specs=[row, row],
        out_shape=[jax.ShapeDtypeStruct((rows, d), jnp.float32),
                   jax.ShapeDtypeStruct((rows, d), jnp.bfloat16)],
        compiler_params=_params("arbitrary"), name="resid_norm",
    )(hs, m, g1.reshape(1, d), g2.reshape(1, d))


def _mm_kernel(x_ref, w_ref, o_ref):
    o_ref[...] = jnp.dot(x_ref[...], w_ref[...], preferred_element_type=jnp.float32).astype(o_ref.dtype)


def matmul(x, w, out_dtype, tm=1024, tn=1024):
    m, k = x.shape
    n = w.shape[1]
    tm, tn = _tile(m, tm), _tile(n, tn)
    return pl.pallas_call(
        _mm_kernel,
        grid=(m // tm, n // tn),
        in_specs=[pl.BlockSpec((tm, k), lambda i, j: (i, 0)), pl.BlockSpec((k, tn), lambda i, j: (0, j))],
        out_specs=pl.BlockSpec((tm, tn), lambda i, j: (i, j)),
        out_shape=jax.ShapeDtypeStruct((m, n), out_dtype),
        compiler_params=_params("arbitrary", "arbitrary"),
        name="matmul",
    )(x, w)


def _rolling(cast, compute):
    jj = pl.program_id(0)

    @pl.when(jj == 0)
    def _():
        cast(0)

    for slot in range(2):
        @pl.when(jnp.logical_and(jj > 0, jj % 2 == slot))
        def _():
            compute(1 - slot)
            cast(slot)


def _chunk_rows(w_ref):
    ck = w_ref.shape[0]
    return pl.ds(pl.multiple_of(pl.program_id(1) * ck, ck), ck)


def _mm_ws_kernel(*refs, scaled):
    if scaled:
        x_ref, xm_ref, w_ref, s_ref, o_ref, om_ref, wa_ref, wb_ref = refs
    else:
        x_ref, xm_ref, w_ref, o_ref, om_ref, wa_ref, wb_ref = refs
    bufs = (wa_ref, wb_ref)

    def cast(slot):
        w = w_ref[...]
        if scaled:
            w = w * s_ref[...]
        bufs[slot][_chunk_rows(w_ref), :] = w.astype(jnp.bfloat16)

    def compute(slot):
        @pl.when(pl.program_id(1) == 0)
        def _():
            om_ref[...] = jnp.dot(xm_ref[...], bufs[slot][...],
                                  preferred_element_type=jnp.float32).astype(om_ref.dtype)

        o_ref[...] = jnp.dot(x_ref[...], bufs[slot][...], preferred_element_type=jnp.float32).astype(o_ref.dtype)

    _rolling(cast, compute)


def _ws_geometry(m, k, tm):
    ni = m // tm
    assert k % ni == 0 and (k // ni) % BF16_ROWS == 0, (m, k, tm)
    return ni, k // ni


def matmul_ws(x, x_meta, w_stack, layer, n, out_dtype, scale=None, tm=1024, tn=1024):
    m, k = x.shape
    tm, tn = _tile(m, tm), _tile(n, tn)
    ni, ck = _ws_geometry(m, k, tm)
    nj = n // tn
    scaled = scale is not None
    nxt = lambda jj: jnp.minimum(jj, nj - 1)
    cur = lambda jj: jnp.maximum(jj - 1, 0)
    in_specs = [pl.BlockSpec((tm, k), lambda jj, i: (i, 0)),
                pl.BlockSpec(x_meta.shape, lambda jj, i: (0, 0)),
                pl.BlockSpec((None, ck, tn), lambda jj, i: (layer, i, nxt(jj)))]
    args = [x, x_meta, w_stack]
    if scaled:
        in_specs.append(pl.BlockSpec((1, tn), lambda jj, i: (0, nxt(jj))))
        args.append(scale)
    mm = x_meta.shape[0]
    return pl.pallas_call(
        functools.partial(_mm_ws_kernel, scaled=scaled),
        grid=(nj + 1, ni),
        in_specs=in_specs,
        out_specs=[pl.BlockSpec((tm, tn), lambda jj, i: (jnp.where(jj == 0, 0, i), cur(jj))),
                   pl.BlockSpec((mm, tn), lambda jj, i: (0, cur(jj)))],
        out_shape=[jax.ShapeDtypeStruct((m, n), out_dtype), jax.ShapeDtypeStruct((mm, n), out_dtype)],
        scratch_shapes=[pltpu.VMEM((k, tn), jnp.bfloat16)] * 2,
        compiler_params=_params("arbitrary", "arbitrary"),
        name="matmul_ws",
    )(*args)


def _mm_acc_kernel(x_ref, w_ref, o_ref):
    d = jnp.dot(x_ref[...], w_ref[...], preferred_element_type=jnp.float32)
    k = pl.program_id(2)

    @pl.when(k == 0)
    def _():
        o_ref[...] = d

    @pl.when(k > 0)
    def _():
        o_ref[...] += d


def matmul_ktiled(x, w, tm=512, tn=1024, tk=5504):
    m, k = x.shape
    n = w.shape[1]
    tm, tn, tk = _tile(m, tm), _tile(n, tn), _tile(k, tk)
    return pl.pallas_call(
        _mm_acc_kernel,
        grid=(m // tm, n // tn, k // tk),
        in_specs=[pl.BlockSpec((tm, tk), lambda i, j, kk: (i, kk)),
                  pl.BlockSpec((tk, tn), lambda i, j, kk: (kk, j))],
        out_specs=pl.BlockSpec((tm, tn), lambda i, j, kk: (i, j)),
        out_shape=jax.ShapeDtypeStruct((m, n), jnp.float32),
        compiler_params=_params("arbitrary", "arbitrary", "arbitrary"),
        name="matmul_ktiled",
    )(x, w)


def _ffn_up_kernel(x_ref, xm_ref, wg_ref, wu_ref, o_ref, om_ref, ga_ref, ua_ref, gb_ref, ub_ref):
    bufs = ((ga_ref, ua_ref), (gb_ref, ub_ref))

    def cast(slot):
        rows = _chunk_rows(wg_ref)
        bufs[slot][0][rows, :] = wg_ref[...].astype(jnp.bfloat16)
        bufs[slot][1][rows, :] = wu_ref[...].astype(jnp.bfloat16)

    def compute(slot):
        wg_ref_b, wu_ref_b = bufs[slot]

        def act(x):
            g = jnp.dot(x, wg_ref_b[...], preferred_element_type=jnp.float32)
            u = jnp.dot(x, wu_ref_b[...], preferred_element_type=jnp.float32)
            return ((g / (1.0 + jnp.exp(-g))) * u).astype(o_ref.dtype)

        @pl.when(pl.program_id(1) == 0)
        def _():
            om_ref[...] = act(xm_ref[...])

        o_ref[...] = act(x_ref[...])

    _rolling(cast, compute)


def ffn_up(x, x_meta, wg_stack, wu_stack, layer, tm=1024, tf=FF_TILE):
    m, k = x.shape
    d_ff = wg_stack.shape[-1]
    tm = _tile(m, tm)
    ni, ck = _ws_geometry(m, k, tm)
    nj = pl.cdiv(d_ff, tf)
    mm = x_meta.shape[0]
    nxt = lambda jj: jnp.minimum(jj, nj - 1)
    cur = lambda jj: jnp.maximum(jj - 1, 0)
    wspec = pl.BlockSpec((None, ck, tf), lambda jj, i: (layer, i, nxt(jj)))
    return pl.pallas_call(
        _ffn_up_kernel,
        grid=(nj + 1, ni),
        in_specs=[pl.BlockSpec((tm, k), lambda jj, i: (i, 0)), pl.BlockSpec((mm, k), lambda jj, i: (0, 0)),
                  wspec, wspec],
        out_specs=[pl.BlockSpec((tm, tf), lambda jj, i: (jnp.where(jj == 0, 0, i), cur(jj))),
                   pl.BlockSpec((mm, tf), lambda jj, i: (0, cur(jj)))],
        out_shape=[jax.ShapeDtypeStruct((m, d_ff), jnp.bfloat16),
                   jax.ShapeDtypeStruct((mm, d_ff), jnp.bfloat16)],
        scratch_shapes=[pltpu.VMEM((k, tf), jnp.bfloat16)] * 4,
        compiler_params=_params("arbitrary", "arbitrary"),
        name="ffn_up",
    )(x, x_meta, wg_stack, wu_stack)


def _rope128(r, c, sa, sb):
    return r * c + pltpu.roll(r, LANES - MLA_ROPE_DIM // 2, 1) * sa + pltpu.roll(r, MLA_ROPE_DIM // 2, 1) * sb


def _mla_latent_kernel(s_ref, gq_ref, gkv_ref, c_ref, sa_ref, sb_ref, cq_ref, ckv_ref, kr_ref):
    q0, kv0, r0 = 0, MLA_Q_RANK, MLA_Q_RANK + KV_RANK_PAD
    cq_ref[...] = _rms(s_ref[:, q0:kv0], gq_ref[...], MLA_Q_RANK).astype(cq_ref.dtype)
    ckv_ref[...] = _rms(s_ref[:, kv0:r0], gkv_ref[...], MLA_KV_RANK).astype(ckv_ref.dtype)
    kr_ref[...] = _rope128(s_ref[:, r0:r0 + LANES], c_ref[...], sa_ref[...], sb_ref[...]).astype(kr_ref.dtype)


def mla_latent(small, gq, gkv_pad, tabs, seq, tm=512):
    m = small.shape[0]
    tm = _tile(m, tm)
    nseq = seq // tm
    tab = pl.BlockSpec((tm, LANES), lambda i: (i % nseq, 0))
    return pl.pallas_call(
        _mla_latent_kernel,
        grid=(m // tm,),
        in_specs=[pl.BlockSpec((tm, EVEN_SMALL), lambda i: (i, 0)),
                  pl.BlockSpec((1, MLA_Q_RANK), lambda i: (0, 0)),
                  pl.BlockSpec((1, KV_RANK_PAD), lambda i: (0, 0)), tab, tab, tab],
        out_specs=[pl.BlockSpec((tm, MLA_Q_RANK), lambda i: (i, 0)),
                   pl.BlockSpec((tm, KV_RANK_PAD), lambda i: (i, 0)),
                   pl.BlockSpec((tm, LANES), lambda i: (i, 0))],
        out_shape=[jax.ShapeDtypeStruct((m, MLA_Q_RANK), jnp.bfloat16),
                   jax.ShapeDtypeStruct((m, KV_RANK_PAD), jnp.bfloat16),
                   jax.ShapeDtypeStruct((m, LANES), jnp.bfloat16)],
        compiler_params=_params("arbitrary"),
        name="mla_latent",
    )(small, gq.reshape(1, -1), gkv_pad.reshape(1, -1), *tabs)


def _uq_rope_kernel(x_ref, w_ref, c_ref, sa_ref, sb_ref, o_ref):
    q = jnp.dot(x_ref[...], w_ref[...], preferred_element_type=jnp.float32)
    c, sa, sb = c_ref[...], sa_ref[...], sb_ref[...]
    one = (lax.broadcasted_iota(jnp.int32, (1, LANES), 1) == MLA_MASK_LANE).astype(jnp.float32)
    for h in range(q.shape[1] // (2 * LANES)):
        n0, r0 = 2 * h * LANES, (2 * h + 1) * LANES
        o_ref[:, n0:r0] = q[:, n0:r0].astype(o_ref.dtype)
        o_ref[:, r0:r0 + LANES] = (_rope128(q[:, r0:r0 + LANES], c, sa, sb) + one).astype(o_ref.dtype)


def uq_rope(cq, w_uq_p, tabs, seq, tm=1024, tn=1024):
    m, k = cq.shape
    n = w_uq_p.shape[1]
    tm, tn = _tile(seq, tm), _tile(n, tn, 2 * LANES)
    nseq = seq // tm
    tab = pl.BlockSpec((tm, LANES), lambda i, j: (i % nseq, 0))
    return pl.pallas_call(
        _uq_rope_kernel, grid=(m // tm, n // tn),
        in_specs=[pl.BlockSpec((tm, k), lambda i, j: (i, 0)), pl.BlockSpec((k, tn), lambda i, j: (0, j)),
                  tab, tab, tab],
        out_specs=pl.BlockSpec((tm, tn), lambda i, j: (i, j)),
        out_shape=jax.ShapeDtypeStruct((m, n), jnp.bfloat16),
        compiler_params=_params("arbitrary", "arbitrary"), name="uq_rope",
    )(cq, w_uq_p, *tabs)


def _forget_cumsum_kernel(x_ref, b_ref, o_ref):
    z = x_ref[...] + b_ref[...]
    log_f = jnp.minimum(z, 0.0) - jnp.log(1.0 + jnp.exp(-jnp.abs(z)))
    n = log_f.shape[-1]
    row = lax.broadcasted_iota(jnp.int32, (LANES, LANES), 0)
    col = lax.broadcasted_iota(jnp.int32, (LANES, LANES), 1)
    upper = (row <= col).astype(jnp.float32)
    carry = jnp.zeros((log_f.shape[0], 1), jnp.float32)
    for c in range(n // LANES):
        chunk = jnp.dot(log_f[:, c * LANES:(c + 1) * LANES], upper,
                        precision=lax.Precision.HIGHEST, preferred_element_type=jnp.float32) + carry
        o_ref[:, c * LANES:(c + 1) * LANES] = chunk
        carry = chunk[:, LANES - 1:LANES]


def forget_cumsum(f_logit, b_f):
    b, h, n = f_logit.shape
    return pl.pallas_call(
        _forget_cumsum_kernel,
        grid=(b,),
        in_specs=[pl.BlockSpec((None, h, n), lambda i: (i, 0, 0)), pl.BlockSpec((h, 1), lambda i: (0, 0))],
        out_specs=pl.BlockSpec((None, h, n), lambda i: (i, 0, 0)),
        out_shape=jax.ShapeDtypeStruct((b, h, n), jnp.float32),
        compiler_params=_params("arbitrary"),
        name="forget_cumsum",
    )(f_logit, b_f.reshape(h, 1))


def _t5_tile_kernel(rb_ref, o_ref, *, offset, causal):
    h = pl.program_id(0)
    shape = o_ref.shape
    i = lax.broadcasted_iota(jnp.int32, shape, 0)
    j = lax.broadcasted_iota(jnp.int32, shape, 1)
    dist = jnp.maximum(j - i + offset, 0)
    max_exact = REL_BUCKETS // 2
    d = jnp.maximum(dist, 1).astype(jnp.float32)
    large = max_exact + (jnp.log(d / max_exact) / math.log(REL_MAX_DIST / max_exact)
                         * (REL_BUCKETS - max_exact)).astype(jnp.int32)
    large = jnp.minimum(large, REL_BUCKETS - 1)
    bucket = jnp.where(dist < max_exact, dist, large)
    far = rb_ref[h, REL_BUCKETS - 1]
    tile = jnp.zeros(shape, jnp.float32)
    for b in range(REL_BUCKETS):
        tile = jnp.where(bucket == b, (rb_ref[h, b] - far) * LOG2E, tile)
    if causal:
        tile = jnp.where(i > j, MASK_VALUE, tile)
    o_ref[...] = tile


def t5_tiles(rel_bias_t, rows, cols, offset, causal):
    h = rel_bias_t.shape[0]
    return pl.pallas_call(
        functools.partial(_t5_tile_kernel, offset=offset, causal=causal),
        grid=(h,),
        in_specs=[pl.BlockSpec(memory_space=pltpu.SMEM)],
        out_specs=pl.BlockSpec((None, rows, cols), lambda i: (i, 0, 0)),
        out_shape=jax.ShapeDtypeStruct((h, rows, cols), jnp.float32),
        compiler_params=_params("arbitrary"),
        name="t5_tiles",
    )(rel_bias_t)


def _q_aug(q):
    ones = (lax.broadcasted_iota(jnp.int32, (LANES, q.shape[0]), 0) < BIAS_SLOTS).astype(q.dtype)
    return jnp.concatenate([q.T, ones], axis=0)


def _v_aug(v):
    return jnp.concatenate([v.T, jnp.ones((BF16_ROWS, v.shape[0]), v.dtype)], axis=0)


def _k_aug(k, extra):
    return jnp.concatenate([k, extra], axis=1)


def _init_state(tq, dv):
    return (jnp.full((1, tq), MASK_VALUE, jnp.float32), jnp.zeros((dv + BF16_ROWS, tq), jnp.float32))


def _score(k_aug, q_aug):
    return jnp.dot(k_aug, q_aug, preferred_element_type=jnp.float32)


def _causal(s):
    i = lax.broadcasted_iota(jnp.int32, s.shape, 0)
    j = lax.broadcasted_iota(jnp.int32, s.shape, 1)
    return jnp.where(i > j, MASK_VALUE, s)


def _consume(s, v_aug, state):
    m, acc = state
    m_new = jnp.maximum(m, jnp.max(s, axis=0, keepdims=True))
    p = jnp.exp2(s - m_new).astype(v_aug.dtype)
    acc = jnp.exp2(m - m_new) * acc + jnp.dot(v_aug, p, preferred_element_type=jnp.float32)
    return m_new, acc


def _next_tile(qb, kb):
    wrap = kb == qb
    return jnp.where(wrap, qb + 1, qb), jnp.where(wrap, 0, kb + 1)


def _flat_sweep(nq, score, consume, bufs):
    sa, sb = bufs
    n_steps = nq * (nq + 1) // 2 - 1
    zero = jnp.int32(0)
    score(zero, zero, sa)

    def pair(tile):
        t1 = _next_tile(*tile)
        t2 = _next_tile(*t1)
        score(*t1, sb)
        consume(*tile, sa)
        score(*t2, sa)
        consume(*t1, sb)
        return t2

    def trip(_, tile):
        for _ in range(SWEEP_PAIRS_PER_TRIP):
            tile = pair(tile)
        return tile

    tile = lax.fori_loop(0, n_steps // (2 * SWEEP_PAIRS_PER_TRIP), trip, (zero, zero))
    rest = n_steps % (2 * SWEEP_PAIRS_PER_TRIP)
    for _ in range(rest // 2):
        tile = pair(tile)
    if rest % 2:
        t1 = _next_tile(*tile)
        score(*t1, sb)
        consume(*tile, sa)
        consume(*t1, sb)
    else:
        consume(*tile, sa)


def _finish(state, dv):
    _, acc = state
    return (acc[:dv] * (1.0 / acc[dv:dv + 1])).T


def _resume(first, m, acc):
    return jnp.where(first, MASK_VALUE, m), jnp.where(first, 0.0, acc)


def _single_meta_kernel(q_ref, km_ref, xm_ref, vm_ref, o_ref, *, rope_keys):
    q_aug = q_ref[...].T if rope_keys else _q_aug(q_ref[...])
    dv = vm_ref.shape[-1]
    s = _causal(_score(_k_aug(km_ref[...], xm_ref[...]), q_aug))
    state = _consume(s, _v_aug(vm_ref[...]), _init_state(q_aug.shape[1], dv))
    o_ref[...] = _finish(state, dv).astype(o_ref.dtype)


def _single_kernel(q_ref, kr_ref, xr_ref, vr_ref, km_ref, xm_ref, vm_ref, add_ref, o_ref,
                   kar_ref, var_ref, kam_ref, vam_ref, sa_ref, sb_ref, qa_ref, m_ref, acc_ref, *, rope_keys):
    nq, t = kr_ref.shape[0], kr_ref.shape[1]
    dv = vm_ref.shape[-1]
    kam_ref[...] = _k_aug(km_ref[...], xm_ref[...])
    vam_ref[...] = _v_aug(vm_ref[...])
    for blk in range(nq):
        kar_ref[blk] = _k_aug(kr_ref[blk], xr_ref[blk])
        var_ref[blk] = _v_aug(vr_ref[blk])
        q = q_ref[blk * t:(blk + 1) * t, :]
        qa_ref[blk] = q.T if rope_keys else _q_aug(q)

    def score(qb, kb, buf):
        buf[...] = _score(kar_ref[kb], qa_ref[qb])

    def consume(qb, kb, buf):
        s = buf[...] + add_ref[(kb == qb).astype(jnp.int32)]
        m_ref[qb], acc_ref[qb] = _consume(s, var_ref[kb], _resume(kb == 0, m_ref[qb], acc_ref[qb]))

    _flat_sweep(nq, score, consume, (sa_ref, sb_ref))
    s_meta = [_score(kam_ref[...], qa_ref[qb]) for qb in range(nq)]
    for qb in range(nq):
        state = _consume(s_meta[qb], vam_ref[...], (m_ref[qb], acc_ref[qb]))
        o_ref[qb * t:(qb + 1) * t, :] = _finish(state, dv).astype(o_ref.dtype)


def _diff_out(outs, lam_ref, g_ref, lam_init, dv):
    lam = lam_ref[...]
    lam_full = (jnp.exp(jnp.sum(lam[0:1] * lam[1:2], axis=-1, keepdims=True))
                - jnp.exp(jnp.sum(lam[2:3] * lam[3:4], axis=-1, keepdims=True)) + lam_init)
    return _rms(outs[0] - lam_full * outs[1], g_ref[...], dv) * (1.0 - lam_init)


def _diff_meta_kernel(q_ref, km_ref, xm_ref, vm_ref, tm_ref, lam_ref, g_ref, o_ref, *, lam_init):
    d = DIFF_HEAD_DIM
    dv = vm_ref.shape[-1]
    km, xm, v_aug = km_ref[...], xm_ref[...], _v_aug(vm_ref[...])
    outs = []
    for c in range(2):
        q_aug = _q_aug(q_ref[:, c * d:(c + 1) * d])
        s = _score(_k_aug(km[:, c * d:(c + 1) * d], xm), q_aug) + tm_ref[...]
        outs.append(_finish(_consume(s, v_aug, _init_state(q_aug.shape[1], dv)), dv))
    o_ref[...] = _diff_out(outs, lam_ref, g_ref, lam_init, dv).astype(o_ref.dtype)


def _diff_kernel(q_ref, kr_ref, xr_ref, vr_ref, km_ref, xm_ref, vm_ref, tm_ref, add_ref, lam_ref, g_ref, o_ref,
                 kar_ref, var_ref, kam_ref, vam_ref, sa_ref, sb_ref, qa_ref, m_ref, acc_ref, *, lam_init):
    nq, t = kr_ref.shape[0], kr_ref.shape[1]
    d = DIFF_HEAD_DIM
    dv = vm_ref.shape[-1]
    km, xm, xr = km_ref[...], xm_ref[...], xr_ref[...]
    vam_ref[...] = _v_aug(vm_ref[...])
    for c in range(2):
        kam_ref[c] = _k_aug(km[:, c * d:(c + 1) * d], xm)
    for blk in range(nq):
        var_ref[blk] = _v_aug(vr_ref[blk])
        for c in range(2):
            kar_ref[c, blk] = _k_aug(kr_ref[blk, :, c * d:(c + 1) * d], xr)
            qa_ref[c, blk] = _q_aug(q_ref[blk * t:(blk + 1) * t, c * d:(c + 1) * d])

    def score(qb, kb, buf):
        for c in range(2):
            buf[c] = _score(kar_ref[c, kb], qa_ref[c, qb])

    def consume(qb, kb, buf):
        add = add_ref[jnp.clip(kb - qb + 2, 0, 2)]
        for c in range(2):
            m_ref[c, qb], acc_ref[c, qb] = _consume(buf[c] + add, var_ref[kb],
                                                    _resume(kb == 0, m_ref[c, qb], acc_ref[c, qb]))

    _flat_sweep(nq, score, consume, (sa_ref, sb_ref))
    for qb in range(nq):
        outs = []
        for c in range(2):
            s = _score(kam_ref[c], qa_ref[c, qb]) + tm_ref[min(qb, 1)]
            outs.append(_finish(_consume(s, vam_ref[...], (m_ref[c, qb], acc_ref[c, qb])), dv))
        o_ref[qb * t:(qb + 1) * t, :] = _diff_out(outs, lam_ref, g_ref, lam_init, dv).astype(o_ref.dtype)


def _attn_call(kernel, name, b, heads, nq, tq, dv, in_specs, args, scratch=()):
    return pl.pallas_call(
        kernel,
        grid=(b, heads, nq),
        in_specs=in_specs,
        out_specs=pl.BlockSpec((tq, dv), lambda bi, h, qi: (bi * nq + qi, h)),
        out_shape=jax.ShapeDtypeStruct((b * nq * tq, heads * dv), jnp.bfloat16),
        scratch_shapes=list(scratch),
        compiler_params=_params("arbitrary", "arbitrary", "arbitrary"),
        name=name,
    )(*args)


def _aug_scratch(n_comp, nkb, t, dv):
    lead = (n_comp,) if n_comp > 1 else ()
    bf = jnp.bfloat16
    return [pltpu.VMEM(lead + (nkb, t, 2 * LANES), bf), pltpu.VMEM((nkb, dv + BF16_ROWS, t), bf),
            pltpu.VMEM(lead + (META_BLOCK, 2 * LANES), bf), pltpu.VMEM((dv + BF16_ROWS, META_BLOCK), bf),
            pltpu.VMEM(lead + (t, t), jnp.float32), pltpu.VMEM(lead + (t, t), jnp.float32),
            pltpu.VMEM(lead + (nkb, 2 * LANES, t), bf), pltpu.VMEM(lead + (nkb, 1, t), jnp.float32),
            pltpu.VMEM(lead + (nkb, dv + BF16_ROWS, t), jnp.float32)]


def _causal_tiles(t):
    i = lax.broadcasted_iota(jnp.int32, (t, t), 0)
    j = lax.broadcasted_iota(jnp.int32, (t, t), 1)
    return jnp.stack([jnp.zeros((t, t), jnp.float32), jnp.where(i > j, MASK_VALUE, 0.0).astype(jnp.float32)])


def _whole(shape):
    return pl.BlockSpec(shape, lambda bi, h, qi: (0,) * len(shape))


def _qspec(tq, dq, nq, col=lambda h: h):
    return pl.BlockSpec((tq, dq), lambda bi, h, qi: (bi * nq + qi, col(h)))


def _kspec(nkb, tk, d, col):
    return pl.BlockSpec((None, nkb, tk, d), lambda bi, h, qi: (bi, 0, 0, col(h)))


def _mspec(d, col):
    return pl.BlockSpec((META_BLOCK, d), lambda bi, h, qi: (0, col(h)))


def fox_attention(qkv, qkv_meta, extra, extra_meta, b, seq, t=ATTN_TILE):
    hd = FOX_HEADS
    xm_spec = pl.BlockSpec((None, META_BLOCK, LANES), lambda bi, h, qi: (h, 0, 0))
    meta_specs = [_mspec(FOX_DIM, lambda h: hd + h), xm_spec, _mspec(FOX_DIM, lambda h: 2 * hd + h)]
    meta_args = (qkv_meta, extra_meta, qkv_meta)
    if qkv is None:
        return _attn_call(functools.partial(_single_meta_kernel, rope_keys=False),
                          "fox_attention_meta", 1, hd, 1, META_BLOCK, FOX_DIM,
                          [_qspec(META_BLOCK, FOX_DIM, 1)] + meta_specs, (qkv_meta,) + meta_args)
    t = _tile(seq, t)
    nq = seq // t
    kv4 = qkv.reshape(b, nq, t, qkv.shape[-1])
    return _attn_call(
        functools.partial(_single_kernel, rope_keys=False), "fox_attention",
        b, hd, 1, seq, FOX_DIM,
        [_qspec(seq, FOX_DIM, 1), _kspec(nq, t, FOX_DIM, lambda h: hd + h),
         pl.BlockSpec((None, None, nq, t, LANES), lambda bi, h, qi: (bi, h, 0, 0, 0)),
         _kspec(nq, t, FOX_DIM, lambda h: 2 * hd + h)] + meta_specs + [_whole((2, t, t))],
        (qkv, kv4, extra.reshape(b, hd, nq, t, LANES), kv4) + meta_args + (_causal_tiles(t),),
        _aug_scratch(1, nq, t, FOX_DIM))


def mla_attention(q, kv, kr, q_meta, kv_meta, kr_meta, b, seq, t=ATTN_TILE):
    hd = MLA_HEADS
    meta_specs = [_mspec(LANES, lambda h: 2 * h), _mspec(LANES, lambda h: 0), _mspec(LANES, lambda h: 2 * h + 1)]
    meta_args = (kv_meta, kr_meta, kv_meta)
    if q is None:
        return _attn_call(functools.partial(_single_meta_kernel, rope_keys=True),
                          "mla_attention_meta", 1, hd, 1, META_BLOCK, MLA_V_DIM,
                          [_qspec(META_BLOCK, 2 * LANES, 1)] + meta_specs, (q_meta,) + meta_args)
    t = _tile(seq, t)
    nq = seq // t
    kv4 = kv.reshape(b, nq, t, kv.shape[-1])
    kr4 = kr.reshape(b, nq, t, kr.shape[-1])
    return _attn_call(
        functools.partial(_single_kernel, rope_keys=True), "mla_attention",
        b, hd, 1, seq, MLA_V_DIM,
        [_qspec(seq, 2 * LANES, 1), _kspec(nq, t, LANES, lambda h: 2 * h),
         _kspec(nq, t, LANES, lambda h: 0), _kspec(nq, t, LANES, lambda h: 2 * h + 1)] + meta_specs
        + [_whole((2, t, t))],
        (q, kv4, kr4, kv4) + meta_args + (_causal_tiles(t),), _aug_scratch(1, nq, t, MLA_V_DIM))


def diff_attention(qkv, qkv_meta, extra, extra_meta, tiles, lam, subln, lam_init, b, seq, t=ATTN_TILE):
    hd = DIFF_HEADS
    dv = 2 * DIFF_HEAD_DIM
    tile_meta_only, tile_meta, tile_real = tiles
    lam_spec = pl.BlockSpec((4, DIFF_HEAD_DIM), lambda bi, h, qi: (0, 0))
    g_spec = pl.BlockSpec((1, dv), lambda bi, h, qi: (0, 0))
    kcol, vcol = (lambda h: hd + h), (lambda h: 2 * hd + h)
    per_head = lambda r, c: pl.BlockSpec((None, r, c), lambda bi, h, qi: (h, 0, 0))
    meta_specs = [_mspec(dv, kcol), per_head(META_BLOCK, LANES), _mspec(dv, vcol)]
    meta_args = (qkv_meta, extra_meta, qkv_meta)
    tail = (lam, subln.reshape(1, dv))
    if qkv is None:
        return _attn_call(
            functools.partial(_diff_meta_kernel, lam_init=lam_init),
            "diff_attention_meta", 1, hd, 1, META_BLOCK, dv,
            [_qspec(META_BLOCK, dv, 1)] + meta_specs + [per_head(META_BLOCK, META_BLOCK), lam_spec, g_spec],
            (qkv_meta,) + meta_args + (tile_meta_only,) + tail)
    t = _tile(seq, t)
    nq = seq // t
    kv4 = qkv.reshape(b, nq, t, qkv.shape[-1])
    return _attn_call(
        functools.partial(_diff_kernel, lam_init=lam_init),
        "diff_attention", b, hd, 1, seq, dv,
        [_qspec(seq, dv, 1), _kspec(nq, t, dv, kcol), per_head(t, LANES), _kspec(nq, t, dv, vcol)] + meta_specs
        + [pl.BlockSpec((None, 2, META_BLOCK, t), lambda bi, h, qi: (h, 0, 0, 0)),
           pl.BlockSpec((None, 3, t, t), lambda bi, h, qi: (h, 0, 0, 0)), lam_spec, g_spec],
        (qkv, kv4, extra, kv4) + meta_args + (tile_meta, tile_real) + tail,
        _aug_scratch(2, nq, t, dv))


def _front_pad(x):
    return jnp.pad(x, ((META_BLOCK - x.shape[0], 0), (0, 0)))


def _split3(x):
    def head(v):
        bits = lax.bitcast_convert_type(v, jnp.uint32) & jnp.uint32(0xFFFF0000)
        return lax.bitcast_convert_type(bits, jnp.float32)

    hi = head(x)
    mid = head(x - hi)
    lo = head(x - hi - mid)
    return tuple(p.astype(jnp.bfloat16) for p in (hi, mid, lo))


def _extra_lanes(x, mask=None):
    pieces = list(_split3(x)) + [jnp.zeros(x.shape, jnp.bfloat16) if mask is None else mask.astype(jnp.bfloat16)]
    out = jnp.stack(pieces, axis=-1)
    return jnp.pad(out, [(0, 0)] * x.ndim + [(0, LANES - BIAS_SLOTS)])


def _pad_mask(heads):
    m = jnp.where(jnp.arange(META_BLOCK) < META_BLOCK - N_META, MASK_VALUE, 0.0).astype(jnp.float32)
    return jnp.broadcast_to(m, (heads, META_BLOCK))


def _rope_tables(n_pos):
    inv = ROPE_THETA ** (-jnp.arange(0, MLA_ROPE_DIM, 2, dtype=jnp.float32) / MLA_ROPE_DIM)
    ang = jnp.arange(n_pos, dtype=jnp.float32)[:, None] * inv[None, :]
    cos, sin = jnp.cos(ang), jnp.sin(ang)
    z32 = jnp.zeros_like(cos)
    z64 = jnp.zeros((n_pos, LANES - MLA_ROPE_DIM), jnp.float32)
    c = jnp.concatenate([cos, cos, z64], axis=1)
    sa = jnp.concatenate([-sin, z32, z64], axis=1)
    sb = jnp.concatenate([z32, sin, z64], axis=1)
    return c, sa, sb


def _col_scale(n, n_scaled, value):
    return jnp.where(jnp.arange(n) < n_scaled, value, 1.0).astype(jnp.float32)[None, :]


def _even_weights(w_in, w_uq, w_ukv, gkv):
    o = [0]
    for s in (FOX_HEADS * FOX_DIM,) * 3 + (FOX_HEADS, MLA_Q_RANK, MLA_KV_RANK, MLA_ROPE_DIM):
        o.append(o[-1] + s)
    bf = jnp.bfloat16
    mla_scale = (MLA_NOPE_DIM + MLA_ROPE_DIM) ** -0.5 * LOG2E
    pad = lambda w, n: jnp.pad(w, ((0, 0), (0, n - w.shape[1])))
    w_small = jnp.concatenate([w_in[:, o[4]:o[5]], pad(w_in[:, o[5]:o[6]], KV_RANK_PAD),
                               pad(w_in[:, o[6]:o[7]], LANES), pad(w_in[:, o[3]:o[4]], LANES)],
                              axis=1).astype(bf)
    hq = (w_uq * mla_scale).reshape(MLA_Q_RANK, MLA_HEADS, MLA_NOPE_DIM + MLA_ROPE_DIM)
    hq = jnp.pad(hq, ((0, 0), (0, 0), (0, 2 * LANES - hq.shape[-1])))
    w_uq_p = hq.reshape(MLA_Q_RANK, MLA_HEADS * 2 * LANES).astype(bf)
    w_ukv_p = jnp.pad(w_ukv, ((0, KV_RANK_PAD - MLA_KV_RANK), (0, 0))).astype(bf)
    gkv_p = jnp.pad(gkv, (0, KV_RANK_PAD - MLA_KV_RANK))
    return w_small, w_uq_p, w_ukv_p, gkv_p


def _even_mixer(hn, hn_m, b, seq, i, ev_w_in, b_f, gq, gkv, w_uq, w_ukv, ev_w_out, tabs, tabs_m):
    w_small, w_uq_p, w_ukv_p, gkv_p = _even_weights(ev_w_in[i], w_uq, w_ukv, gkv)
    f0 = MLA_Q_RANK + KV_RANK_PAD + LANES
    n_q = FOX_HEADS * FOX_DIM
    qkv, qkv_m = matmul_ws(hn, hn_m, ev_w_in, i, 3 * n_q, jnp.bfloat16,
                           _col_scale(3 * n_q, n_q, FOX_DIM ** -0.5 * LOG2E))

    def stream(x, tab, n_pos):
        small = matmul(x, w_small, jnp.float32, tn=EVEN_SMALL // 2)
        cq, ckv, kr = mla_latent(small, gq, gkv_p, tab, n_pos)
        q = uq_rope(cq, w_uq_p, tab, n_pos)
        kv = matmul(ckv, w_ukv_p, jnp.bfloat16)
        return small[:, f0:f0 + FOX_HEADS], q, kv, kr

    fl, q, kv, kr = stream(hn, tabs, seq)
    fl_m, q_m, kv_m, kr_m = stream(hn_m, tabs_m, N_META)

    cum = forget_cumsum(fl.reshape(b, seq, FOX_HEADS).transpose(0, 2, 1), b_f)
    fl_m = jnp.pad(fl_m.T, ((0, 0), (0, LANES - N_META)))[None]
    cum_m = forget_cumsum(fl_m, b_f)[0, :, :N_META]
    mask = _pad_mask(FOX_HEADS)
    front = lambda x: jnp.pad(x, ((0, 0), (META_BLOCK - N_META, 0)))
    extra = _extra_lanes(-LOG2E * cum)
    extra_meta = _extra_lanes(front(LOG2E * (cum_m[:, -1:] - cum_m)), mask)
    extra_meta_only = _extra_lanes(front(-LOG2E * cum_m), mask)

    qkv_mp, q_mp, kv_mp, kr_mp = map(_front_pad, (qkv_m, q_m, kv_m, kr_m))
    kr_mp = kr_mp.at[:META_BLOCK - N_META, MLA_MASK_LANE].set(MASK_VALUE)
    o_f = fox_attention(qkv, qkv_mp, extra, extra_meta, b, seq)
    o_f_m = fox_attention(None, qkv_mp, None, extra_meta_only, b, seq)
    o_m = mla_attention(q, kv, kr, q_mp, kv_mp, kr_mp, b, seq)
    o_m_m = mla_attention(None, None, None, q_mp, kv_mp, kr_mp, b, seq)
    o = jnp.concatenate([o_f, o_m], axis=1)
    o_meta = jnp.concatenate([o_f_m, o_m_m], axis=1)[-N_META:]
    return matmul_ws(o, o_meta, ev_w_out, i, D_MODEL, jnp.float32)


def _diff_mixer(hn, hn_m, b, seq, i, od_w_in, lam, subln, od_w_out, bias, lam_init):
    n_q = DIFF_HEADS * 2 * DIFF_HEAD_DIM
    extra, extra_meta, tiles = bias
    qkv, qkv_m = matmul_ws(hn, hn_m, od_w_in, i, 3 * n_q, jnp.bfloat16,
                           _col_scale(3 * n_q, n_q, DIFF_HEAD_DIM ** -0.5 * LOG2E))
    qkv_mp = _front_pad(qkv_m)
    o = diff_attention(qkv, qkv_mp, extra, extra_meta, tiles, lam, subln, lam_init, b, seq)
    o_meta = diff_attention(None, qkv_mp, None, extra_meta, tiles, lam, subln, lam_init, b, seq)
    return matmul_ws(o, o_meta[-N_META:], od_w_out, i, D_MODEL, jnp.float32)


def _diff_bias(rel_bias, t):
    rb_t = rel_bias.astype(jnp.float32).T
    far = LOG2E * rb_t[:, REL_BUCKETS - 1]
    extra = _extra_lanes(jnp.broadcast_to(far[:, None], (DIFF_HEADS, t)))
    extra_meta = _extra_lanes(jnp.broadcast_to(far[:, None], (DIFF_HEADS, META_BLOCK)), _pad_mask(DIFF_HEADS))
    tile_meta0 = t5_tiles(rb_t, META_BLOCK, t, META_BLOCK, False)
    tile_sub, tile_diag = t5_tiles(rb_t, t, t, t, False), t5_tiles(rb_t, t, t, 0, True)
    tiles = (t5_tiles(rb_t, META_BLOCK, META_BLOCK, 0, True),
             jnp.stack([tile_meta0, jnp.zeros_like(tile_meta0)], axis=1),
             jnp.stack([jnp.zeros_like(tile_sub), tile_sub, tile_diag], axis=1))
    return extra, extra_meta, tiles


def kernel(x, meta_tokens, rel_bias, ev_w_in, ev_b_f, ev_q_norm, ev_kv_norm, ev_w_uq, ev_w_ukv, ev_w_out,
           od_w_in, od_lambda, od_subln, od_w_out, norm_g, ffn_w_gate, ffn_w_up, ffn_w_down):
    b, seq, d = x.shape
    hs = x.reshape(b * seq, d)
    hs_m = meta_tokens.astype(x.dtype)

    c, sa, sb = _rope_tables(N_META + seq)
    tabs_m = tuple(tb[:N_META] for tb in (c, sa, sb))
    tabs = tuple(tb[N_META:] for tb in (c, sa, sb))
    bias = _diff_bias(rel_bias, _tile(seq, ATTN_TILE))

    hn = rmsnorm_bf16(hs, norm_g[0, 0])
    hn_m = rmsnorm_bf16(hs_m, norm_g[0, 0])
    for layer in range(DEPTH):
        g = norm_g[layer]
        i = layer // 2
        if layer % 2 == 0:
            m, m_m = _even_mixer(hn, hn_m, b, seq, i, ev_w_in, ev_b_f[i], ev_q_norm[i], ev_kv_norm[i],
                                 ev_w_uq[i], ev_w_ukv[i], ev_w_out, tabs, tabs_m)
        else:
            lam_init = 0.8 - 0.6 * math.exp(-0.3 * layer)
            m, m_m = _diff_mixer(hn, hn_m, b, seq, i, od_w_in, od_lambda[i], od_subln[i], od_w_out,
                                 bias, lam_init)
        hs, hn = resid_norm(hs, m, g[1], g[2])
        hs_m, hn_m = resid_norm(hs_m, m_m, g[1], g[2])
        h, h_m = ffn_up(hn, hn_m, ffn_w_gate, ffn_w_up, layer)
        wd = ffn_w_down[layer].astype(jnp.bfloat16)
        f, f_m = matmul_ktiled(h, wd), matmul_ktiled(h_m, wd)
        g_next = norm_g[layer + 1, 0] if layer + 1 < DEPTH else None
        hs, hn = resid_norm(hs, f, g[3], g_next)
        if g_next is not None:
            hs_m, hn_m = resid_norm(hs_m, f_m, g[3], g_next)
    return hs.reshape(b, seq, d)
```

```python
import functools
import math

import jax
import jax.numpy as jnp
from jax import lax
from jax.experimental import pallas as pl
from jax.experimental.pallas import tpu as pltpu

D_MODEL = 4096
DEPTH = 4
N_META = 16
EPS = 1e-6
FOX_HEADS = 16
FOX_DIM = 128
MLA_HEADS = 16
MLA_Q_RANK = 896
MLA_KV_RANK = 320
MLA_NOPE_DIM = 128
MLA_ROPE_DIM = 64
MLA_V_DIM = 128
ROPE_THETA = 10000.0
DIFF_HEAD_DIM = 128
DIFF_HEADS = D_MODEL // (2 * DIFF_HEAD_DIM)
REL_BUCKETS = 32
REL_MAX_DIST = 128
D_FF = ((8 * D_MODEL + 3 * 256 - 1) // (3 * 256)) * 256

LANES = 128
BF16_ROWS = 16
META_BLOCK = LANES
MASK_VALUE = -1e30
VMEM_LIMIT_BYTES = 56 * 1024 * 1024
LOG2E = math.log2(math.e)

FF_TILE = 512
KV_RANK_PAD = 384
EVEN_SMALL = MLA_Q_RANK + KV_RANK_PAD + 2 * LANES
ATTN_TILE = 512
BIAS_SLOTS = 4
MLA_MASK_LANE = MLA_ROPE_DIM
SWEEP_PAIRS_PER_TRIP = 6


def _params(*sem):
    return pltpu.CompilerParams(dimension_semantics=sem, vmem_limit_bytes=VMEM_LIMIT_BYTES)


def _tile(n, t, unit=LANES):
    if n <= t:
        return n
    return max(c for c in range(unit, t + 1, unit) if n % c == 0)


def _rms(x, g, n):
    ms = jnp.sum(x * x, axis=-1, keepdims=True) / n
    return x * lax.rsqrt(ms + EPS) * g


def _norm_kernel(x_ref, g_ref, o_ref):
    o_ref[...] = _rms(x_ref[...], g_ref[...], x_ref.shape[-1]).astype(o_ref.dtype)


def rmsnorm_bf16(x, g, tm=256):
    m, d = x.shape
    tm = _tile(m, tm)
    return pl.pallas_call(
        _norm_kernel,
        grid=(m // tm,),
        in_specs=[pl.BlockSpec((tm, d), lambda i: (i, 0)), pl.BlockSpec((1, d), lambda i: (0, 0))],
        out_specs=pl.BlockSpec((tm, d), lambda i: (i, 0)),
        out_shape=jax.ShapeDtypeStruct((m, d), jnp.bfloat16),
        compiler_params=_params("arbitrary"),
        name="rmsnorm",
    )(x, g.reshape(1, d))


def _resid_norm_kernel(hs_ref, m_ref, g1_ref, g2_ref, hs_out_ref, hn_out_ref):
    d = hs_ref.shape[-1]
    hs = hs_ref[...] + _rms(m_ref[...], g1_ref[...], d)
    hs_out_ref[...] = hs
    hn_out_ref[...] = _rms(hs, g2_ref[...], d).astype(hn_out_ref.dtype)


def _resid_kernel(hs_ref, m_ref, g1_ref, hs_out_ref):
    hs_out_ref[...] = hs_ref[...] + _rms(m_ref[...], g1_ref[...], hs_ref.shape[-1])


def resid_norm(hs, m, g1, g2, tm=256):
    rows, d = hs.shape
    tm = _tile(rows, tm)
    row = pl.BlockSpec((tm, d), lambda i: (i, 0))
    vec = pl.BlockSpec((1, d), lambda i: (0, 0))
    if g2 is None:
        return pl.pallas_call(
            _resid_kernel, grid=(rows // tm,), in_specs=[row, row, vec], out_specs=row,
            out_shape=jax.ShapeDtypeStruct((rows, d), jnp.float32),
            compiler_params=_params("arbitrary"), name="resid",
        )(hs, m, g1.reshape(1, d)), None
    return pl.pallas_call(
        _resid_norm_kernel, grid=(rows // tm,), in_specs=[row, row, vec, vec], out_specs=[row, row],
        out_shape=[jax.ShapeDtypeStruct((rows, d), jnp.float32),
                   jax.ShapeDtypeStruct((rows, d), jnp.bfloat16)],
        compiler_params=_params("arbitrary"), name="resid_norm",
    )(hs, m, g1.reshape(1, d), g2.reshape(1, d))


def _mm_kernel(x_ref, w_ref, o_ref):
    o_ref[...] = jnp.dot(x_ref[...], w_ref[...], preferred_element_type=jnp.float32).astype(o_ref.dtype)


def matmul(x, w, out_dtype, tm=1024, tn=1024):
    m, k = x.shape
    n = w.shape[1]
    tm, tn = _tile(m, tm), _tile(n, tn)
    return pl.pallas_call(
        _mm_kernel,
        grid=(m // tm, n // tn),
        in_specs=[pl.BlockSpec((tm, k), lambda i, j: (i, 0)), pl.BlockSpec((k, tn), lambda i, j: (0, j))],
        out_specs=pl.BlockSpec((tm, tn), lambda i, j: (i, j)),
        out_shape=jax.ShapeDtypeStruct((m, n), out_dtype),
        compiler_params=_params("arbitrary", "arbitrary"),
        name="matmul",
    )(x, w)


def _rolling(cast, compute):
    jj = pl.program_id(0)

    @pl.when(jj == 0)
    def _():
        cast(0)

    for slot in range(2):
        @pl.when(jnp.logical_and(jj > 0, jj % 2 == slot))
        def _():
            compute(1 - slot)
            cast(slot)


def _chunk_rows(w_ref):
    ck = w_ref.shape[0]
    return pl.ds(pl.multiple_of(pl.program_id(1) * ck, ck), ck)


def _mm_ws_kernel(*refs, scaled):
    if scaled:
        x_ref, xm_ref, w_ref, s_ref, o_ref, om_ref, wa_ref, wb_ref = refs
    else:
        x_ref, xm_ref, w_ref, o_ref, om_ref, wa_ref, wb_ref = refs
    bufs = (wa_ref, wb_ref)

    def cast(slot):
        w = w_ref[...]
        if scaled:
            w = w * s_ref[...]
        bufs[slot][_chunk_rows(w_ref), :] = w.astype(jnp.bfloat16)

    def compute(slot):
        @pl.when(pl.program_id(1) == 0)
        def _():
            om_ref[...] = jnp.dot(xm_ref[...], bufs[slot][...],
                                  preferred_element_type=jnp.float32).astype(om_ref.dtype)

        o_ref[...] = jnp.dot(x_ref[...], bufs[slot][...], preferred_element_type=jnp.float32).astype(o_ref.dtype)

    _rolling(cast, compute)


def _ws_geometry(m, k, tm):
    ni = m // tm
    assert k % ni == 0 and (k // ni) % BF16_ROWS == 0, (m, k, tm)
    return ni, k // ni


def matmul_ws(x, x_meta, w_stack, layer, n, out_dtype, scale=None, tm=1024, tn=1024):
    m, k = x.shape
    tm, tn = _tile(m, tm), _tile(n, tn)
    ni, ck = _ws_geometry(m, k, tm)
    nj = n // tn
    scaled = scale is not None
    nxt = lambda jj: jnp.minimum(jj, nj - 1)
    cur = lambda jj: jnp.maximum(jj - 1, 0)
    in_specs = [pl.BlockSpec((tm, k), lambda jj, i: (jnp.where(jj == 0, 0, i), 0)),
                pl.BlockSpec(x_meta.shape, lambda jj, i: (0, 0)),
                pl.BlockSpec((None, ck, tn), lambda jj, i: (layer, i, nxt(jj)))]
    args = [x, x_meta, w_stack]
    if scaled:
        in_specs.append(pl.BlockSpec((1, tn), lambda jj, i: (0, nxt(jj))))
        args.append(scale)
    mm = x_meta.shape[0]
    return pl.pallas_call(
        functools.partial(_mm_ws_kernel, scaled=scaled),
        grid=(nj + 1, ni),
        in_specs=in_specs,
        out_specs=[pl.BlockSpec((tm, tn), lambda jj, i: (jnp.where(jj == 0, 0, i), cur(jj))),
                   pl.BlockSpec((mm, tn), lambda jj, i: (0, cur(jj)))],
        out_shape=[jax.ShapeDtypeStruct((m, n), out_dtype), jax.ShapeDtypeStruct((mm, n), out_dtype)],
        scratch_shapes=[pltpu.VMEM((k, tn), jnp.bfloat16)] * 2,
        compiler_params=_params("arbitrary", "arbitrary"),
        name="matmul_ws",
    )(*args)


def _mm_acc_kernel(x_ref, w_ref, o_ref):
    d = jnp.dot(x_ref[...], w_ref[...], preferred_element_type=jnp.float32)
    k = pl.program_id(2)

    @pl.when(k == 0)
    def _():
        o_ref[...] = d

    @pl.when(k > 0)
    def _():
        o_ref[...] += d


def matmul_ktiled(x, w, tm=512, tn=1024, tk=5504):
    m, k = x.shape
    n = w.shape[1]
    tm, tn, tk = _tile(m, tm), _tile(n, tn), _tile(k, tk)
    return pl.pallas_call(
        _mm_acc_kernel,
        grid=(m // tm, n // tn, k // tk),
        in_specs=[pl.BlockSpec((tm, tk), lambda i, j, kk: (i, kk)),
                  pl.BlockSpec((tk, tn), lambda i, j, kk: (kk, j))],
        out_specs=pl.BlockSpec((tm, tn), lambda i, j, kk: (i, j)),
        out_shape=jax.ShapeDtypeStruct((m, n), jnp.float32),
        compiler_params=_params("arbitrary", "arbitrary", "arbitrary"),
        name="matmul_ktiled",
    )(x, w)


def _ffn_up_kernel(x_ref, xm_ref, wg_ref, wu_ref, o_ref, om_ref, ga_ref, ua_ref, gb_ref, ub_ref):
    bufs = ((ga_ref, ua_ref), (gb_ref, ub_ref))

    def cast(slot):
        rows = _chunk_rows(wg_ref)
        bufs[slot][0][rows, :] = wg_ref[...].astype(jnp.bfloat16)
        bufs[slot][1][rows, :] = wu_ref[...].astype(jnp.bfloat16)

    def compute(slot):
        wg_ref_b, wu_ref_b = bufs[slot]

        def act(x):
            g = jnp.dot(x, wg_ref_b[...], preferred_element_type=jnp.float32)
            u = jnp.dot(x, wu_ref_b[...], preferred_element_type=jnp.float32)
            return ((g / (1.0 + jnp.exp(-g))) * u).astype(o_ref.dtype)

        @pl.when(pl.program_id(1) == 0)
        def _():
            om_ref[...] = act(xm_ref[...])

        o_ref[...] = act(x_ref[...])

    _rolling(cast, compute)


def ffn_up(x, x_meta, wg_stack, wu_stack, layer, tm=1024, tf=FF_TILE):
    m, k = x.shape
    d_ff = wg_stack.shape[-1]
    tm = _tile(m, tm)
    ni, ck = _ws_geometry(m, k, tm)
    nj = pl.cdiv(d_ff, tf)
    mm = x_meta.shape[0]
    nxt = lambda jj: jnp.minimum(jj, nj - 1)
    cur = lambda jj: jnp.maximum(jj - 1, 0)
    wspec = pl.BlockSpec((None, ck, tf), lambda jj, i: (layer, i, nxt(jj)))
    return pl.pallas_call(
        _ffn_up_kernel,
        grid=(nj + 1, ni),
        in_specs=[pl.BlockSpec((tm, k), lambda jj, i: (jnp.where(jj == 0, 0, i), 0)),
                  pl.BlockSpec((mm, k), lambda jj, i: (0, 0)),
                  wspec, wspec],
        out_specs=[pl.BlockSpec((tm, tf), lambda jj, i: (jnp.where(jj == 0, 0, i), cur(jj))),
                   pl.BlockSpec((mm, tf), lambda jj, i: (0, cur(jj)))],
        out_shape=[jax.ShapeDtypeStruct((m, d_ff), jnp.bfloat16),
                   jax.ShapeDtypeStruct((mm, d_ff), jnp.bfloat16)],
        scratch_shapes=[pltpu.VMEM((k, tf), jnp.bfloat16)] * 4,
        compiler_params=_params("arbitrary", "arbitrary"),
        name="ffn_up",
    )(x, x_meta, wg_stack, wu_stack)


def _rope128(r, c, sa, sb):
    return r * c + pltpu.roll(r, LANES - MLA_ROPE_DIM // 2, 1) * sa + pltpu.roll(r, MLA_ROPE_DIM // 2, 1) * sb


def _mla_latent_kernel(s_ref, gq_ref, gkv_ref, c_ref, sa_ref, sb_ref, cq_ref, ckv_ref, kr_ref):
    q0, kv0, r0 = 0, MLA_Q_RANK, MLA_Q_RANK + KV_RANK_PAD
    cq_ref[...] = _rms(s_ref[:, q0:kv0], gq_ref[...], MLA_Q_RANK).astype(cq_ref.dtype)
    ckv_ref[...] = _rms(s_ref[:, kv0:r0], gkv_ref[...], MLA_KV_RANK).astype(ckv_ref.dtype)
    kr_ref[...] = _rope128(s_ref[:, r0:r0 + LANES], c_ref[...], sa_ref[...], sb_ref[...]).astype(kr_ref.dtype)


def mla_latent(small, gq, gkv_pad, tabs, seq, tm=512):
    m = small.shape[0]
    tm = _tile(m, tm)
    nseq = seq // tm
    tab = pl.BlockSpec((tm, LANES), lambda i: (i % nseq, 0))
    return pl.pallas_call(
        _mla_latent_kernel,
        grid=(m // tm,),
        in_specs=[pl.BlockSpec((tm, EVEN_SMALL), lambda i: (i, 0)),
                  pl.BlockSpec((1, MLA_Q_RANK), lambda i: (0, 0)),
                  pl.BlockSpec((1, KV_RANK_PAD), lambda i: (0, 0)), tab, tab, tab],
        out_specs=[pl.BlockSpec((tm, MLA_Q_RANK), lambda i: (i, 0)),
                   pl.BlockSpec((tm, KV_RANK_PAD), lambda i: (i, 0)),
                   pl.BlockSpec((tm, LANES), lambda i: (i, 0))],
        out_shape=[jax.ShapeDtypeStruct((m, MLA_Q_RANK), jnp.bfloat16),
                   jax.ShapeDtypeStruct((m, KV_RANK_PAD), jnp.bfloat16),
                   jax.ShapeDtypeStruct((m, LANES), jnp.bfloat16)],
        compiler_params=_params("arbitrary"),
        name="mla_latent",
    )(small, gq.reshape(1, -1), gkv_pad.reshape(1, -1), *tabs)


def _uq_rope_kernel(x_ref, w_ref, c_ref, sa_ref, sb_ref, o_ref):
    q = jnp.dot(x_ref[...], w_ref[...], preferred_element_type=jnp.float32)
    c, sa, sb = c_ref[...], sa_ref[...], sb_ref[...]
    one = (lax.broadcasted_iota(jnp.int32, (1, LANES), 1) == MLA_MASK_LANE).astype(jnp.float32)
    for h in range(q.shape[1] // (2 * LANES)):
        n0, r0 = 2 * h * LANES, (2 * h + 1) * LANES
        o_ref[:, n0:r0] = q[:, n0:r0].astype(o_ref.dtype)
        o_ref[:, r0:r0 + LANES] = (_rope128(q[:, r0:r0 + LANES], c, sa, sb) + one).astype(o_ref.dtype)


def uq_rope(cq, w_uq_p, tabs, seq, tm=1024, tn=1024):
    m, k = cq.shape
    n = w_uq_p.shape[1]
    tm, tn = _tile(seq, tm), _tile(n, tn, 2 * LANES)
    nseq = seq // tm
    tab = pl.BlockSpec((tm, LANES), lambda i, j: (i % nseq, 0))
    return pl.pallas_call(
        _uq_rope_kernel, grid=(m // tm, n // tn),
        in_specs=[pl.BlockSpec((tm, k), lambda i, j: (i, 0)), pl.BlockSpec((k, tn), lambda i, j: (0, j)),
                  tab, tab, tab],
        out_specs=pl.BlockSpec((tm, tn), lambda i, j: (i, j)),
        out_shape=jax.ShapeDtypeStruct((m, n), jnp.bfloat16),
        compiler_params=_params("arbitrary", "arbitrary"), name="uq_rope",
    )(cq, w_uq_p, *tabs)


def _forget_cumsum_kernel(x_ref, b_ref, o_ref):
    z = x_ref[...] + b_ref[...]
    log_f = jnp.minimum(z, 0.0) - jnp.log(1.0 + jnp.exp(-jnp.abs(z)))
    n = log_f.shape[-1]
    row = lax.broadcasted_iota(jnp.int32, (LANES, LANES), 0)
    col = lax.broadcasted_iota(jnp.int32, (LANES, LANES), 1)
    upper = (row <= col).astype(jnp.float32)
    carry = jnp.zeros((log_f.shape[0], 1), jnp.float32)
    for c in range(n // LANES):
        chunk = jnp.dot(log_f[:, c * LANES:(c + 1) * LANES], upper,
                        precision=lax.Precision.HIGHEST, preferred_element_type=jnp.float32) + carry
        o_ref[:, c * LANES:(c + 1) * LANES] = chunk
        carry = chunk[:, LANES - 1:LANES]


def forget_cumsum(f_logit, b_f):
    b, h, n = f_logit.shape
    return pl.pallas_call(
        _forget_cumsum_kernel,
        grid=(b,),
        in_specs=[pl.BlockSpec((None, h, n), lambda i: (i, 0, 0)), pl.BlockSpec((h, 1), lambda i: (0, 0))],
        out_specs=pl.BlockSpec((None, h, n), lambda i: (i, 0, 0)),
        out_shape=jax.ShapeDtypeStruct((b, h, n), jnp.float32),
        compiler_params=_params("arbitrary"),
        name="forget_cumsum",
    )(f_logit, b_f.reshape(h, 1))


def _t5_tile_kernel(rb_ref, o_ref, *, offset, causal):
    h = pl.program_id(0)
    shape = o_ref.shape
    i = lax.broadcasted_iota(jnp.int32, shape, 0)
    j = lax.broadcasted_iota(jnp.int32, shape, 1)
    dist = jnp.maximum(j - i + offset, 0)
    max_exact = REL_BUCKETS // 2
    d = jnp.maximum(dist, 1).astype(jnp.float32)
    large = max_exact + (jnp.log(d / max_exact) / math.log(REL_MAX_DIST / max_exact)
                         * (REL_BUCKETS - max_exact)).astype(jnp.int32)
    large = jnp.minimum(large, REL_BUCKETS - 1)
    bucket = jnp.where(dist < max_exact, dist, large)
    far = rb_ref[h, REL_BUCKETS - 1]
    tile = jnp.zeros(shape, jnp.float32)
    for b in range(REL_BUCKETS):
        tile = jnp.where(bucket == b, (rb_ref[h, b] - far) * LOG2E, tile)
    if causal:
        tile = jnp.where(i > j, MASK_VALUE, tile)
    o_ref[...] = tile


def t5_tiles(rel_bias_t, rows, cols, offset, causal):
    h = rel_bias_t.shape[0]
    return pl.pallas_call(
        functools.partial(_t5_tile_kernel, offset=offset, causal=causal),
        grid=(h,),
        in_specs=[pl.BlockSpec(memory_space=pltpu.SMEM)],
        out_specs=pl.BlockSpec((None, rows, cols), lambda i: (i, 0, 0)),
        out_shape=jax.ShapeDtypeStruct((h, rows, cols), jnp.float32),
        compiler_params=_params("arbitrary"),
        name="t5_tiles",
    )(rel_bias_t)


def _q_aug(q):
    ones = (lax.broadcasted_iota(jnp.int32, (LANES, q.shape[0]), 0) < BIAS_SLOTS).astype(q.dtype)
    return jnp.concatenate([q.T, ones], axis=0)


def _v_aug(v):
    return jnp.concatenate([v.T, jnp.ones((BF16_ROWS, v.shape[0]), v.dtype)], axis=0)


def _k_aug(k, extra):
    return jnp.concatenate([k, extra], axis=1)


def _init_state(tq, dv):
    return (jnp.full((1, tq), MASK_VALUE, jnp.float32), jnp.zeros((dv + BF16_ROWS, tq), jnp.float32))


def _score(k_aug, q_aug):
    return jnp.dot(k_aug, q_aug, preferred_element_type=jnp.float32)


def _causal(s):
    i = lax.broadcasted_iota(jnp.int32, s.shape, 0)
    j = lax.broadcasted_iota(jnp.int32, s.shape, 1)
    return jnp.where(i > j, MASK_VALUE, s)


def _consume(s, v_aug, state):
    m, acc = state
    m_new = jnp.maximum(m, jnp.max(s, axis=0, keepdims=True))
    p = jnp.exp2(s - m_new).astype(v_aug.dtype)
    acc = jnp.exp2(m - m_new) * acc + jnp.dot(v_aug, p, preferred_element_type=jnp.float32)
    return m_new, acc


def _next_tile(qb, kb):
    wrap = kb == qb
    return jnp.where(wrap, qb + 1, qb), jnp.where(wrap, 0, kb + 1)


def _flat_sweep(nq, score, consume, bufs):
    sa, sb = bufs
    n_steps = nq * (nq + 1) // 2 - 1
    zero = jnp.int32(0)
    score(zero, zero, sa)

    def pair(tile):
        t1 = _next_tile(*tile)
        t2 = _next_tile(*t1)
        score(*t1, sb)
        consume(*tile, sa)
        score(*t2, sa)
        consume(*t1, sb)
        return t2

    def trip(_, tile):
        for _ in range(SWEEP_PAIRS_PER_TRIP):
            tile = pair(tile)
        return tile

    tile = lax.fori_loop(0, n_steps // (2 * SWEEP_PAIRS_PER_TRIP), trip, (zero, zero))
    rest = n_steps % (2 * SWEEP_PAIRS_PER_TRIP)
    for _ in range(rest // 2):
        tile = pair(tile)
    if rest % 2:
        t1 = _next_tile(*tile)
        score(*t1, sb)
        consume(*tile, sa)
        consume(*t1, sb)
    else:
        consume(*tile, sa)


def _finish(state, dv):
    _, acc = state
    return (acc[:dv] * (1.0 / acc[dv:dv + 1])).T


def _resume(first, m, acc):
    return jnp.where(first, MASK_VALUE, m), jnp.where(first, 0.0, acc)


def _single_meta_kernel(q_ref, km_ref, xm_ref, vm_ref, o_ref, *, rope_keys):
    q_aug = q_ref[...].T if rope_keys else _q_aug(q_ref[...])
    dv = vm_ref.shape[-1]
    s = _causal(_score(_k_aug(km_ref[...], xm_ref[...]), q_aug))
    state = _consume(s, _v_aug(vm_ref[...]), _init_state(q_aug.shape[1], dv))
    o_ref[...] = _finish(state, dv).astype(o_ref.dtype)


def _single_kernel(q_ref, kr_ref, xr_ref, vr_ref, km_ref, xm_ref, vm_ref, add_ref, o_ref,
                   kar_ref, var_ref, kam_ref, vam_ref, sa_ref, sb_ref, qa_ref, m_ref, acc_ref, *, rope_keys):
    nq, t = kr_ref.shape[0], kr_ref.shape[1]
    dv = vm_ref.shape[-1]
    kam_ref[...] = _k_aug(km_ref[...], xm_ref[...])
    vam_ref[...] = _v_aug(vm_ref[...])
    for blk in range(nq):
        kar_ref[blk] = _k_aug(kr_ref[blk], xr_ref[blk])
        var_ref[blk] = _v_aug(vr_ref[blk])
        q = q_ref[blk * t:(blk + 1) * t, :]
        qa_ref[blk] = q.T if rope_keys else _q_aug(q)

    def score(qb, kb, buf):
        buf[...] = _score(kar_ref[kb], qa_ref[qb])

    def consume(qb, kb, buf):
        s = buf[...] + add_ref[(kb == qb).astype(jnp.int32)]
        m_ref[qb], acc_ref[qb] = _consume(s, var_ref[kb], _resume(kb == 0, m_ref[qb], acc_ref[qb]))

    _flat_sweep(nq, score, consume, (sa_ref, sb_ref))
    s_meta = [_score(kam_ref[...], qa_ref[qb]) for qb in range(nq)]
    for qb in range(nq):
        state = _consume(s_meta[qb], vam_ref[...], (m_ref[qb], acc_ref[qb]))
        o_ref[qb * t:(qb + 1) * t, :] = _finish(state, dv).astype(o_ref.dtype)


def _diff_out(outs, lam_ref, g_ref, lam_init, dv):
    lam = lam_ref[...]
    lam_full = (jnp.exp(jnp.sum(lam[0:1] * lam[1:2], axis=-1, keepdims=True))
                - jnp.exp(jnp.sum(lam[2:3] * lam[3:4], axis=-1, keepdims=True)) + lam_init)
    return _rms(outs[0] - lam_full * outs[1], g_ref[...], dv) * (1.0 - lam_init)


def _diff_meta_kernel(q_ref, km_ref, xm_ref, vm_ref, tm_ref, lam_ref, g_ref, o_ref, *, lam_init):
    d = DIFF_HEAD_DIM
    dv = vm_ref.shape[-1]
    km, xm, v_aug = km_ref[...], xm_ref[...], _v_aug(vm_ref[...])
    outs = []
    for c in range(2):
        q_aug = _q_aug(q_ref[:, c * d:(c + 1) * d])
        s = _score(_k_aug(km[:, c * d:(c + 1) * d], xm), q_aug) + tm_ref[...]
        outs.append(_finish(_consume(s, v_aug, _init_state(q_aug.shape[1], dv)), dv))
    o_ref[...] = _diff_out(outs, lam_ref, g_ref, lam_init, dv).astype(o_ref.dtype)


def _diff_kernel(q_ref, kr_ref, xr_ref, vr_ref, km_ref, xm_ref, vm_ref, tm_ref, add_ref, lam_ref, g_ref, o_ref,
                 kar_ref, var_ref, kam_ref, vam_ref, sa_ref, sb_ref, qa_ref, m_ref, acc_ref, *, lam_init):
    nq, t = kr_ref.shape[0], kr_ref.shape[1]
    d = DIFF_HEAD_DIM
    dv = vm_ref.shape[-1]
    km, xm, xr = km_ref[...], xm_ref[...], xr_ref[...]
    vam_ref[...] = _v_aug(vm_ref[...])
    for c in range(2):
        kam_ref[c] = _k_aug(km[:, c * d:(c + 1) * d], xm)
    for blk in range(nq):
        var_ref[blk] = _v_aug(vr_ref[blk])
        for c in range(2):
            kar_ref[c, blk] = _k_aug(kr_ref[blk, :, c * d:(c + 1) * d], xr)
            qa_ref[c, blk] = _q_aug(q_ref[blk * t:(blk + 1) * t, c * d:(c + 1) * d])

    def score(qb, kb, buf):
        for c in range(2):
            buf[c] = _score(kar_ref[c, kb], qa_ref[c, qb])

    def consume(qb, kb, buf):
        add = add_ref[jnp.clip(kb - qb + 2, 0, 2)]
        for c in range(2):
            m_ref[c, qb], acc_ref[c, qb] = _consume(buf[c] + add, var_ref[kb],
                                                    _resume(kb == 0, m_ref[c, qb], acc_ref[c, qb]))

    _flat_sweep(nq, score, consume, (sa_ref, sb_ref))
    for qb in range(nq):
        outs = []
        for c in range(2):
            s = _score(kam_ref[c], qa_ref[c, qb]) + tm_ref[min(qb, 1)]
            outs.append(_finish(_consume(s, vam_ref[...], (m_ref[c, qb], acc_ref[c, qb])), dv))
        o_ref[qb * t:(qb + 1) * t, :] = _diff_out(outs, lam_ref, g_ref, lam_init, dv).astype(o_ref.dtype)


def _attn_call(kernel, name, b, heads, nq, tq, dv, in_specs, args, scratch=()):
    return pl.pallas_call(
        kernel,
        grid=(b, heads, nq),
        in_specs=in_specs,
        out_specs=pl.BlockSpec((tq, dv), lambda bi, h, qi: (bi * nq + qi, h)),
        out_shape=jax.ShapeDtypeStruct((b * nq * tq, heads * dv), jnp.bfloat16),
        scratch_shapes=list(scratch),
        compiler_params=_params("arbitrary", "arbitrary", "arbitrary"),
        name=name,
    )(*args)


def _aug_scratch(n_comp, nkb, t, dv):
    lead = (n_comp,) if n_comp > 1 else ()
    bf = jnp.bfloat16
    return [pltpu.VMEM(lead + (nkb, t, 2 * LANES), bf), pltpu.VMEM((nkb, dv + BF16_ROWS, t), bf),
            pltpu.VMEM(lead + (META_BLOCK, 2 * LANES), bf), pltpu.VMEM((dv + BF16_ROWS, META_BLOCK), bf),
            pltpu.VMEM(lead + (t, t), jnp.float32), pltpu.VMEM(lead + (t, t), jnp.float32),
            pltpu.VMEM(lead + (nkb, 2 * LANES, t), bf), pltpu.VMEM(lead + (nkb, 1, t), jnp.float32),
            pltpu.VMEM(lead + (nkb, dv + BF16_ROWS, t), jnp.float32)]


def _causal_tiles(t):
    i = lax.broadcasted_iota(jnp.int32, (t, t), 0)
    j = lax.broadcasted_iota(jnp.int32, (t, t), 1)
    return jnp.stack([jnp.zeros((t, t), jnp.float32), jnp.where(i > j, MASK_VALUE, 0.0).astype(jnp.float32)])


def _whole(shape):
    return pl.BlockSpec(shape, lambda bi, h, qi: (0,) * len(shape))


def _qspec(tq, dq, nq, col=lambda h: h):
    return pl.BlockSpec((tq, dq), lambda bi, h, qi: (bi * nq + qi, col(h)))


def _kspec(nkb, tk, d, col):
    return pl.BlockSpec((None, nkb, tk, d), lambda bi, h, qi: (bi, 0, 0, col(h)))


def _mspec(d, col):
    return pl.BlockSpec((META_BLOCK, d), lambda bi, h, qi: (0, col(h)))


def fox_attention(qkv, qkv_meta, extra, extra_meta, b, seq, t=ATTN_TILE):
    hd = FOX_HEADS
    xm_spec = pl.BlockSpec((None, META_BLOCK, LANES), lambda bi, h, qi: (h, 0, 0))
    meta_specs = [_mspec(FOX_DIM, lambda h: hd + h), xm_spec, _mspec(FOX_DIM, lambda h: 2 * hd + h)]
    meta_args = (qkv_meta, extra_meta, qkv_meta)
    if qkv is None:
        return _attn_call(functools.partial(_single_meta_kernel, rope_keys=False),
                          "fox_attention_meta", 1, hd, 1, META_BLOCK, FOX_DIM,
                          [_qspec(META_BLOCK, FOX_DIM, 1)] + meta_specs, (qkv_meta,) + meta_args)
    t = _tile(seq, t)
    nq = seq // t
    kv4 = qkv.reshape(b, nq, t, qkv.shape[-1])
    return _attn_call(
        functools.partial(_single_kernel, rope_keys=False), "fox_attention",
        b, hd, 1, seq, FOX_DIM,
        [_qspec(seq, FOX_DIM, 1), _kspec(nq, t, FOX_DIM, lambda h: hd + h),
         pl.BlockSpec((None, None, nq, t, LANES), lambda bi, h, qi: (bi, h, 0, 0, 0)),
         _kspec(nq, t, FOX_DIM, lambda h: 2 * hd + h)] + meta_specs + [_whole((2, t, t))],
        (qkv, kv4, extra.reshape(b, hd, nq, t, LANES), kv4) + meta_args + (_causal_tiles(t),),
        _aug_scratch(1, nq, t, FOX_DIM))


def mla_attention(q, kv, kr, q_meta, kv_meta, kr_meta, b, seq, t=ATTN_TILE):
    hd = MLA_HEADS
    meta_specs = [_mspec(LANES, lambda h: 2 * h), _mspec(LANES, lambda h: 0), _mspec(LANES, lambda h: 2 * h + 1)]
    meta_args = (kv_meta, kr_meta, kv_meta)
    if q is None:
        return _attn_call(functools.partial(_single_meta_kernel, rope_keys=True),
                          "mla_attention_meta", 1, hd, 1, META_BLOCK, MLA_V_DIM,
                          [_qspec(META_BLOCK, 2 * LANES, 1)] + meta_specs, (q_meta,) + meta_args)
    t = _tile(seq, t)
    nq = seq // t
    kv4 = kv.reshape(b, nq, t, kv.shape[-1])
    kr4 = kr.reshape(b, nq, t, kr.shape[-1])
    return _attn_call(
        functools.partial(_single_kernel, rope_keys=True), "mla_attention",
        b, hd, 1, seq, MLA_V_DIM,
        [_qspec(seq, 2 * LANES, 1), _kspec(nq, t, LANES, lambda h: 2 * h),
         _kspec(nq, t, LANES, lambda h: 0), _kspec(nq, t, LANES, lambda h: 2 * h + 1)] + meta_specs
        + [_whole((2, t, t))],
        (q, kv4, kr4, kv4) + meta_args + (_causal_tiles(t),), _aug_scratch(1, nq, t, MLA_V_DIM))


def diff_attention(qkv, qkv_meta, extra, extra_meta, tiles, lam, subln, lam_init, b, seq, t=ATTN_TILE):
    hd = DIFF_HEADS
    dv = 2 * DIFF_HEAD_DIM
    tile_meta_only, tile_meta, tile_real = tiles
    lam_spec = pl.BlockSpec((4, DIFF_HEAD_DIM), lambda bi, h, qi: (0, 0))
    g_spec = pl.BlockSpec((1, dv), lambda bi, h, qi: (0, 0))
    kcol, vcol = (lambda h: hd + h), (lambda h: 2 * hd + h)
    per_head = lambda r, c: pl.BlockSpec((None, r, c), lambda bi, h, qi: (h, 0, 0))
    meta_specs = [_mspec(dv, kcol), per_head(META_BLOCK, LANES), _mspec(dv, vcol)]
    meta_args = (qkv_meta, extra_meta, qkv_meta)
    tail = (lam, subln.reshape(1, dv))
    if qkv is None:
        return _attn_call(
            functools.partial(_diff_meta_kernel, lam_init=lam_init),
            "diff_attention_meta", 1, hd, 1, META_BLOCK, dv,
            [_qspec(META_BLOCK, dv, 1)] + meta_specs + [per_head(META_BLOCK, META_BLOCK), lam_spec, g_spec],
            (qkv_meta,) + meta_args + (tile_meta_only,) + tail)
    t = _tile(seq, t)
    nq = seq // t
    kv4 = qkv.reshape(b, nq, t, qkv.shape[-1])
    return _attn_call(
        functools.partial(_diff_kernel, lam_init=lam_init),
        "diff_attention", b, hd, 1, seq, dv,
        [_qspec(seq, dv, 1), _kspec(nq, t, dv, kcol), per_head(t, LANES), _kspec(nq, t, dv, vcol)] + meta_specs
        + [pl.BlockSpec((None, 2, META_BLOCK, t), lambda bi, h, qi: (h, 0, 0, 0)),
           pl.BlockSpec((None, 3, t, t), lambda bi, h, qi: (h, 0, 0, 0)), lam_spec, g_spec],
        (qkv, kv4, extra, kv4) + meta_args + (tile_meta, tile_real) + tail,
        _aug_scratch(2, nq, t, dv))


def _front_pad(x):
    return jnp.pad(x, ((META_BLOCK - x.shape[0], 0), (0, 0)))


def _split3(x):
    def head(v):
        bits = lax.bitcast_convert_type(v, jnp.uint32) & jnp.uint32(0xFFFF0000)
        return lax.bitcast_convert_type(bits, jnp.float32)

    hi = head(x)
    mid = head(x - hi)
    lo = head(x - hi - mid)
    return tuple(p.astype(jnp.bfloat16) for p in (hi, mid, lo))


def _extra_lanes(x, mask=None):
    pieces = list(_split3(x)) + [jnp.zeros(x.shape, jnp.bfloat16) if mask is None else mask.astype(jnp.bfloat16)]
    out = jnp.stack(pieces, axis=-1)
    return jnp.pad(out, [(0, 0)] * x.ndim + [(0, LANES - BIAS_SLOTS)])


def _pad_mask(heads):
    m = jnp.where(jnp.arange(META_BLOCK) < META_BLOCK - N_META, MASK_VALUE, 0.0).astype(jnp.float32)
    return jnp.broadcast_to(m, (heads, META_BLOCK))


def _rope_tables(n_pos):
    inv = ROPE_THETA ** (-jnp.arange(0, MLA_ROPE_DIM, 2, dtype=jnp.float32) / MLA_ROPE_DIM)
    ang = jnp.arange(n_pos, dtype=jnp.float32)[:, None] * inv[None, :]
    cos, sin = jnp.cos(ang), jnp.sin(ang)
    z32 = jnp.zeros_like(cos)
    z64 = jnp.zeros((n_pos, LANES - MLA_ROPE_DIM), jnp.float32)
    c = jnp.concatenate([cos, cos, z64], axis=1)
    sa = jnp.concatenate([-sin, z32, z64], axis=1)
    sb = jnp.concatenate([z32, sin, z64], axis=1)
    return c, sa, sb


def _col_scale(n, n_scaled, value):
    return jnp.where(jnp.arange(n) < n_scaled, value, 1.0).astype(jnp.float32)[None, :]


def _even_weights(w_in, w_uq, w_ukv, gkv):
    o = [0]
    for s in (FOX_HEADS * FOX_DIM,) * 3 + (FOX_HEADS, MLA_Q_RANK, MLA_KV_RANK, MLA_ROPE_DIM):
        o.append(o[-1] + s)
    bf = jnp.bfloat16
    mla_scale = (MLA_NOPE_DIM + MLA_ROPE_DIM) ** -0.5 * LOG2E
    pad = lambda w, n: jnp.pad(w, ((0, 0), (0, n - w.shape[1])))
    w_small = jnp.concatenate([w_in[:, o[4]:o[5]], pad(w_in[:, o[5]:o[6]], KV_RANK_PAD),
                               pad(w_in[:, o[6]:o[7]], LANES), pad(w_in[:, o[3]:o[4]], LANES)],
                              axis=1).astype(bf)
    hq = (w_uq * mla_scale).reshape(MLA_Q_RANK, MLA_HEADS, MLA_NOPE_DIM + MLA_ROPE_DIM)
    hq = jnp.pad(hq, ((0, 0), (0, 0), (0, 2 * LANES - hq.shape[-1])))
    w_uq_p = hq.reshape(MLA_Q_RANK, MLA_HEADS * 2 * LANES).astype(bf)
    w_ukv_p = jnp.pad(w_ukv, ((0, KV_RANK_PAD - MLA_KV_RANK), (0, 0))).astype(bf)
    gkv_p = jnp.pad(gkv, (0, KV_RANK_PAD - MLA_KV_RANK))
    return w_small, w_uq_p, w_ukv_p, gkv_p


def _even_mixer(hn, hn_m, b, seq, i, w_in, ev_w_qkv, b_f, gq, gkv, w_uq, w_ukv, ev_w_out, tabs, tabs_m):
    w_small, w_uq_p, w_ukv_p, gkv_p = _even_weights(w_in, w_uq, w_ukv, gkv)
    f0 = MLA_Q_RANK + KV_RANK_PAD + LANES
    n_q = FOX_HEADS * FOX_DIM
    qkv, qkv_m = matmul_ws(hn, hn_m, ev_w_qkv, i, 3 * n_q, jnp.bfloat16,
                           _col_scale(3 * n_q, n_q, FOX_DIM ** -0.5 * LOG2E))

    def stream(x, tab, n_pos):
        small = matmul(x, w_small, jnp.float32, tn=EVEN_SMALL // 2)
        cq, ckv, kr = mla_latent(small, gq, gkv_p, tab, n_pos)
        q = uq_rope(cq, w_uq_p, tab, n_pos)
        kv = matmul(ckv, w_ukv_p, jnp.bfloat16)
        return small[:, f0:f0 + FOX_HEADS], q, kv, kr

    fl, q, kv, kr = stream(hn, tabs, seq)
    fl_m, q_m, kv_m, kr_m = stream(hn_m, tabs_m, N_META)

    cum = forget_cumsum(fl.reshape(b, seq, FOX_HEADS).transpose(0, 2, 1), b_f)
    fl_m = jnp.pad(fl_m.T, ((0, 0), (0, LANES - N_META)))[None]
    cum_m = forget_cumsum(fl_m, b_f)[0, :, :N_META]
    mask = _pad_mask(FOX_HEADS)
    front = lambda x: jnp.pad(x, ((0, 0), (META_BLOCK - N_META, 0)))
    extra = _extra_lanes(-LOG2E * cum)
    extra_meta = _extra_lanes(front(LOG2E * (cum_m[:, -1:] - cum_m)), mask)
    extra_meta_only = _extra_lanes(front(-LOG2E * cum_m), mask)

    qkv_mp, q_mp, kv_mp, kr_mp = map(_front_pad, (qkv_m, q_m, kv_m, kr_m))
    kr_mp = kr_mp.at[:META_BLOCK - N_META, MLA_MASK_LANE].set(MASK_VALUE)
    o_f = fox_attention(qkv, qkv_mp, extra, extra_meta, b, seq)
    o_f_m = fox_attention(None, qkv_mp, None, extra_meta_only, b, seq)
    o_m = mla_attention(q, kv, kr, q_mp, kv_mp, kr_mp, b, seq)
    o_m_m = mla_attention(None, None, None, q_mp, kv_mp, kr_mp, b, seq)
    o = jnp.concatenate([o_f, o_m], axis=1)
    o_meta = jnp.concatenate([o_f_m, o_m_m], axis=1)[-N_META:]
    return matmul_ws(o, o_meta, ev_w_out, i, D_MODEL, jnp.float32)


def _diff_mixer(hn, hn_m, b, seq, i, od_w_in, lam, subln, od_w_out, bias, lam_init):
    n_q = DIFF_HEADS * 2 * DIFF_HEAD_DIM
    extra, extra_meta, tiles = bias
    qkv, qkv_m = matmul_ws(hn, hn_m, od_w_in, i, 3 * n_q, jnp.bfloat16,
                           _col_scale(3 * n_q, n_q, DIFF_HEAD_DIM ** -0.5 * LOG2E))
    qkv_mp = _front_pad(qkv_m)
    o = diff_attention(qkv, qkv_mp, extra, extra_meta, tiles, lam, subln, lam_init, b, seq)
    o_meta = diff_attention(None, qkv_mp, None, extra_meta, tiles, lam, subln, lam_init, b, seq)
    return matmul_ws(o, o_meta[-N_META:], od_w_out, i, D_MODEL, jnp.float32)


def _diff_bias(rel_bias, t):
    rb_t = rel_bias.astype(jnp.float32).T
    far = LOG2E * rb_t[:, REL_BUCKETS - 1]
    extra = _extra_lanes(jnp.broadcast_to(far[:, None], (DIFF_HEADS, t)))
    extra_meta = _extra_lanes(jnp.broadcast_to(far[:, None], (DIFF_HEADS, META_BLOCK)), _pad_mask(DIFF_HEADS))
    tile_meta0 = t5_tiles(rb_t, META_BLOCK, t, META_BLOCK, False)
    tile_sub, tile_diag = t5_tiles(rb_t, t, t, t, False), t5_tiles(rb_t, t, t, 0, True)
    tiles = (t5_tiles(rb_t, META_BLOCK, META_BLOCK, 0, True),
             jnp.stack([tile_meta0, jnp.zeros_like(tile_meta0)], axis=1),
             jnp.stack([jnp.zeros_like(tile_sub), tile_sub, tile_diag], axis=1))
    return extra, extra_meta, tiles


def kernel(x, meta_tokens, rel_bias, ev_w_in, ev_b_f, ev_q_norm, ev_kv_norm, ev_w_uq, ev_w_ukv, ev_w_out,
           od_w_in, od_lambda, od_subln, od_w_out, norm_g, ffn_w_gate, ffn_w_up, ffn_w_down):
    b, seq, d = x.shape
    hs = x.reshape(b * seq, d)
    hs_m = meta_tokens.astype(x.dtype)

    c, sa, sb = _rope_tables(N_META + seq)
    tabs_m = tuple(tb[:N_META] for tb in (c, sa, sb))
    tabs = tuple(tb[N_META:] for tb in (c, sa, sb))
    bias = _diff_bias(rel_bias, _tile(seq, ATTN_TILE))
    ev_w_qkv = ev_w_in[:, :, :3 * FOX_HEADS * FOX_DIM]

    hn = rmsnorm_bf16(hs, norm_g[0, 0])
    hn_m = rmsnorm_bf16(hs_m, norm_g[0, 0])
    for layer in range(DEPTH):
        g = norm_g[layer]
        i = layer // 2
        if layer % 2 == 0:
            m, m_m = _even_mixer(hn, hn_m, b, seq, i, ev_w_in[i], ev_w_qkv, ev_b_f[i], ev_q_norm[i], ev_kv_norm[i],
                                 ev_w_uq[i], ev_w_ukv[i], ev_w_out, tabs, tabs_m)
        else:
            lam_init = 0.8 - 0.6 * math.exp(-0.3 * layer)
            m, m_m = _diff_mixer(hn, hn_m, b, seq, i, od_w_in, od_lambda[i], od_subln[i], od_w_out,
                                 bias, lam_init)
        hs, hn = resid_norm(hs, m, g[1], g[2])
        hs_m, hn_m = resid_norm(hs_m, m_m, g[1], g[2])
        h, h_m = ffn_up(hn, hn_m, ffn_w_gate, ffn_w_up, layer)
        wd = ffn_w_down[layer].astype(jnp.bfloat16)
        f, f_m = matmul_ktiled(h, wd), matmul_ktiled(h_m, wd)
        g_next = norm_g[layer + 1, 0] if layer + 1 < DEPTH else None
        hs, hn = resid_norm(hs, f, g[3], g_next)
        if g_next is not None:
            hs_m, hn_m = resid_norm(hs_m, f_m, g[3], g_next)
    return hs.reshape(b, seq, d)
```

```python
import functools
import math

import jax
import jax.numpy as jnp
from jax import lax
from jax.experimental import pallas as pl
from jax.experimental.pallas import tpu as pltpu

D_MODEL = 4096
DEPTH = 4
N_META = 16
EPS = 1e-6
FOX_HEADS = 16
FOX_DIM = 128
MLA_HEADS = 16
MLA_Q_RANK = 896
MLA_KV_RANK = 320
MLA_NOPE_DIM = 128
MLA_ROPE_DIM = 64
MLA_V_DIM = 128
ROPE_THETA = 10000.0
DIFF_HEAD_DIM = 128
DIFF_HEADS = D_MODEL // (2 * DIFF_HEAD_DIM)
REL_BUCKETS = 32
REL_MAX_DIST = 128
D_FF = ((8 * D_MODEL + 3 * 256 - 1) // (3 * 256)) * 256

LANES = 128
BF16_ROWS = 16
META_BLOCK = LANES
MASK_VALUE = -1e30
VMEM_LIMIT_BYTES = 56 * 1024 * 1024
LOG2E = math.log2(math.e)

FF_TILE = 512
KV_RANK_PAD = 384
EVEN_SMALL = MLA_Q_RANK + KV_RANK_PAD + 2 * LANES
ATTN_TILE = 512
BIAS_SLOTS = 4
MLA_MASK_LANE = MLA_ROPE_DIM
SWEEP_PAIRS_PER_TRIP = 6


def _params(*sem):
    return pltpu.CompilerParams(dimension_semantics=sem, vmem_limit_bytes=VMEM_LIMIT_BYTES)


def _tile(n, t, unit=LANES):
    if n <= t:
        return n
    return max(c for c in range(unit, t + 1, unit) if n % c == 0)


def _rms(x, g, n):
    ms = jnp.sum(x * x, axis=-1, keepdims=True) / n
    return x * lax.rsqrt(ms + EPS) * g


def _norm_kernel(x_ref, g_ref, o_ref):
    o_ref[...] = _rms(x_ref[...], g_ref[...], x_ref.shape[-1]).astype(o_ref.dtype)


def rmsnorm_bf16(x, g, tm=256):
    m, d = x.shape
    tm = _tile(m, tm)
    return pl.pallas_call(
        _norm_kernel,
        grid=(m // tm,),
        in_specs=[pl.BlockSpec((tm, d), lambda i: (i, 0)), pl.BlockSpec((1, d), lambda i: (0, 0))],
        out_specs=pl.BlockSpec((tm, d), lambda i: (i, 0)),
        out_shape=jax.ShapeDtypeStruct((m, d), jnp.bfloat16),
        compiler_params=_params("arbitrary"),
        name="rmsnorm",
    )(x, g.reshape(1, d))


def _resid_norm_kernel(hs_ref, m_ref, g1_ref, g2_ref, hs_out_ref, hn_out_ref):
    d = hs_ref.shape[-1]
    hs = hs_ref[...] + _rms(m_ref[...], g1_ref[...], d)
    hs_out_ref[...] = hs
    hn_out_ref[...] = _rms(hs, g2_ref[...], d).astype(hn_out_ref.dtype)


def _resid_kernel(hs_ref, m_ref, g1_ref, hs_out_ref):
    hs_out_ref[...] = hs_ref[...] + _rms(m_ref[...], g1_ref[...], hs_ref.shape[-1])


def resid_norm(hs, m, g1, g2, tm=256):
    rows, d = hs.shape
    tm = _tile(rows, tm)
    row = pl.BlockSpec((tm, d), lambda i: (i, 0))
    vec = pl.BlockSpec((1, d), lambda i: (0, 0))
    if g2 is None:
        return pl.pallas_call(
            _resid_kernel, grid=(rows // tm,), in_specs=[row, row, vec], out_specs=row,
            out_shape=jax.ShapeDtypeStruct((rows, d), jnp.float32),
            compiler_params=_params("arbitrary"), name="resid",
        )(hs, m, g1.reshape(1, d)), None
    return pl.pallas_call(
        _resid_norm_kernel, grid=(rows // tm,), in_specs=[row, row, vec, vec], out_specs=[row, row],
        out_shape=[jax.ShapeDtypeStruct((rows, d), jnp.float32),
                   jax.ShapeDtypeStruct((rows, d), jnp.bfloat16)],
        compiler_params=_params("arbitrary"), name="resid_norm",
    )(hs, m, g1.reshape(1, d), g2.reshape(1, d))


def _mm_kernel(x_ref, w_ref, o_ref):
    o_ref[...] = jnp.dot(x_ref[...], w_ref[...], preferred_element_type=jnp.float32).astype(o_ref.dtype)


def matmul(x, w, out_dtype, tm=1024, tn=1024):
    m, k = x.shape
    n = w.shape[1]
    tm, tn = _tile(m, tm), _tile(n, tn)
    return pl.pallas_call(
        _mm_kernel,
        grid=(m // tm, n // tn),
        in_specs=[pl.BlockSpec((tm, k), lambda i, j: (i, 0)), pl.BlockSpec((k, tn), lambda i, j: (0, j))],
        out_specs=pl.BlockSpec((tm, tn), lambda i, j: (i, j)),
        out_shape=jax.ShapeDtypeStruct((m, n), out_dtype),
        compiler_params=_params("arbitrary", "arbitrary"),
        name="matmul",
    )(x, w)


def _rolling(cast, compute):
    jj = pl.program_id(0)

    @pl.when(jj == 0)
    def _():
        cast(0)

    for slot in range(2):
        @pl.when(jnp.logical_and(jj > 0, jj % 2 == slot))
        def _():
            compute(1 - slot)
            cast(slot)


def _chunk_rows(w_ref):
    ck = w_ref.shape[0]
    return pl.ds(pl.multiple_of(pl.program_id(1) * ck, ck), ck)


def _mm_ws_kernel(*refs, scaled):
    if scaled:
        x_ref, xm_ref, w_ref, s_ref, o_ref, om_ref, wa_ref, wb_ref = refs
    else:
        x_ref, xm_ref, w_ref, o_ref, om_ref, wa_ref, wb_ref = refs
    bufs = (wa_ref, wb_ref)

    def cast(slot):
        w = w_ref[...]
        if scaled:
            w = w * s_ref[...]
        bufs[slot][_chunk_rows(w_ref), :] = w.astype(jnp.bfloat16)

    def compute(slot):
        @pl.when(pl.program_id(1) == 0)
        def _():
            om_ref[...] = jnp.dot(xm_ref[...], bufs[slot][...],
                                  preferred_element_type=jnp.float32).astype(om_ref.dtype)

        o_ref[...] = jnp.dot(x_ref[...], bufs[slot][...], preferred_element_type=jnp.float32).astype(o_ref.dtype)

    _rolling(cast, compute)


def _ws_geometry(m, k, tm):
    ni = m // tm
    assert k % ni == 0 and (k // ni) % BF16_ROWS == 0, (m, k, tm)
    return ni, k // ni


def matmul_ws(x, x_meta, w_stack, layer, n, out_dtype, scale=None, tm=1024, tn=1024):
    m, k = x.shape
    tm, tn = _tile(m, tm), _tile(n, tn)
    ni, ck = _ws_geometry(m, k, tm)
    nj = n // tn
    scaled = scale is not None
    nxt = lambda jj: jnp.minimum(jj, nj - 1)
    cur = lambda jj: jnp.maximum(jj - 1, 0)
    in_specs = [pl.BlockSpec((tm, k), lambda jj, i: (jnp.where(jj == 0, 0, i), 0)),
                pl.BlockSpec(x_meta.shape, lambda jj, i: (0, 0)),
                pl.BlockSpec((None, ck, tn), lambda jj, i: (layer, i, nxt(jj)))]
    args = [x, x_meta, w_stack]
    if scaled:
        in_specs.append(pl.BlockSpec((1, tn), lambda jj, i: (0, nxt(jj))))
        args.append(scale)
    mm = x_meta.shape[0]
    return pl.pallas_call(
        functools.partial(_mm_ws_kernel, scaled=scaled),
        grid=(nj + 1, ni),
        in_specs=in_specs,
        out_specs=[pl.BlockSpec((tm, tn), lambda jj, i: (jnp.where(jj == 0, 0, i), cur(jj))),
                   pl.BlockSpec((mm, tn), lambda jj, i: (0, cur(jj)))],
        out_shape=[jax.ShapeDtypeStruct((m, n), out_dtype), jax.ShapeDtypeStruct((mm, n), out_dtype)],
        scratch_shapes=[pltpu.VMEM((k, tn), jnp.bfloat16)] * 2,
        compiler_params=_params("arbitrary", "arbitrary"),
        name="matmul_ws",
    )(*args)


def _mm_acc_kernel(x_ref, w_ref, o_ref):
    d = jnp.dot(x_ref[...], w_ref[...], preferred_element_type=jnp.float32)
    k = pl.program_id(2)

    @pl.when(k == 0)
    def _():
        o_ref[...] = d

    @pl.when(k > 0)
    def _():
        o_ref[...] += d


def matmul_ktiled(x, w, tm=512, tn=1024, tk=5504):
    m, k = x.shape
    n = w.shape[1]
    tm, tn, tk = _tile(m, tm), _tile(n, tn), _tile(k, tk)
    return pl.pallas_call(
        _mm_acc_kernel,
        grid=(m // tm, n // tn, k // tk),
        in_specs=[pl.BlockSpec((tm, tk), lambda i, j, kk: (i, kk)),
                  pl.BlockSpec((tk, tn), lambda i, j, kk: (kk, j))],
        out_specs=pl.BlockSpec((tm, tn), lambda i, j, kk: (i, j)),
        out_shape=jax.ShapeDtypeStruct((m, n), jnp.float32),
        compiler_params=_params("arbitrary", "arbitrary", "arbitrary"),
        name="matmul_ktiled",
    )(x, w)


def _ffn_up_kernel(x_ref, xm_ref, wg_ref, wu_ref, o_ref, om_ref, ga_ref, ua_ref, gb_ref, ub_ref):
    bufs = ((ga_ref, ua_ref), (gb_ref, ub_ref))

    def cast(slot):
        rows = _chunk_rows(wg_ref)
        bufs[slot][0][rows, :] = wg_ref[...].astype(jnp.bfloat16)
        bufs[slot][1][rows, :] = wu_ref[...].astype(jnp.bfloat16)

    def compute(slot):
        wg_ref_b, wu_ref_b = bufs[slot]

        def act(x):
            g = jnp.dot(x, wg_ref_b[...], preferred_element_type=jnp.float32)
            u = jnp.dot(x, wu_ref_b[...], preferred_element_type=jnp.float32)
            return ((g / (1.0 + jnp.exp(-g))) * u).astype(o_ref.dtype)

        @pl.when(pl.program_id(1) == 0)
        def _():
            om_ref[...] = act(xm_ref[...])

        o_ref[...] = act(x_ref[...])

    _rolling(cast, compute)


def ffn_up(x, x_meta, wg_stack, wu_stack, layer, tm=1024, tf=FF_TILE):
    m, k = x.shape
    d_ff = wg_stack.shape[-1]
    tm = _tile(m, tm)
    ni, ck = _ws_geometry(m, k, tm)
    nj = pl.cdiv(d_ff, tf)
    mm = x_meta.shape[0]
    nxt = lambda jj: jnp.minimum(jj, nj - 1)
    cur = lambda jj: jnp.maximum(jj - 1, 0)
    wspec = pl.BlockSpec((None, ck, tf), lambda jj, i: (layer, i, nxt(jj)))
    return pl.pallas_call(
        _ffn_up_kernel,
        grid=(nj + 1, ni),
        in_specs=[pl.BlockSpec((tm, k), lambda jj, i: (jnp.where(jj == 0, 0, i), 0)),
                  pl.BlockSpec((mm, k), lambda jj, i: (0, 0)),
                  wspec, wspec],
        out_specs=[pl.BlockSpec((tm, tf), lambda jj, i: (jnp.where(jj == 0, 0, i), cur(jj))),
                   pl.BlockSpec((mm, tf), lambda jj, i: (0, cur(jj)))],
        out_shape=[jax.ShapeDtypeStruct((m, d_ff), jnp.bfloat16),
                   jax.ShapeDtypeStruct((mm, d_ff), jnp.bfloat16)],
        scratch_shapes=[pltpu.VMEM((k, tf), jnp.bfloat16)] * 4,
        compiler_params=_params("arbitrary", "arbitrary"),
        name="ffn_up",
    )(x, x_meta, wg_stack, wu_stack)


def _rope128(r, c, sa, sb):
    return r * c + pltpu.roll(r, LANES - MLA_ROPE_DIM // 2, 1) * sa + pltpu.roll(r, MLA_ROPE_DIM // 2, 1) * sb


def _mla_latent_kernel(s_ref, gq_ref, gkv_ref, c_ref, sa_ref, sb_ref, cq_ref, ckv_ref, kr_ref):
    q0, kv0, r0 = 0, MLA_Q_RANK, MLA_Q_RANK + KV_RANK_PAD
    cq_ref[...] = _rms(s_ref[:, q0:kv0], gq_ref[...], MLA_Q_RANK).astype(cq_ref.dtype)
    ckv_ref[...] = _rms(s_ref[:, kv0:r0], gkv_ref[...], MLA_KV_RANK).astype(ckv_ref.dtype)
    kr_ref[...] = _rope128(s_ref[:, r0:r0 + LANES], c_ref[...], sa_ref[...], sb_ref[...]).astype(kr_ref.dtype)


def mla_latent(small, gq, gkv_pad, tabs, seq, tm=512):
    m = small.shape[0]
    tm = _tile(m, tm)
    nseq = seq // tm
    tab = pl.BlockSpec((tm, LANES), lambda i: (i % nseq, 0))
    return pl.pallas_call(
        _mla_latent_kernel,
        grid=(m // tm,),
        in_specs=[pl.BlockSpec((tm, EVEN_SMALL), lambda i: (i, 0)),
                  pl.BlockSpec((1, MLA_Q_RANK), lambda i: (0, 0)),
                  pl.BlockSpec((1, KV_RANK_PAD), lambda i: (0, 0)), tab, tab, tab],
        out_specs=[pl.BlockSpec((tm, MLA_Q_RANK), lambda i: (i, 0)),
                   pl.BlockSpec((tm, KV_RANK_PAD), lambda i: (i, 0)),
                   pl.BlockSpec((tm, LANES), lambda i: (i, 0))],
        out_shape=[jax.ShapeDtypeStruct((m, MLA_Q_RANK), jnp.bfloat16),
                   jax.ShapeDtypeStruct((m, KV_RANK_PAD), jnp.bfloat16),
                   jax.ShapeDtypeStruct((m, LANES), jnp.bfloat16)],
        compiler_params=_params("arbitrary"),
        name="mla_latent",
    )(small, gq.reshape(1, -1), gkv_pad.reshape(1, -1), *tabs)


def _uq_rope_kernel(x_ref, w_ref, c_ref, sa_ref, sb_ref, o_ref):
    q = jnp.dot(x_ref[...], w_ref[...], preferred_element_type=jnp.float32)
    c, sa, sb = c_ref[...], sa_ref[...], sb_ref[...]
    one = (lax.broadcasted_iota(jnp.int32, (1, LANES), 1) == MLA_MASK_LANE).astype(jnp.float32)
    for h in range(q.shape[1] // (2 * LANES)):
        n0, r0 = 2 * h * LANES, (2 * h + 1) * LANES
        o_ref[:, n0:r0] = q[:, n0:r0].astype(o_ref.dtype)
        o_ref[:, r0:r0 + LANES] = (_rope128(q[:, r0:r0 + LANES], c, sa, sb) + one).astype(o_ref.dtype)


def uq_rope(cq, w_uq_p, tabs, seq, tm=1024, tn=1024):
    m, k = cq.shape
    n = w_uq_p.shape[1]
    tm, tn = _tile(seq, tm), _tile(n, tn, 2 * LANES)
    nseq = seq // tm
    tab = pl.BlockSpec((tm, LANES), lambda i, j: (i % nseq, 0))
    return pl.pallas_call(
        _uq_rope_kernel, grid=(m // tm, n // tn),
        in_specs=[pl.BlockSpec((tm, k), lambda i, j: (i, 0)), pl.BlockSpec((k, tn), lambda i, j: (0, j)),
                  tab, tab, tab],
        out_specs=pl.BlockSpec((tm, tn), lambda i, j: (i, j)),
        out_shape=jax.ShapeDtypeStruct((m, n), jnp.bfloat16),
        compiler_params=_params("arbitrary", "arbitrary"), name="uq_rope",
    )(cq, w_uq_p, *tabs)


def _forget_cumsum_kernel(x_ref, b_ref, o_ref):
    z = x_ref[...] + b_ref[...]
    log_f = jnp.minimum(z, 0.0) - jnp.log(1.0 + jnp.exp(-jnp.abs(z)))
    n = log_f.shape[-1]
    row = lax.broadcasted_iota(jnp.int32, (LANES, LANES), 0)
    col = lax.broadcasted_iota(jnp.int32, (LANES, LANES), 1)
    upper = (row <= col).astype(jnp.float32)
    carry = jnp.zeros((log_f.shape[0], 1), jnp.float32)
    for c in range(n // LANES):
        chunk = jnp.dot(log_f[:, c * LANES:(c + 1) * LANES], upper,
                        precision=lax.Precision.HIGHEST, preferred_element_type=jnp.float32) + carry
        o_ref[:, c * LANES:(c + 1) * LANES] = chunk
        carry = chunk[:, LANES - 1:LANES]


def forget_cumsum(f_logit, b_f):
    b, h, n = f_logit.shape
    return pl.pallas_call(
        _forget_cumsum_kernel,
        grid=(b,),
        in_specs=[pl.BlockSpec((None, h, n), lambda i: (i, 0, 0)), pl.BlockSpec((h, 1), lambda i: (0, 0))],
        out_specs=pl.BlockSpec((None, h, n), lambda i: (i, 0, 0)),
        out_shape=jax.ShapeDtypeStruct((b, h, n), jnp.float32),
        compiler_params=_params("arbitrary"),
        name="forget_cumsum",
    )(f_logit, b_f.reshape(h, 1))


def _t5_tile_kernel(rb_ref, o_ref, *, offset, causal):
    h = pl.program_id(0)
    shape = o_ref.shape
    i = lax.broadcasted_iota(jnp.int32, shape, 0)
    j = lax.broadcasted_iota(jnp.int32, shape, 1)
    dist = jnp.maximum(j - i + offset, 0)
    max_exact = REL_BUCKETS // 2
    d = jnp.maximum(dist, 1).astype(jnp.float32)
    large = max_exact + (jnp.log(d / max_exact) / math.log(REL_MAX_DIST / max_exact)
                         * (REL_BUCKETS - max_exact)).astype(jnp.int32)
    large = jnp.minimum(large, REL_BUCKETS - 1)
    bucket = jnp.where(dist < max_exact, dist, large)
    far = rb_ref[h, REL_BUCKETS - 1]
    tile = jnp.zeros(shape, jnp.float32)
    for b in range(REL_BUCKETS):
        tile = jnp.where(bucket == b, (rb_ref[h, b] - far) * LOG2E, tile)
    if causal:
        tile = jnp.where(i > j, MASK_VALUE, tile)
    o_ref[...] = tile


def t5_tiles(rel_bias_t, rows, cols, offset, causal):
    h = rel_bias_t.shape[0]
    return pl.pallas_call(
        functools.partial(_t5_tile_kernel, offset=offset, causal=causal),
        grid=(h,),
        in_specs=[pl.BlockSpec(memory_space=pltpu.SMEM)],
        out_specs=pl.BlockSpec((None, rows, cols), lambda i: (i, 0, 0)),
        out_shape=jax.ShapeDtypeStruct((h, rows, cols), jnp.float32),
        compiler_params=_params("arbitrary"),
        name="t5_tiles",
    )(rel_bias_t)


def _q_aug(q):
    ones = (lax.broadcasted_iota(jnp.int32, (LANES, q.shape[0]), 0) < BIAS_SLOTS).astype(q.dtype)
    return jnp.concatenate([q.T, ones], axis=0)


def _v_aug(v):
    return jnp.concatenate([v.T, jnp.ones((BF16_ROWS, v.shape[0]), v.dtype)], axis=0)


def _k_aug(k, extra):
    return jnp.concatenate([k, extra], axis=1)


def _init_state(tq, dv):
    return (jnp.full((1, tq), MASK_VALUE, jnp.float32), jnp.zeros((dv + BF16_ROWS, tq), jnp.float32))


def _score(k_aug, q_aug):
    return jnp.dot(k_aug, q_aug, preferred_element_type=jnp.float32)


def _causal(s):
    i = lax.broadcasted_iota(jnp.int32, s.shape, 0)
    j = lax.broadcasted_iota(jnp.int32, s.shape, 1)
    return jnp.where(i > j, MASK_VALUE, s)


def _consume(s, v_aug, state):
    m, acc = state
    m_new = jnp.maximum(m, jnp.max(s, axis=0, keepdims=True))
    p = jnp.exp2(s - m_new).astype(v_aug.dtype)
    acc = jnp.exp2(m - m_new) * acc + jnp.dot(v_aug, p, preferred_element_type=jnp.float32)
    return m_new, acc


def _next_tile(qb, kb):
    wrap = kb == qb
    return jnp.where(wrap, qb + 1, qb), jnp.where(wrap, 0, kb + 1)


def _flat_sweep(nq, score, consume, bufs):
    sa, sb = bufs
    n_steps = nq * (nq + 1) // 2 - 1
    zero = jnp.int32(0)
    score(zero, zero, sa)

    def pair(tile):
        t1 = _next_tile(*tile)
        t2 = _next_tile(*t1)
        score(*t1, sb)
        consume(*tile, sa)
        score(*t2, sa)
        consume(*t1, sb)
        return t2

    def trip(_, tile):
        for _ in range(SWEEP_PAIRS_PER_TRIP):
            tile = pair(tile)
        return tile

    tile = lax.fori_loop(0, n_steps // (2 * SWEEP_PAIRS_PER_TRIP), trip, (zero, zero))
    rest = n_steps % (2 * SWEEP_PAIRS_PER_TRIP)
    for _ in range(rest // 2):
        tile = pair(tile)
    if rest % 2:
        t1 = _next_tile(*tile)
        score(*t1, sb)
        consume(*tile, sa)
        consume(*t1, sb)
    else:
        consume(*tile, sa)


def _finish(state, dv):
    _, acc = state
    return (acc[:dv] * (1.0 / acc[dv:dv + 1])).T


def _resume(first, m, acc):
    return jnp.where(first, MASK_VALUE, m), jnp.where(first, 0.0, acc)


def _single_meta_kernel(q_ref, km_ref, xm_ref, vm_ref, o_ref, *, rope_keys):
    q_aug = q_ref[...].T if rope_keys else _q_aug(q_ref[...])
    dv = vm_ref.shape[-1]
    s = _causal(_score(_k_aug(km_ref[...], xm_ref[...]), q_aug))
    state = _consume(s, _v_aug(vm_ref[...]), _init_state(q_aug.shape[1], dv))
    o_ref[...] = _finish(state, dv).astype(o_ref.dtype)


def _single_kernel(q_ref, kr_ref, xr_ref, vr_ref, km_ref, xm_ref, vm_ref, add_ref, o_ref,
                   kar_ref, var_ref, kam_ref, vam_ref, sa_ref, sb_ref, qa_ref, m_ref, acc_ref, *, rope_keys):
    nq, t = kr_ref.shape[0], kr_ref.shape[1]
    dv = vm_ref.shape[-1]
    kam_ref[...] = _k_aug(km_ref[...], xm_ref[...])
    vam_ref[...] = _v_aug(vm_ref[...])
    for blk in range(nq):
        kar_ref[blk] = _k_aug(kr_ref[blk], xr_ref[blk])
        var_ref[blk] = _v_aug(vr_ref[blk])
        q = q_ref[blk * t:(blk + 1) * t, :]
        qa_ref[blk] = q.T if rope_keys else _q_aug(q)

    def score(qb, kb, buf):
        buf[...] = _score(kar_ref[kb], qa_ref[qb])

    def consume(qb, kb, buf):
        s = buf[...] + add_ref[(kb == qb).astype(jnp.int32)]
        m_ref[qb], acc_ref[qb] = _consume(s, var_ref[kb], _resume(kb == 0, m_ref[qb], acc_ref[qb]))

    _flat_sweep(nq, score, consume, (sa_ref, sb_ref))
    s_meta = [_score(kam_ref[...], qa_ref[qb]) for qb in range(nq)]
    for qb in range(nq):
        state = _consume(s_meta[qb], vam_ref[...], (m_ref[qb], acc_ref[qb]))
        o_ref[qb * t:(qb + 1) * t, :] = _finish(state, dv).astype(o_ref.dtype)


def _diff_out(outs, lam_ref, g_ref, lam_init, dv):
    lam = lam_ref[...]
    lam_full = (jnp.exp(jnp.sum(lam[0:1] * lam[1:2], axis=-1, keepdims=True))
                - jnp.exp(jnp.sum(lam[2:3] * lam[3:4], axis=-1, keepdims=True)) + lam_init)
    return _rms(outs[0] - lam_full * outs[1], g_ref[...], dv) * (1.0 - lam_init)


def _diff_meta_kernel(q_ref, km_ref, xm_ref, vm_ref, tm_ref, lam_ref, g_ref, o_ref, *, lam_init):
    d = DIFF_HEAD_DIM
    dv = vm_ref.shape[-1]
    km, xm, v_aug = km_ref[...], xm_ref[...], _v_aug(vm_ref[...])
    outs = []
    for c in range(2):
        q_aug = _q_aug(q_ref[:, c * d:(c + 1) * d])
        s = _score(_k_aug(km[:, c * d:(c + 1) * d], xm), q_aug) + tm_ref[...]
        outs.append(_finish(_consume(s, v_aug, _init_state(q_aug.shape[1], dv)), dv))
    o_ref[...] = _diff_out(outs, lam_ref, g_ref, lam_init, dv).astype(o_ref.dtype)


def _diff_kernel(q_ref, kr_ref, xr_ref, vr_ref, km_ref, xm_ref, vm_ref, tm_ref, add_ref, lam_ref, g_ref, o_ref,
                 kar_ref, var_ref, kam_ref, vam_ref, sa_ref, sb_ref, qa_ref, m_ref, acc_ref, *, lam_init):
    nq, t = kr_ref.shape[0], kr_ref.shape[1]
    d = DIFF_HEAD_DIM
    dv = vm_ref.shape[-1]
    km, xm, xr = km_ref[...], xm_ref[...], xr_ref[...]
    vam_ref[...] = _v_aug(vm_ref[...])
    for c in range(2):
        kam_ref[c] = _k_aug(km[:, c * d:(c + 1) * d], xm)
    for blk in range(nq):
        var_ref[blk] = _v_aug(vr_ref[blk])
        for c in range(2):
            kar_ref[c, blk] = _k_aug(kr_ref[blk, :, c * d:(c + 1) * d], xr)
            qa_ref[c, blk] = _q_aug(q_ref[blk * t:(blk + 1) * t, c * d:(c + 1) * d])

    def score(qb, kb, buf):
        for c in range(2):
            buf[c] = _score(kar_ref[c, kb], qa_ref[c, qb])

    def consume(qb, kb, buf):
        add = add_ref[jnp.clip(kb - qb + 2, 0, 2)]
        for c in range(2):
            m_ref[c, qb], acc_ref[c, qb] = _consume(buf[c] + add, var_ref[kb],
                                                    _resume(kb == 0, m_ref[c, qb], acc_ref[c, qb]))

    _flat_sweep(nq, score, consume, (sa_ref, sb_ref))
    for qb in range(nq):
        outs = []
        for c in range(2):
            s = _score(kam_ref[c], qa_ref[c, qb]) + tm_ref[min(qb, 1)]
            outs.append(_finish(_consume(s, vam_ref[...], (m_ref[c, qb], acc_ref[c, qb])), dv))
        o_ref[qb * t:(qb + 1) * t, :] = _diff_out(outs, lam_ref, g_ref, lam_init, dv).astype(o_ref.dtype)


def _attn_call(kernel, name, b, heads, nq, tq, dv, in_specs, args, scratch=()):
    return pl.pallas_call(
        kernel,
        grid=(b, heads, nq),
        in_specs=in_specs,
        out_specs=pl.BlockSpec((tq, dv), lambda bi, h, qi: (bi * nq + qi, h)),
        out_shape=jax.ShapeDtypeStruct((b * nq * tq, heads * dv), jnp.bfloat16),
        scratch_shapes=list(scratch),
        compiler_params=_params("arbitrary", "arbitrary", "arbitrary"),
        name=name,
    )(*args)


def _aug_scratch(n_comp, nkb, t, dv):
    lead = (n_comp,) if n_comp > 1 else ()
    bf = jnp.bfloat16
    return [pltpu.VMEM(lead + (nkb, t, 2 * LANES), bf), pltpu.VMEM((nkb, dv + BF16_ROWS, t), bf),
            pltpu.VMEM(lead + (META_BLOCK, 2 * LANES), bf), pltpu.VMEM((dv + BF16_ROWS, META_BLOCK), bf),
            pltpu.VMEM(lead + (t, t), jnp.float32), pltpu.VMEM(lead + (t, t), jnp.float32),
            pltpu.VMEM(lead + (nkb, 2 * LANES, t), bf), pltpu.VMEM(lead + (nkb, 1, t), jnp.float32),
            pltpu.VMEM(lead + (nkb, dv + BF16_ROWS, t), jnp.float32)]


def _causal_tiles(t):
    i = lax.broadcasted_iota(jnp.int32, (t, t), 0)
    j = lax.broadcasted_iota(jnp.int32, (t, t), 1)
    return jnp.stack([jnp.zeros((t, t), jnp.float32), jnp.where(i > j, MASK_VALUE, 0.0).astype(jnp.float32)])


def _whole(shape):
    return pl.BlockSpec(shape, lambda bi, h, qi: (0,) * len(shape))


def _qspec(tq, dq, nq, col=lambda h: h):
    return pl.BlockSpec((tq, dq), lambda bi, h, qi: (bi * nq + qi, col(h)))


def _kspec(nkb, tk, d, col):
    return pl.BlockSpec((None, nkb, tk, d), lambda bi, h, qi: (bi, 0, 0, col(h)))


def _mspec(d, col):
    return pl.BlockSpec((META_BLOCK, d), lambda bi, h, qi: (0, col(h)))


def fox_attention(qkv, qkv_meta, extra, extra_meta, b, seq, t=ATTN_TILE):
    hd = FOX_HEADS
    xm_spec = pl.BlockSpec((None, META_BLOCK, LANES), lambda bi, h, qi: (h, 0, 0))
    meta_specs = [_mspec(FOX_DIM, lambda h: hd + h), xm_spec, _mspec(FOX_DIM, lambda h: 2 * hd + h)]
    meta_args = (qkv_meta, extra_meta, qkv_meta)
    if qkv is None:
        return _attn_call(functools.partial(_single_meta_kernel, rope_keys=False),
                          "fox_attention_meta", 1, hd, 1, META_BLOCK, FOX_DIM,
                          [_qspec(META_BLOCK, FOX_DIM, 1)] + meta_specs, (qkv_meta,) + meta_args)
    t = _tile(seq, t)
    nq = seq // t
    kv4 = qkv.reshape(b, nq, t, qkv.shape[-1])
    return _attn_call(
        functools.partial(_single_kernel, rope_keys=False), "fox_attention",
        b, hd, 1, seq, FOX_DIM,
        [_qspec(seq, FOX_DIM, 1), _kspec(nq, t, FOX_DIM, lambda h: hd + h),
         pl.BlockSpec((None, None, nq, t, LANES), lambda bi, h, qi: (bi, h, 0, 0, 0)),
         _kspec(nq, t, FOX_DIM, lambda h: 2 * hd + h)] + meta_specs + [_whole((2, t, t))],
        (qkv, kv4, extra.reshape(b, hd, nq, t, LANES), kv4) + meta_args + (_causal_tiles(t),),
        _aug_scratch(1, nq, t, FOX_DIM))


def mla_attention(q, kv, kr, q_meta, kv_meta, kr_meta, b, seq, t=ATTN_TILE):
    hd = MLA_HEADS
    meta_specs = [_mspec(LANES, lambda h: 2 * h), _mspec(LANES, lambda h: 0), _mspec(LANES, lambda h: 2 * h + 1)]
    meta_args = (kv_meta, kr_meta, kv_meta)
    if q is None:
        return _attn_call(functools.partial(_single_meta_kernel, rope_keys=True),
                          "mla_attention_meta", 1, hd, 1, META_BLOCK, MLA_V_DIM,
                          [_qspec(META_BLOCK, 2 * LANES, 1)] + meta_specs, (q_meta,) + meta_args)
    t = _tile(seq, t)
    nq = seq // t
    kv4 = kv.reshape(b, nq, t, kv.shape[-1])
    kr4 = kr.reshape(b, nq, t, kr.shape[-1])
    return _attn_call(
        functools.partial(_single_kernel, rope_keys=True), "mla_attention",
        b, hd, 1, seq, MLA_V_DIM,
        [_qspec(seq, 2 * LANES, 1), _kspec(nq, t, LANES, lambda h: 2 * h),
         _kspec(nq, t, LANES, lambda h: 0), _kspec(nq, t, LANES, lambda h: 2 * h + 1)] + meta_specs
        + [_whole((2, t, t))],
        (q, kv4, kr4, kv4) + meta_args + (_causal_tiles(t),), _aug_scratch(1, nq, t, MLA_V_DIM))


def diff_attention(qkv, qkv_meta, extra, extra_meta, tiles, lam, subln, lam_init, b, seq, t=ATTN_TILE):
    hd = DIFF_HEADS
    dv = 2 * DIFF_HEAD_DIM
    tile_meta_only, tile_meta, tile_real = tiles
    lam_spec = pl.BlockSpec((4, DIFF_HEAD_DIM), lambda bi, h, qi: (0, 0))
    g_spec = pl.BlockSpec((1, dv), lambda bi, h, qi: (0, 0))
    kcol, vcol = (lambda h: hd + h), (lambda h: 2 * hd + h)
    per_head = lambda r, c: pl.BlockSpec((None, r, c), lambda bi, h, qi: (h, 0, 0))
    meta_specs = [_mspec(dv, kcol), per_head(META_BLOCK, LANES), _mspec(dv, vcol)]
    meta_args = (qkv_meta, extra_meta, qkv_meta)
    tail = (lam, subln.reshape(1, dv))
    if qkv is None:
        return _attn_call(
            functools.partial(_diff_meta_kernel, lam_init=lam_init),
            "diff_attention_meta", 1, hd, 1, META_BLOCK, dv,
            [_qspec(META_BLOCK, dv, 1)] + meta_specs + [per_head(META_BLOCK, META_BLOCK), lam_spec, g_spec],
            (qkv_meta,) + meta_args + (tile_meta_only,) + tail)
    t = _tile(seq, t)
    nq = seq // t
    kv4 = qkv.reshape(b, nq, t, qkv.shape[-1])
    return _attn_call(
        functools.partial(_diff_kernel, lam_init=lam_init),
        "diff_attention", b, hd, 1, seq, dv,
        [_qspec(seq, dv, 1), _kspec(nq, t, dv, kcol), per_head(t, LANES), _kspec(nq, t, dv, vcol)] + meta_specs
        + [pl.BlockSpec((None, 2, META_BLOCK, t), lambda bi, h, qi: (h, 0, 0, 0)),
           pl.BlockSpec((None, 3, t, t), lambda bi, h, qi: (h, 0, 0, 0)), lam_spec, g_spec],
        (qkv, kv4, extra, kv4) + meta_args + (tile_meta, tile_real) + tail,
        _aug_scratch(2, nq, t, dv))


def _front_pad(x):
    return jnp.pad(x, ((META_BLOCK - x.shape[0], 0), (0, 0)))


def _split3(x):
    def head(v):
        bits = lax.bitcast_convert_type(v, jnp.uint32) & jnp.uint32(0xFFFF0000)
        return lax.bitcast_convert_type(bits, jnp.float32)

    hi = head(x)
    mid = head(x - hi)
    lo = head(x - hi - mid)
    return tuple(p.astype(jnp.bfloat16) for p in (hi, mid, lo))


def _extra_lanes(x, mask=None):
    pieces = list(_split3(x)) + [jnp.zeros(x.shape, jnp.bfloat16) if mask is None else mask.astype(jnp.bfloat16)]
    out = jnp.stack(pieces, axis=-1)
    return jnp.pad(out, [(0, 0)] * x.ndim + [(0, LANES - BIAS_SLOTS)])


def _pad_mask(heads):
    m = jnp.where(jnp.arange(META_BLOCK) < META_BLOCK - N_META, MASK_VALUE, 0.0).astype(jnp.float32)
    return jnp.broadcast_to(m, (heads, META_BLOCK))


def _rope_tables(n_pos):
    inv = ROPE_THETA ** (-jnp.arange(0, MLA_ROPE_DIM, 2, dtype=jnp.float32) / MLA_ROPE_DIM)
    ang = jnp.arange(n_pos, dtype=jnp.float32)[:, None] * inv[None, :]
    cos, sin = jnp.cos(ang), jnp.sin(ang)
    z32 = jnp.zeros_like(cos)
    z64 = jnp.zeros((n_pos, LANES - MLA_ROPE_DIM), jnp.float32)
    c = jnp.concatenate([cos, cos, z64], axis=1)
    sa = jnp.concatenate([-sin, z32, z64], axis=1)
    sb = jnp.concatenate([z32, sin, z64], axis=1)
    return c, sa, sb


def _col_scale(n, n_scaled, value):
    return jnp.where(jnp.arange(n) < n_scaled, value, 1.0).astype(jnp.float32)[None, :]


def _even_weights(w_in, w_uq, w_ukv, gkv):
    o = [0]
    for s in (FOX_HEADS * FOX_DIM,) * 3 + (FOX_HEADS, MLA_Q_RANK, MLA_KV_RANK, MLA_ROPE_DIM):
        o.append(o[-1] + s)
    bf = jnp.bfloat16
    mla_scale = (MLA_NOPE_DIM + MLA_ROPE_DIM) ** -0.5 * LOG2E
    w_qkv = jnp.concatenate([w_in[:, :o[1]] * (FOX_DIM ** -0.5 * LOG2E), w_in[:, o[1]:o[3]]], axis=1).astype(bf)
    pad = lambda w, n: jnp.pad(w, ((0, 0), (0, n - w.shape[1])))
    w_small = jnp.concatenate([w_in[:, o[4]:o[5]], pad(w_in[:, o[5]:o[6]], KV_RANK_PAD),
                               pad(w_in[:, o[6]:o[7]], LANES), pad(w_in[:, o[3]:o[4]], LANES)],
                              axis=1).astype(bf)
    hq = (w_uq * mla_scale).reshape(MLA_Q_RANK, MLA_HEADS, MLA_NOPE_DIM + MLA_ROPE_DIM)
    hq = jnp.pad(hq, ((0, 0), (0, 0), (0, 2 * LANES - hq.shape[-1])))
    w_uq_p = hq.reshape(MLA_Q_RANK, MLA_HEADS * 2 * LANES).astype(bf)
    w_ukv_p = jnp.pad(w_ukv, ((0, KV_RANK_PAD - MLA_KV_RANK), (0, 0))).astype(bf)
    gkv_p = jnp.pad(gkv, (0, KV_RANK_PAD - MLA_KV_RANK))
    return w_qkv, w_small, w_uq_p, w_ukv_p, gkv_p


def _even_mixer(hn, hn_m, b, seq, i, w_in, b_f, gq, gkv, w_uq, w_ukv, ev_w_out, tabs, tabs_m):
    w_qkv, w_small, w_uq_p, w_ukv_p, gkv_p = _even_weights(w_in, w_uq, w_ukv, gkv)
    f0 = MLA_Q_RANK + KV_RANK_PAD + LANES
    qkv, qkv_m = matmul(hn, w_qkv, jnp.bfloat16), matmul(hn_m, w_qkv, jnp.bfloat16)

    def stream(x, tab, n_pos):
        small = matmul(x, w_small, jnp.float32, tn=EVEN_SMALL // 2)
        cq, ckv, kr = mla_latent(small, gq, gkv_p, tab, n_pos)
        q = uq_rope(cq, w_uq_p, tab, n_pos)
        kv = matmul(ckv, w_ukv_p, jnp.bfloat16)
        return small[:, f0:f0 + FOX_HEADS], q, kv, kr

    fl, q, kv, kr = stream(hn, tabs, seq)
    fl_m, q_m, kv_m, kr_m = stream(hn_m, tabs_m, N_META)

    cum = forget_cumsum(fl.reshape(b, seq, FOX_HEADS).transpose(0, 2, 1), b_f)
    fl_m = jnp.pad(fl_m.T, ((0, 0), (0, LANES - N_META)))[None]
    cum_m = forget_cumsum(fl_m, b_f)[0, :, :N_META]
    mask = _pad_mask(FOX_HEADS)
    front = lambda x: jnp.pad(x, ((0, 0), (META_BLOCK - N_META, 0)))
    extra = _extra_lanes(-LOG2E * cum)
    extra_meta = _extra_lanes(front(LOG2E * (cum_m[:, -1:] - cum_m)), mask)
    extra_meta_only = _extra_lanes(front(-LOG2E * cum_m), mask)

    qkv_mp, q_mp, kv_mp, kr_mp = map(_front_pad, (qkv_m, q_m, kv_m, kr_m))
    kr_mp = kr_mp.at[:META_BLOCK - N_META, MLA_MASK_LANE].set(MASK_VALUE)
    o_f = fox_attention(qkv, qkv_mp, extra, extra_meta, b, seq)
    o_f_m = fox_attention(None, qkv_mp, None, extra_meta_only, b, seq)
    o_m = mla_attention(q, kv, kr, q_mp, kv_mp, kr_mp, b, seq)
    o_m_m = mla_attention(None, None, None, q_mp, kv_mp, kr_mp, b, seq)
    o = jnp.concatenate([o_f, o_m], axis=1)
    o_meta = jnp.concatenate([o_f_m, o_m_m], axis=1)[-N_META:]
    return matmul_ws(o, o_meta, ev_w_out, i, D_MODEL, jnp.float32)


def _diff_mixer(hn, hn_m, b, seq, i, od_w_in, lam, subln, od_w_out, bias, lam_init):
    n_q = DIFF_HEADS * 2 * DIFF_HEAD_DIM
    extra, extra_meta, tiles = bias
    qkv, qkv_m = matmul_ws(hn, hn_m, od_w_in, i, 3 * n_q, jnp.bfloat16,
                           _col_scale(3 * n_q, n_q, DIFF_HEAD_DIM ** -0.5 * LOG2E))
    qkv_mp = _front_pad(qkv_m)
    o = diff_attention(qkv, qkv_mp, extra, extra_meta, tiles, lam, subln, lam_init, b, seq)
    o_meta = diff_attention(None, qkv_mp, None, extra_meta, tiles, lam, subln, lam_init, b, seq)
    return matmul_ws(o, o_meta[-N_META:], od_w_out, i, D_MODEL, jnp.float32)


def _diff_bias(rel_bias, t):
    rb_t = rel_bias.astype(jnp.float32).T
    far = LOG2E * rb_t[:, REL_BUCKETS - 1]
    extra = _extra_lanes(jnp.broadcast_to(far[:, None], (DIFF_HEADS, t)))
    extra_meta = _extra_lanes(jnp.broadcast_to(far[:, None], (DIFF_HEADS, META_BLOCK)), _pad_mask(DIFF_HEADS))
    tile_meta0 = t5_tiles(rb_t, META_BLOCK, t, META_BLOCK, False)
    tile_sub, tile_diag = t5_tiles(rb_t, t, t, t, False), t5_tiles(rb_t, t, t, 0, True)
    tiles = (t5_tiles(rb_t, META_BLOCK, META_BLOCK, 0, True),
             jnp.stack([tile_meta0, jnp.zeros_like(tile_meta0)], axis=1),
             jnp.stack([jnp.zeros_like(tile_sub), tile_sub, tile_diag], axis=1))
    return extra, extra_meta, tiles


def kernel(x, meta_tokens, rel_bias, ev_w_in, ev_b_f, ev_q_norm, ev_kv_norm, ev_w_uq, ev_w_ukv, ev_w_out,
           od_w_in, od_lambda, od_subln, od_w_out, norm_g, ffn_w_gate, ffn_w_up, ffn_w_down):
    b, seq, d = x.shape
    hs = x.reshape(b * seq, d)
    hs_m = meta_tokens.astype(x.dtype)

    c, sa, sb = _rope_tables(N_META + seq)
    tabs_m = tuple(tb[:N_META] for tb in (c, sa, sb))
    tabs = tuple(tb[N_META:] for tb in (c, sa, sb))
    bias = _diff_bias(rel_bias, _tile(seq, ATTN_TILE))

    hn = rmsnorm_bf16(hs, norm_g[0, 0])
    hn_m = rmsnorm_bf16(hs_m, norm_g[0, 0])
    for layer in range(DEPTH):
        g = norm_g[layer]
        i = layer // 2
        if layer % 2 == 0:
            m, m_m = _even_mixer(hn, hn_m, b, seq, i, ev_w_in[i], ev_b_f[i], ev_q_norm[i], ev_kv_norm[i],
                                 ev_w_uq[i], ev_w_ukv[i], ev_w_out, tabs, tabs_m)
        else:
            lam_init = 0.8 - 0.6 * math.exp(-0.3 * layer)
            m, m_m = _diff_mixer(hn, hn_m, b, seq, i, od_w_in, od_lambda[i], od_subln[i], od_w_out,
                                 bias, lam_init)
        hs, hn = resid_norm(hs, m, g[1], g[2])
        hs_m, hn_m = resid_norm(hs_m, m_m, g[1], g[2])
        h, h_m = ffn_up(hn, hn_m, ffn_w_gate, ffn_w_up, layer)
        wd = ffn_w_down[layer].astype(jnp.bfloat16)
        f, f_m = matmul_ktiled(h, wd), matmul_ktiled(h_m, wd)
        g_next = norm_g[layer + 1, 0] if layer + 1 < DEPTH else None
        hs, hn = resid_norm(hs, f, g[3], g_next)
        if g_next is not None:
            hs_m, hn_m = resid_norm(hs_m, f_m, g[3], g_next)
    return hs.reshape(b, seq, d)
```

```python
import functools
import math

import jax
import jax.numpy as jnp
from jax import lax
from jax.experimental import pallas as pl
from jax.experimental.pallas import tpu as pltpu

D_MODEL = 4096
DEPTH = 4
N_META = 16
EPS = 1e-6
FOX_HEADS = 16
FOX_DIM = 128
MLA_HEADS = 16
MLA_Q_RANK = 896
MLA_KV_RANK = 320
MLA_NOPE_DIM = 128
MLA_ROPE_DIM = 64
MLA_V_DIM = 128
ROPE_THETA = 10000.0
DIFF_HEAD_DIM = 128
DIFF_HEADS = D_MODEL // (2 * DIFF_HEAD_DIM)
REL_BUCKETS = 32
REL_MAX_DIST = 128
D_FF = ((8 * D_MODEL + 3 * 256 - 1) // (3 * 256)) * 256

LANES = 128
BF16_ROWS = 16
META_BLOCK = LANES
MASK_VALUE = -1e30
VMEM_LIMIT_BYTES = 56 * 1024 * 1024
LOG2E = math.log2(math.e)

FF_TILE = 512
KV_RANK_PAD = 384
EVEN_SMALL = MLA_Q_RANK + KV_RANK_PAD + 2 * LANES
ATTN_TILE = 512
BIAS_SLOTS = 4
MLA_MASK_LANE = MLA_ROPE_DIM
SWEEP_PAIRS_PER_TRIP = 6


def _params(*sem):
    return pltpu.CompilerParams(dimension_semantics=sem, vmem_limit_bytes=VMEM_LIMIT_BYTES)


def _tile(n, t, unit=LANES):
    if n <= t:
        return n
    return max(c for c in range(unit, t + 1, unit) if n % c == 0)


def _rms(x, g, n):
    ms = jnp.sum(x * x, axis=-1, keepdims=True) / n
    return x * lax.rsqrt(ms + EPS) * g


def _norm_kernel(x_ref, g_ref, o_ref):
    o_ref[...] = _rms(x_ref[...], g_ref[...], x_ref.shape[-1]).astype(o_ref.dtype)


def rmsnorm_bf16(x, g, tm=256):
    m, d = x.shape
    tm = _tile(m, tm)
    return pl.pallas_call(
        _norm_kernel,
        grid=(m // tm,),
        in_specs=[pl.BlockSpec((tm, d), lambda i: (i, 0)), pl.BlockSpec((1, d), lambda i: (0, 0))],
        out_specs=pl.BlockSpec((tm, d), lambda i: (i, 0)),
        out_shape=jax.ShapeDtypeStruct((m, d), jnp.bfloat16),
        compiler_params=_params("arbitrary"),
        name="rmsnorm",
    )(x, g.reshape(1, d))


def _resid_norm_kernel(hs_ref, m_ref, g1_ref, g2_ref, hs_out_ref, hn_out_ref):
    d = hs_ref.shape[-1]
    hs = hs_ref[...] + _rms(m_ref[...], g1_ref[...], d)
    hs_out_ref[...] = hs
    hn_out_ref[...] = _rms(hs, g2_ref[...], d).astype(hn_out_ref.dtype)


def _resid_kernel(hs_ref, m_ref, g1_ref, hs_out_ref):
    hs_out_ref[...] = hs_ref[...] + _rms(m_ref[...], g1_ref[...], hs_ref.shape[-1])


def resid_norm(hs, m, g1, g2, tm=256):
    rows, d = hs.shape
    tm = _tile(rows, tm)
    row = pl.BlockSpec((tm, d), lambda i: (i, 0))
    vec = pl.BlockSpec((1, d), lambda i: (0, 0))
    if g2 is None:
        return pl.pallas_call(
            _resid_kernel, grid=(rows // tm,), in_specs=[row, row, vec], out_specs=row,
            out_shape=jax.ShapeDtypeStruct((rows, d), jnp.float32),
            compiler_params=_params("arbitrary"), name="resid",
        )(hs, m, g1.reshape(1, d)), None
    return pl.pallas_call(
        _resid_norm_kernel, grid=(rows // tm,), in_specs=[row, row, vec, vec], out_specs=[row, row],
        out_shape=[jax.ShapeDtypeStruct((rows, d), jnp.float32),
                   jax.ShapeDtypeStruct((rows, d), jnp.bfloat16)],
        compiler_params=_params("arbitrary"), name="resid_norm",
    )(hs, m, g1.reshape(1, d), g2.reshape(1, d))


def _mm_kernel(x_ref, w_ref, o_ref):
    o_ref[...] = jnp.dot(x_ref[...], w_ref[...], preferred_element_type=jnp.float32).astype(o_ref.dtype)


def matmul(x, w, out_dtype, tm=1024, tn=1024):
    m, k = x.shape
    n = w.shape[1]
    tm, tn = _tile(m, tm), _tile(n, tn)
    return pl.pallas_call(
        _mm_kernel,
        grid=(m // tm, n // tn),
        in_specs=[pl.BlockSpec((tm, k), lambda i, j: (i, 0)), pl.BlockSpec((k, tn), lambda i, j: (0, j))],
        out_specs=pl.BlockSpec((tm, tn), lambda i, j: (i, j)),
        out_shape=jax.ShapeDtypeStruct((m, n), out_dtype),
        compiler_params=_params("arbitrary", "arbitrary"),
        name="matmul",
    )(x, w)


def _rolling(cast, compute):
    jj = pl.program_id(0)

    @pl.when(jj == 0)
    def _():
        cast(0)

    for slot in range(2):
        @pl.when(jnp.logical_and(jj > 0, jj % 2 == slot))
        def _():
            compute(1 - slot)
            cast(slot)


def _chunk_rows(w_ref):
    ck = w_ref.shape[0]
    return pl.ds(pl.multiple_of(pl.program_id(1) * ck, ck), ck)


def _mm_ws_kernel(*refs, scaled):
    if scaled:
        x_ref, xm_ref, w_ref, s_ref, o_ref, om_ref, wa_ref, wb_ref = refs
    else:
        x_ref, xm_ref, w_ref, o_ref, om_ref, wa_ref, wb_ref = refs
    bufs = (wa_ref, wb_ref)

    def cast(slot):
        w = w_ref[...]
        if scaled:
            w = w * s_ref[...]
        bufs[slot][_chunk_rows(w_ref), :] = w.astype(jnp.bfloat16)

    def compute(slot):
        @pl.when(pl.program_id(1) == 0)
        def _():
            om_ref[...] = jnp.dot(xm_ref[...], bufs[slot][...],
                                  preferred_element_type=jnp.float32).astype(om_ref.dtype)

        o_ref[...] = jnp.dot(x_ref[...], bufs[slot][...], preferred_element_type=jnp.float32).astype(o_ref.dtype)

    _rolling(cast, compute)


def _ws_geometry(m, k, tm):
    ni = m // tm
    assert k % ni == 0 and (k // ni) % BF16_ROWS == 0, (m, k, tm)
    return ni, k // ni


def matmul_ws(x, x_meta, w_stack, layer, n, out_dtype, scale=None, tm=1024, tn=1024):
    m, k = x.shape
    tm, tn = _tile(m, tm), _tile(n, tn)
    ni, ck = _ws_geometry(m, k, tm)
    nj = n // tn
    scaled = scale is not None
    nxt = lambda jj: jnp.minimum(jj, nj - 1)
    cur = lambda jj: jnp.maximum(jj - 1, 0)
    in_specs = [pl.BlockSpec((tm, k), lambda jj, i: (jnp.where(jj == 0, 0, i), 0)),
                pl.BlockSpec(x_meta.shape, lambda jj, i: (0, 0)),
                pl.BlockSpec((None, ck, tn), lambda jj, i: (layer, i, nxt(jj)))]
    args = [x, x_meta, w_stack]
    if scaled:
        in_specs.append(pl.BlockSpec((1, tn), lambda jj, i: (0, nxt(jj))))
        args.append(scale)
    mm = x_meta.shape[0]
    return pl.pallas_call(
        functools.partial(_mm_ws_kernel, scaled=scaled),
        grid=(nj + 1, ni),
        in_specs=in_specs,
        out_specs=[pl.BlockSpec((tm, tn), lambda jj, i: (jnp.where(jj == 0, 0, i), cur(jj))),
                   pl.BlockSpec((mm, tn), lambda jj, i: (0, cur(jj)))],
        out_shape=[jax.ShapeDtypeStruct((m, n), out_dtype), jax.ShapeDtypeStruct((mm, n), out_dtype)],
        scratch_shapes=[pltpu.VMEM((k, tn), jnp.bfloat16)] * 2,
        compiler_params=_params("arbitrary", "arbitrary"),
        name="matmul_ws",
    )(*args)


def _mm_acc_kernel(x_ref, w_ref, o_ref):
    d = jnp.dot(x_ref[...], w_ref[...], preferred_element_type=jnp.float32)
    k = pl.program_id(2)

    @pl.when(k == 0)
    def _():
        o_ref[...] = d

    @pl.when(k > 0)
    def _():
        o_ref[...] += d


def matmul_ktiled(x, w, tm=1024, tn=512, tk=5504):
    m, k = x.shape
    n = w.shape[1]
    tm, tn, tk = _tile(m, tm), _tile(n, tn), _tile(k, tk)
    return pl.pallas_call(
        _mm_acc_kernel,
        grid=(m // tm, n // tn, k // tk),
        in_specs=[pl.BlockSpec((tm, tk), lambda i, j, kk: (i, kk)),
                  pl.BlockSpec((tk, tn), lambda i, j, kk: (kk, j))],
        out_specs=pl.BlockSpec((tm, tn), lambda i, j, kk: (i, j)),
        out_shape=jax.ShapeDtypeStruct((m, n), jnp.float32),
        compiler_params=_params("arbitrary", "arbitrary", "arbitrary"),
        name="matmul_ktiled",
    )(x, w)


def _ffn_up_kernel(x_ref, xm_ref, wg_ref, wu_ref, o_ref, om_ref, ga_ref, ua_ref, gb_ref, ub_ref):
    bufs = ((ga_ref, ua_ref), (gb_ref, ub_ref))

    def cast(slot):
        rows = _chunk_rows(wg_ref)
        bufs[slot][0][rows, :] = wg_ref[...].astype(jnp.bfloat16)
        bufs[slot][1][rows, :] = wu_ref[...].astype(jnp.bfloat16)

    def compute(slot):
        wg_ref_b, wu_ref_b = bufs[slot]

        def act(x):
            g = jnp.dot(x, wg_ref_b[...], preferred_element_type=jnp.float32)
            u = jnp.dot(x, wu_ref_b[...], preferred_element_type=jnp.float32)
            return ((g / (1.0 + jnp.exp(-g))) * u).astype(o_ref.dtype)

        @pl.when(pl.program_id(1) == 0)
        def _():
            om_ref[...] = act(xm_ref[...])

        o_ref[...] = act(x_ref[...])

    _rolling(cast, compute)


def ffn_up(x, x_meta, wg_stack, wu_stack, layer, tm=1024, tf=FF_TILE):
    m, k = x.shape
    d_ff = wg_stack.shape[-1]
    tm = _tile(m, tm)
    ni, ck = _ws_geometry(m, k, tm)
    nj = pl.cdiv(d_ff, tf)
    mm = x_meta.shape[0]
    nxt = lambda jj: jnp.minimum(jj, nj - 1)
    cur = lambda jj: jnp.maximum(jj - 1, 0)
    wspec = pl.BlockSpec((None, ck, tf), lambda jj, i: (layer, i, nxt(jj)))
    return pl.pallas_call(
        _ffn_up_kernel,
        grid=(nj + 1, ni),
        in_specs=[pl.BlockSpec((tm, k), lambda jj, i: (jnp.where(jj == 0, 0, i), 0)),
                  pl.BlockSpec((mm, k), lambda jj, i: (0, 0)),
                  wspec, wspec],
        out_specs=[pl.BlockSpec((tm, tf), lambda jj, i: (jnp.where(jj == 0, 0, i), cur(jj))),
                   pl.BlockSpec((mm, tf), lambda jj, i: (0, cur(jj)))],
        out_shape=[jax.ShapeDtypeStruct((m, d_ff), jnp.bfloat16),
                   jax.ShapeDtypeStruct((mm, d_ff), jnp.bfloat16)],
        scratch_shapes=[pltpu.VMEM((k, tf), jnp.bfloat16)] * 4,
        compiler_params=_params("arbitrary", "arbitrary"),
        name="ffn_up",
    )(x, x_meta, wg_stack, wu_stack)


def _rope128(r, c, sa, sb):
    return r * c + pltpu.roll(r, LANES - MLA_ROPE_DIM // 2, 1) * sa + pltpu.roll(r, MLA_ROPE_DIM // 2, 1) * sb


def _mla_latent_kernel(s_ref, gq_ref, gkv_ref, c_ref, sa_ref, sb_ref, cq_ref, ckv_ref, kr_ref):
    q0, kv0, r0 = 0, MLA_Q_RANK, MLA_Q_RANK + KV_RANK_PAD
    cq_ref[...] = _rms(s_ref[:, q0:kv0], gq_ref[...], MLA_Q_RANK).astype(cq_ref.dtype)
    ckv_ref[...] = _rms(s_ref[:, kv0:r0], gkv_ref[...], MLA_KV_RANK).astype(ckv_ref.dtype)
    kr_ref[...] = _rope128(s_ref[:, r0:r0 + LANES], c_ref[...], sa_ref[...], sb_ref[...]).astype(kr_ref.dtype)


def mla_latent(small, gq, gkv_pad, tabs, seq, tm=512):
    m = small.shape[0]
    tm = _tile(m, tm)
    nseq = seq // tm
    tab = pl.BlockSpec((tm, LANES), lambda i: (i % nseq, 0))
    return pl.pallas_call(
        _mla_latent_kernel,
        grid=(m // tm,),
        in_specs=[pl.BlockSpec((tm, EVEN_SMALL), lambda i: (i, 0)),
                  pl.BlockSpec((1, MLA_Q_RANK), lambda i: (0, 0)),
                  pl.BlockSpec((1, KV_RANK_PAD), lambda i: (0, 0)), tab, tab, tab],
        out_specs=[pl.BlockSpec((tm, MLA_Q_RANK), lambda i: (i, 0)),
                   pl.BlockSpec((tm, KV_RANK_PAD), lambda i: (i, 0)),
                   pl.BlockSpec((tm, LANES), lambda i: (i, 0))],
        out_shape=[jax.ShapeDtypeStruct((m, MLA_Q_RANK), jnp.bfloat16),
                   jax.ShapeDtypeStruct((m, KV_RANK_PAD), jnp.bfloat16),
                   jax.ShapeDtypeStruct((m, LANES), jnp.bfloat16)],
        compiler_params=_params("arbitrary"),
        name="mla_latent",
    )(small, gq.reshape(1, -1), gkv_pad.reshape(1, -1), *tabs)


def _uq_rope_kernel(x_ref, w_ref, c_ref, sa_ref, sb_ref, o_ref):
    q = jnp.dot(x_ref[...], w_ref[...], preferred_element_type=jnp.float32)
    c, sa, sb = c_ref[...], sa_ref[...], sb_ref[...]
    one = (lax.broadcasted_iota(jnp.int32, (1, LANES), 1) == MLA_MASK_LANE).astype(jnp.float32)
    for h in range(q.shape[1] // (2 * LANES)):
        n0, r0 = 2 * h * LANES, (2 * h + 1) * LANES
        o_ref[:, n0:r0] = q[:, n0:r0].astype(o_ref.dtype)
        o_ref[:, r0:r0 + LANES] = (_rope128(q[:, r0:r0 + LANES], c, sa, sb) + one).astype(o_ref.dtype)


def uq_rope(cq, w_uq_p, tabs, seq, tm=1024, tn=1024):
    m, k = cq.shape
    n = w_uq_p.shape[1]
    tm, tn = _tile(seq, tm), _tile(n, tn, 2 * LANES)
    nseq = seq // tm
    tab = pl.BlockSpec((tm, LANES), lambda i, j: (i % nseq, 0))
    return pl.pallas_call(
        _uq_rope_kernel, grid=(m // tm, n // tn),
        in_specs=[pl.BlockSpec((tm, k), lambda i, j: (i, 0)), pl.BlockSpec((k, tn), lambda i, j: (0, j)),
                  tab, tab, tab],
        out_specs=pl.BlockSpec((tm, tn), lambda i, j: (i, j)),
        out_shape=jax.ShapeDtypeStruct((m, n), jnp.bfloat16),
        compiler_params=_params("arbitrary", "arbitrary"), name="uq_rope",
    )(cq, w_uq_p, *tabs)


def _forget_cumsum_kernel(x_ref, b_ref, o_ref):
    z = x_ref[...] + b_ref[...]
    log_f = jnp.minimum(z, 0.0) - jnp.log(1.0 + jnp.exp(-jnp.abs(z)))
    n = log_f.shape[-1]
    row = lax.broadcasted_iota(jnp.int32, (LANES, LANES), 0)
    col = lax.broadcasted_iota(jnp.int32, (LANES, LANES), 1)
    upper = (row <= col).astype(jnp.float32)
    carry = jnp.zeros((log_f.shape[0], 1), jnp.float32)
    for c in range(n // LANES):
        chunk = jnp.dot(log_f[:, c * LANES:(c + 1) * LANES], upper,
                        precision=lax.Precision.HIGHEST, preferred_element_type=jnp.float32) + carry
        o_ref[:, c * LANES:(c + 1) * LANES] = chunk
        carry = chunk[:, LANES - 1:LANES]


def forget_cumsum(f_logit, b_f):
    b, h, n = f_logit.shape
    return pl.pallas_call(
        _forget_cumsum_kernel,
        grid=(b,),
        in_specs=[pl.BlockSpec((None, h, n), lambda i: (i, 0, 0)), pl.BlockSpec((h, 1), lambda i: (0, 0))],
        out_specs=pl.BlockSpec((None, h, n), lambda i: (i, 0, 0)),
        out_shape=jax.ShapeDtypeStruct((b, h, n), jnp.float32),
        compiler_params=_params("arbitrary"),
        name="forget_cumsum",
    )(f_logit, b_f.reshape(h, 1))


def _t5_tile_kernel(rb_ref, o_ref, *, offset, causal):
    h = pl.program_id(0)
    shape = o_ref.shape
    i = lax.broadcasted_iota(jnp.int32, shape, 0)
    j = lax.broadcasted_iota(jnp.int32, shape, 1)
    dist = jnp.maximum(j - i + offset, 0)
    max_exact = REL_BUCKETS // 2
    d = jnp.maximum(dist, 1).astype(jnp.float32)
    large = max_exact + (jnp.log(d / max_exact) / math.log(REL_MAX_DIST / max_exact)
                         * (REL_BUCKETS - max_exact)).astype(jnp.int32)
    large = jnp.minimum(large, REL_BUCKETS - 1)
    bucket = jnp.where(dist < max_exact, dist, large)
    far = rb_ref[h, REL_BUCKETS - 1]
    tile = jnp.zeros(shape, jnp.float32)
    for b in range(REL_BUCKETS):
        tile = jnp.where(bucket == b, (rb_ref[h, b] - far) * LOG2E, tile)
    if causal:
        tile = jnp.where(i > j, MASK_VALUE, tile)
    o_ref[...] = tile


def t5_tiles(rel_bias_t, rows, cols, offset, causal):
    h = rel_bias_t.shape[0]
    return pl.pallas_call(
        functools.partial(_t5_tile_kernel, offset=offset, causal=causal),
        grid=(h,),
        in_specs=[pl.BlockSpec(memory_space=pltpu.SMEM)],
        out_specs=pl.BlockSpec((None, rows, cols), lambda i: (i, 0, 0)),
        out_shape=jax.ShapeDtypeStruct((h, rows, cols), jnp.float32),
        compiler_params=_params("arbitrary"),
        name="t5_tiles",
    )(rel_bias_t)


def _q_aug(q):
    ones = (lax.broadcasted_iota(jnp.int32, (LANES, q.shape[0]), 0) < BIAS_SLOTS).astype(q.dtype)
    return jnp.concatenate([q.T, ones], axis=0)


def _v_aug(v):
    return jnp.concatenate([v.T, jnp.ones((BF16_ROWS, v.shape[0]), v.dtype)], axis=0)


def _k_aug(k, extra):
    return jnp.concatenate([k, extra], axis=1)


def _init_state(tq, dv):
    return (jnp.full((1, tq), MASK_VALUE, jnp.float32), jnp.zeros((dv + BF16_ROWS, tq), jnp.float32))


def _score(k_aug, q_aug):
    return jnp.dot(k_aug, q_aug, preferred_element_type=jnp.float32)


def _causal(s):
    i = lax.broadcasted_iota(jnp.int32, s.shape, 0)
    j = lax.broadcasted_iota(jnp.int32, s.shape, 1)
    return jnp.where(i > j, MASK_VALUE, s)


def _consume(s, v_aug, state):
    m, acc = state
    m_new = jnp.maximum(m, jnp.max(s, axis=0, keepdims=True))
    p = jnp.exp2(s - m_new).astype(v_aug.dtype)
    acc = jnp.exp2(m - m_new) * acc + jnp.dot(v_aug, p, preferred_element_type=jnp.float32)
    return m_new, acc


def _next_tile(qb, kb):
    wrap = kb == qb
    return jnp.where(wrap, qb + 1, qb), jnp.where(wrap, 0, kb + 1)


def _flat_sweep(nq, score, consume, bufs):
    sa, sb = bufs
    n_steps = nq * (nq + 1) // 2 - 1
    zero = jnp.int32(0)
    score(zero, zero, sa)

    def pair(tile):
        t1 = _next_tile(*tile)
        t2 = _next_tile(*t1)
        score(*t1, sb)
        consume(*tile, sa)
        score(*t2, sa)
        consume(*t1, sb)
        return t2

    def trip(_, tile):
        for _ in range(SWEEP_PAIRS_PER_TRIP):
            tile = pair(tile)
        return tile

    tile = lax.fori_loop(0, n_steps // (2 * SWEEP_PAIRS_PER_TRIP), trip, (zero, zero))
    rest = n_steps % (2 * SWEEP_PAIRS_PER_TRIP)
    for _ in range(rest // 2):
        tile = pair(tile)
    if rest % 2:
        t1 = _next_tile(*tile)
        score(*t1, sb)
        consume(*tile, sa)
        consume(*t1, sb)
    else:
        consume(*tile, sa)


def _finish(state, dv):
    _, acc = state
    return (acc[:dv] * (1.0 / acc[dv:dv + 1])).T


def _resume(first, m, acc):
    return jnp.where(first, MASK_VALUE, m), jnp.where(first, 0.0, acc)


def _single_meta_kernel(q_ref, km_ref, xm_ref, vm_ref, o_ref, *, rope_keys):
    q_aug = q_ref[...].T if rope_keys else _q_aug(q_ref[...])
    dv = vm_ref.shape[-1]
    s = _causal(_score(_k_aug(km_ref[...], xm_ref[...]), q_aug))
    state = _consume(s, _v_aug(vm_ref[...]), _init_state(q_aug.shape[1], dv))
    o_ref[...] = _finish(state, dv).astype(o_ref.dtype)


def _single_kernel(q_ref, kr_ref, xr_ref, vr_ref, km_ref, xm_ref, vm_ref, add_ref, o_ref,
                   kar_ref, var_ref, kam_ref, vam_ref, sa_ref, sb_ref, qa_ref, m_ref, acc_ref, *, rope_keys):
    nq, t = kr_ref.shape[0], kr_ref.shape[1]
    dv = vm_ref.shape[-1]
    kam_ref[...] = _k_aug(km_ref[...], xm_ref[...])
    vam_ref[...] = _v_aug(vm_ref[...])
    for blk in range(nq):
        kar_ref[blk] = _k_aug(kr_ref[blk], xr_ref[blk])
        var_ref[blk] = _v_aug(vr_ref[blk])
        q = q_ref[blk * t:(blk + 1) * t, :]
        qa_ref[blk] = q.T if rope_keys else _q_aug(q)

    def score(qb, kb, buf):
        buf[...] = _score(kar_ref[kb], qa_ref[qb])

    def consume(qb, kb, buf):
        s = buf[...] + add_ref[(kb == qb).astype(jnp.int32)]
        m_ref[qb], acc_ref[qb] = _consume(s, var_ref[kb], _resume(kb == 0, m_ref[qb], acc_ref[qb]))

    _flat_sweep(nq, score, consume, (sa_ref, sb_ref))
    s_meta = [_score(kam_ref[...], qa_ref[qb]) for qb in range(nq)]
    for qb in range(nq):
        state = _consume(s_meta[qb], vam_ref[...], (m_ref[qb], acc_ref[qb]))
        o_ref[qb * t:(qb + 1) * t, :] = _finish(state, dv).astype(o_ref.dtype)


def _diff_out(outs, lam_ref, g_ref, lam_init, dv):
    lam = lam_ref[...]
    lam_full = (jnp.exp(jnp.sum(lam[0:1] * lam[1:2], axis=-1, keepdims=True))
                - jnp.exp(jnp.sum(lam[2:3] * lam[3:4], axis=-1, keepdims=True)) + lam_init)
    return _rms(outs[0] - lam_full * outs[1], g_ref[...], dv) * (1.0 - lam_init)


def _diff_meta_kernel(q_ref, km_ref, xm_ref, vm_ref, tm_ref, lam_ref, g_ref, o_ref, *, lam_init):
    d = DIFF_HEAD_DIM
    dv = vm_ref.shape[-1]
    km, xm, v_aug = km_ref[...], xm_ref[...], _v_aug(vm_ref[...])
    outs = []
    for c in range(2):
        q_aug = _q_aug(q_ref[:, c * d:(c + 1) * d])
        s = _score(_k_aug(km[:, c * d:(c + 1) * d], xm), q_aug) + tm_ref[...]
        outs.append(_finish(_consume(s, v_aug, _init_state(q_aug.shape[1], dv)), dv))
    o_ref[...] = _diff_out(outs, lam_ref, g_ref, lam_init, dv).astype(o_ref.dtype)


def _diff_kernel(q_ref, kr_ref, xr_ref, vr_ref, km_ref, xm_ref, vm_ref, tm_ref, add_ref, lam_ref, g_ref, o_ref,
                 kar_ref, var_ref, kam_ref, vam_ref, sa_ref, sb_ref, qa_ref, m_ref, acc_ref, *, lam_init):
    nq, t = kr_ref.shape[0], kr_ref.shape[1]
    d = DIFF_HEAD_DIM
    dv = vm_ref.shape[-1]
    km, xm, xr = km_ref[...], xm_ref[...], xr_ref[...]
    vam_ref[...] = _v_aug(vm_ref[...])
    for c in range(2):
        kam_ref[c] = _k_aug(km[:, c * d:(c + 1) * d], xm)
    for blk in range(nq):
        var_ref[blk] = _v_aug(vr_ref[blk])
        for c in range(2):
            kar_ref[c, blk] = _k_aug(kr_ref[blk, :, c * d:(c + 1) * d], xr)
            qa_ref[c, blk] = _q_aug(q_ref[blk * t:(blk + 1) * t, c * d:(c + 1) * d])

    def score(qb, kb, buf):
        for c in range(2):
            buf[c] = _score(kar_ref[c, kb], qa_ref[c, qb])

    def consume(qb, kb, buf):
        add = add_ref[jnp.clip(kb - qb + 2, 0, 2)]
        for c in range(2):
            m_ref[c, qb], acc_ref[c, qb] = _consume(buf[c] + add, var_ref[kb],
                                                    _resume(kb == 0, m_ref[c, qb], acc_ref[c, qb]))

    _flat_sweep(nq, score, consume, (sa_ref, sb_ref))
    for qb in range(nq):
        outs = []
        for c in range(2):
            s = _score(kam_ref[c], qa_ref[c, qb]) + tm_ref[min(qb, 1)]
            outs.append(_finish(_consume(s, vam_ref[...], (m_ref[c, qb], acc_ref[c, qb])), dv))
        o_ref[qb * t:(qb + 1) * t, :] = _diff_out(outs, lam_ref, g_ref, lam_init, dv).astype(o_ref.dtype)


def _attn_call(kernel, name, b, heads, nq, tq, dv, in_specs, args, scratch=()):
    return pl.pallas_call(
        kernel,
        grid=(b, heads, nq),
        in_specs=in_specs,
        out_specs=pl.BlockSpec((tq, dv), lambda bi, h, qi: (bi * nq + qi, h)),
        out_shape=jax.ShapeDtypeStruct((b * nq * tq, heads * dv), jnp.bfloat16),
        scratch_shapes=list(scratch),
        compiler_params=_params("arbitrary", "arbitrary", "arbitrary"),
        name=name,
    )(*args)


def _aug_scratch(n_comp, nkb, t, dv):
    lead = (n_comp,) if n_comp > 1 else ()
    bf = jnp.bfloat16
    return [pltpu.VMEM(lead + (nkb, t, 2 * LANES), bf), pltpu.VMEM((nkb, dv + BF16_ROWS, t), bf),
            pltpu.VMEM(lead + (META_BLOCK, 2 * LANES), bf), pltpu.VMEM((dv + BF16_ROWS, META_BLOCK), bf),
            pltpu.VMEM(lead + (t, t), jnp.float32), pltpu.VMEM(lead + (t, t), jnp.float32),
            pltpu.VMEM(lead + (nkb, 2 * LANES, t), bf), pltpu.VMEM(lead + (nkb, 1, t), jnp.float32),
            pltpu.VMEM(lead + (nkb, dv + BF16_ROWS, t), jnp.float32)]


def _causal_tiles(t):
    i = lax.broadcasted_iota(jnp.int32, (t, t), 0)
    j = lax.broadcasted_iota(jnp.int32, (t, t), 1)
    return jnp.stack([jnp.zeros((t, t), jnp.float32), jnp.where(i > j, MASK_VALUE, 0.0).astype(jnp.float32)])


def _whole(shape):
    return pl.BlockSpec(shape, lambda bi, h, qi: (0,) * len(shape))


def _qspec(tq, dq, nq, col=lambda h: h):
    return pl.BlockSpec((tq, dq), lambda bi, h, qi: (bi * nq + qi, col(h)))


def _kspec(nkb, tk, d, col):
    return pl.BlockSpec((None, nkb, tk, d), lambda bi, h, qi: (bi, 0, 0, col(h)))


def _mspec(d, col):
    return pl.BlockSpec((META_BLOCK, d), lambda bi, h, qi: (0, col(h)))


def fox_attention(qkv, qkv_meta, extra, extra_meta, b, seq, t=ATTN_TILE):
    hd = FOX_HEADS
    xm_spec = pl.BlockSpec((None, META_BLOCK, LANES), lambda bi, h, qi: (h, 0, 0))
    meta_specs = [_mspec(FOX_DIM, lambda h: hd + h), xm_spec, _mspec(FOX_DIM, lambda h: 2 * hd + h)]
    meta_args = (qkv_meta, extra_meta, qkv_meta)
    if qkv is None:
        return _attn_call(functools.partial(_single_meta_kernel, rope_keys=False),
                          "fox_attention_meta", 1, hd, 1, META_BLOCK, FOX_DIM,
                          [_qspec(META_BLOCK, FOX_DIM, 1)] + meta_specs, (qkv_meta,) + meta_args)
    t = _tile(seq, t)
    nq = seq // t
    kv4 = qkv.reshape(b, nq, t, qkv.shape[-1])
    return _attn_call(
        functools.partial(_single_kernel, rope_keys=False), "fox_attention",
        b, hd, 1, seq, FOX_DIM,
        [_qspec(seq, FOX_DIM, 1), _kspec(nq, t, FOX_DIM, lambda h: hd + h),
         pl.BlockSpec((None, None, nq, t, LANES), lambda bi, h, qi: (bi, h, 0, 0, 0)),
         _kspec(nq, t, FOX_DIM, lambda h: 2 * hd + h)] + meta_specs + [_whole((2, t, t))],
        (qkv, kv4, extra.reshape(b, hd, nq, t, LANES), kv4) + meta_args + (_causal_tiles(t),),
        _aug_scratch(1, nq, t, FOX_DIM))


def mla_attention(q, kv, kr, q_meta, kv_meta, kr_meta, b, seq, t=ATTN_TILE):
    hd = MLA_HEADS
    meta_specs = [_mspec(LANES, lambda h: 2 * h), _mspec(LANES, lambda h: 0), _mspec(LANES, lambda h: 2 * h + 1)]
    meta_args = (kv_meta, kr_meta, kv_meta)
    if q is None:
        return _attn_call(functools.partial(_single_meta_kernel, rope_keys=True),
                          "mla_attention_meta", 1, hd, 1, META_BLOCK, MLA_V_DIM,
                          [_qspec(META_BLOCK, 2 * LANES, 1)] + meta_specs, (q_meta,) + meta_args)
    t = _tile(seq, t)
    nq = seq // t
    kv4 = kv.reshape(b, nq, t, kv.shape[-1])
    kr4 = kr.reshape(b, nq, t, kr.shape[-1])
    return _attn_call(
        functools.partial(_single_kernel, rope_keys=True), "mla_attention",
        b, hd, 1, seq, MLA_V_DIM,
        [_qspec(seq, 2 * LANES, 1), _kspec(nq, t, LANES, lambda h: 2 * h),
         _kspec(nq, t, LANES, lambda h: 0), _kspec(nq, t, LANES, lambda h: 2 * h + 1)] + meta_specs
        + [_whole((2, t, t))],
        (q, kv4, kr4, kv4) + meta_args + (_causal_tiles(t),), _aug_scratch(1, nq, t, MLA_V_DIM))


def diff_attention(qkv, qkv_meta, extra, extra_meta, tiles, lam, subln, lam_init, b, seq, t=ATTN_TILE):
    hd = DIFF_HEADS
    dv = 2 * DIFF_HEAD_DIM
    tile_meta_only, tile_meta, tile_real = tiles
    lam_spec = pl.BlockSpec((4, DIFF_HEAD_DIM), lambda bi, h, qi: (0, 0))
    g_spec = pl.BlockSpec((1, dv), lambda bi, h, qi: (0, 0))
    kcol, vcol = (lambda h: hd + h), (lambda h: 2 * hd + h)
    per_head = lambda r, c: pl.BlockSpec((None, r, c), lambda bi, h, qi: (h, 0, 0))
    meta_specs = [_mspec(dv, kcol), per_head(META_BLOCK, LANES), _mspec(dv, vcol)]
    meta_args = (qkv_meta, extra_meta, qkv_meta)
    tail = (lam, subln.reshape(1, dv))
    if qkv is None:
        return _attn_call(
            functools.partial(_diff_meta_kernel, lam_init=lam_init),
            "diff_attention_meta", 1, hd, 1, META_BLOCK, dv,
            [_qspec(META_BLOCK, dv, 1)] + meta_specs + [per_head(META_BLOCK, META_BLOCK), lam_spec, g_spec],
            (qkv_meta,) + meta_args + (tile_meta_only,) + tail)
    t = _tile(seq, t)
    nq = seq // t
    kv4 = qkv.reshape(b, nq, t, qkv.shape[-1])
    return _attn_call(
        functools.partial(_diff_kernel, lam_init=lam_init),
        "diff_attention", b, hd, 1, seq, dv,
        [_qspec(seq, dv, 1), _kspec(nq, t, dv, kcol), per_head(t, LANES), _kspec(nq, t, dv, vcol)] + meta_specs
        + [pl.BlockSpec((None, 2, META_BLOCK, t), lambda bi, h, qi: (h, 0, 0, 0)),
           pl.BlockSpec((None, 3, t, t), lambda bi, h, qi: (h, 0, 0, 0)), lam_spec, g_spec],
        (qkv, kv4, extra, kv4) + meta_args + (tile_meta, tile_real) + tail,
        _aug_scratch(2, nq, t, dv))


def _front_pad(x):
    return jnp.pad(x, ((META_BLOCK - x.shape[0], 0), (0, 0)))


def _split3(x):
    def head(v):
        bits = lax.bitcast_convert_type(v, jnp.uint32) & jnp.uint32(0xFFFF0000)
        return lax.bitcast_convert_type(bits, jnp.float32)

    hi = head(x)
    mid = head(x - hi)
    lo = head(x - hi - mid)
    return tuple(p.astype(jnp.bfloat16) for p in (hi, mid, lo))


def _extra_lanes(x, mask=None):
    pieces = list(_split3(x)) + [jnp.zeros(x.shape, jnp.bfloat16) if mask is None else mask.astype(jnp.bfloat16)]
    out = jnp.stack(pieces, axis=-1)
    return jnp.pad(out, [(0, 0)] * x.ndim + [(0, LANES - BIAS_SLOTS)])


def _pad_mask(heads):
    m = jnp.where(jnp.arange(META_BLOCK) < META_BLOCK - N_META, MASK_VALUE, 0.0).astype(jnp.float32)
    return jnp.broadcast_to(m, (heads, META_BLOCK))


def _rope_tables(n_pos):
    inv = ROPE_THETA ** (-jnp.arange(0, MLA_ROPE_DIM, 2, dtype=jnp.float32) / MLA_ROPE_DIM)
    ang = jnp.arange(n_pos, dtype=jnp.float32)[:, None] * inv[None, :]
    cos, sin = jnp.cos(ang), jnp.sin(ang)
    z32 = jnp.zeros_like(cos)
    z64 = jnp.zeros((n_pos, LANES - MLA_ROPE_DIM), jnp.float32)
    c = jnp.concatenate([cos, cos, z64], axis=1)
    sa = jnp.concatenate([-sin, z32, z64], axis=1)
    sb = jnp.concatenate([z32, sin, z64], axis=1)
    return c, sa, sb


def _col_scale(n, n_scaled, value):
    return jnp.where(jnp.arange(n) < n_scaled, value, 1.0).astype(jnp.float32)[None, :]


def _even_weights(w_in, w_uq, w_ukv, gkv):
    o = [0]
    for s in (FOX_HEADS * FOX_DIM,) * 3 + (FOX_HEADS, MLA_Q_RANK, MLA_KV_RANK, MLA_ROPE_DIM):
        o.append(o[-1] + s)
    bf = jnp.bfloat16
    mla_scale = (MLA_NOPE_DIM + MLA_ROPE_DIM) ** -0.5 * LOG2E
    w_qkv = jnp.concatenate([w_in[:, :o[1]] * (FOX_DIM ** -0.5 * LOG2E), w_in[:, o[1]:o[3]]], axis=1).astype(bf)
    pad = lambda w, n: jnp.pad(w, ((0, 0), (0, n - w.shape[1])))
    w_small = jnp.concatenate([w_in[:, o[4]:o[5]], pad(w_in[:, o[5]:o[6]], KV_RANK_PAD),
                               pad(w_in[:, o[6]:o[7]], LANES), pad(w_in[:, o[3]:o[4]], LANES)],
                              axis=1).astype(bf)
    hq = (w_uq * mla_scale).reshape(MLA_Q_RANK, MLA_HEADS, MLA_NOPE_DIM + MLA_ROPE_DIM)
    hq = jnp.pad(hq, ((0, 0), (0, 0), (0, 2 * LANES - hq.shape[-1])))
    w_uq_p = hq.reshape(MLA_Q_RANK, MLA_HEADS * 2 * LANES).astype(bf)
    w_ukv_p = jnp.pad(w_ukv, ((0, KV_RANK_PAD - MLA_KV_RANK), (0, 0))).astype(bf)
    gkv_p = jnp.pad(gkv, (0, KV_RANK_PAD - MLA_KV_RANK))
    return w_qkv, w_small, w_uq_p, w_ukv_p, gkv_p


def _even_mixer(hn, hn_m, b, seq, i, w_in, b_f, gq, gkv, w_uq, w_ukv, ev_w_out, tabs, tabs_m):
    w_qkv, w_small, w_uq_p, w_ukv_p, gkv_p = _even_weights(w_in, w_uq, w_ukv, gkv)
    f0 = MLA_Q_RANK + KV_RANK_PAD + LANES
    qkv, qkv_m = matmul(hn, w_qkv, jnp.bfloat16), matmul(hn_m, w_qkv, jnp.bfloat16)

    def stream(x, tab, n_pos):
        small = matmul(x, w_small, jnp.float32, tn=EVEN_SMALL // 2)
        cq, ckv, kr = mla_latent(small, gq, gkv_p, tab, n_pos)
        q = uq_rope(cq, w_uq_p, tab, n_pos)
        kv = matmul(ckv, w_ukv_p, jnp.bfloat16)
        return small[:, f0:f0 + FOX_HEADS], q, kv, kr

    fl, q, kv, kr = stream(hn, tabs, seq)
    fl_m, q_m, kv_m, kr_m = stream(hn_m, tabs_m, N_META)

    cum = forget_cumsum(fl.reshape(b, seq, FOX_HEADS).transpose(0, 2, 1), b_f)
    fl_m = jnp.pad(fl_m.T, ((0, 0), (0, LANES - N_META)))[None]
    cum_m = forget_cumsum(fl_m, b_f)[0, :, :N_META]
    mask = _pad_mask(FOX_HEADS)
    front = lambda x: jnp.pad(x, ((0, 0), (META_BLOCK - N_META, 0)))
    extra = _extra_lanes(-LOG2E * cum)
    extra_meta = _extra_lanes(front(LOG2E * (cum_m[:, -1:] - cum_m)), mask)
    extra_meta_only = _extra_lanes(front(-LOG2E * cum_m), mask)

    qkv_mp, q_mp, kv_mp, kr_mp = map(_front_pad, (qkv_m, q_m, kv_m, kr_m))
    kr_mp = kr_mp.at[:META_BLOCK - N_META, MLA_MASK_LANE].set(MASK_VALUE)
    o_f = fox_attention(qkv, qkv_mp, extra, extra_meta, b, seq)
    o_f_m = fox_attention(None, qkv_mp, None, extra_meta_only, b, seq)
    o_m = mla_attention(q, kv, kr, q_mp, kv_mp, kr_mp, b, seq)
    o_m_m = mla_attention(None, None, None, q_mp, kv_mp, kr_mp, b, seq)
    o = jnp.concatenate([o_f, o_m], axis=1)
    o_meta = jnp.concatenate([o_f_m, o_m_m], axis=1)[-N_META:]
    return matmul_ws(o, o_meta, ev_w_out, i, D_MODEL, jnp.float32)


def _diff_mixer(hn, hn_m, b, seq, i, od_w_in, lam, subln, od_w_out, bias, lam_init):
    n_q = DIFF_HEADS * 2 * DIFF_HEAD_DIM
    extra, extra_meta, tiles = bias
    qkv, qkv_m = matmul_ws(hn, hn_m, od_w_in, i, 3 * n_q, jnp.bfloat16,
                           _col_scale(3 * n_q, n_q, DIFF_HEAD_DIM ** -0.5 * LOG2E))
    qkv_mp = _front_pad(qkv_m)
    o = diff_attention(qkv, qkv_mp, extra, extra_meta, tiles, lam, subln, lam_init, b, seq)
    o_meta = diff_attention(None, qkv_mp, None, extra_meta, tiles, lam, subln, lam_init, b, seq)
    return matmul_ws(o, o_meta[-N_META:], od_w_out, i, D_MODEL, jnp.float32)


def _diff_bias(rel_bias, t):
    rb_t = rel_bias.astype(jnp.float32).T
    far = LOG2E * rb_t[:, REL_BUCKETS - 1]
    extra = _extra_lanes(jnp.broadcast_to(far[:, None], (DIFF_HEADS, t)))
    extra_meta = _extra_lanes(jnp.broadcast_to(far[:, None], (DIFF_HEADS, META_BLOCK)), _pad_mask(DIFF_HEADS))
    tile_meta0 = t5_tiles(rb_t, META_BLOCK, t, META_BLOCK, False)
    tile_sub, tile_diag = t5_tiles(rb_t, t, t, t, False), t5_tiles(rb_t, t, t, 0, True)
    tiles = (t5_tiles(rb_t, META_BLOCK, META_BLOCK, 0, True),
             jnp.stack([tile_meta0, jnp.zeros_like(tile_meta0)], axis=1),
             jnp.stack([jnp.zeros_like(tile_sub), tile_sub, tile_diag], axis=1))
    return extra, extra_meta, tiles


def kernel(x, meta_tokens, rel_bias, ev_w_in, ev_b_f, ev_q_norm, ev_kv_norm, ev_w_uq, ev_w_ukv, ev_w_out,
           od_w_in, od_lambda, od_subln, od_w_out, norm_g, ffn_w_gate, ffn_w_up, ffn_w_down):
    b, seq, d = x.shape
    hs = x.reshape(b * seq, d)
    hs_m = meta_tokens.astype(x.dtype)

    c, sa, sb = _rope_tables(N_META + seq)
    tabs_m = tuple(tb[:N_META] for tb in (c, sa, sb))
    tabs = tuple(tb[N_META:] for tb in (c, sa, sb))
    bias = _diff_bias(rel_bias, _tile(seq, ATTN_TILE))

    hn = rmsnorm_bf16(hs, norm_g[0, 0])
    hn_m = rmsnorm_bf16(hs_m, norm_g[0, 0])
    for layer in range(DEPTH):
        g = norm_g[layer]
        i = layer // 2
        if layer % 2 == 0:
            m, m_m = _even_mixer(hn, hn_m, b, seq, i, ev_w_in[i], ev_b_f[i], ev_q_norm[i], ev_kv_norm[i],
                                 ev_w_uq[i], ev_w_ukv[i], ev_w_out, tabs, tabs_m)
        else:
            lam_init = 0.8 - 0.6 * math.exp(-0.3 * layer)
            m, m_m = _diff_mixer(hn, hn_m, b, seq, i, od_w_in, od_lambda[i], od_subln[i], od_w_out,
                                 bias, lam_init)
        hs, hn = resid_norm(hs, m, g[1], g[2])
        hs_m, hn_m = resid_norm(hs_m, m_m, g[1], g[2])
        h, h_m = ffn_up(hn, hn_m, ffn_w_gate, ffn_w_up, layer)
        wd = ffn_w_down[layer].astype(jnp.bfloat16)
        f, f_m = matmul_ktiled(h, wd), matmul_ktiled(h_m, wd)
        g_next = norm_g[layer + 1, 0] if layer + 1 < DEPTH else None
        hs, hn = resid_norm(hs, f, g[3], g_next)
        if g_next is not None:
            hs_m, hn_m = resid_norm(hs_m, f_m, g[3], g_next)
    return hs.reshape(b, seq, d)
```
